```python
import jax, jax.numpy as jnp
from jax import lax
import numpy as np

D_MODEL = 1024
BATCH = 2
SEQ = 16384
DEPTH = 1
DEC_BATCH = 128
DEC_SEQ = 8
PAST_LEN = 8192
PAGE_SIZE = 128

D_MIX = D_MODEL
HEAD_DIM = 64
ATT_WIDTH = D_MIX // 2
N_HEADS = ATT_WIDTH // HEAD_DIM
N_KV_HEADS = N_HEADS // 2
N_IDX_HEADS = 8
D_IDX = 64
TOPK_MAX = 256
GM_WIDTH = D_MIX - ATT_WIDTH
GM_GROUPS = 8
GM_GROUP_DIM = GM_WIDTH // GM_GROUPS
CHUNK = 128
Q_BLOCK = 128
PLE_DIM = 256
ROPE_THETA = 500000.0
ROPE_FRACTION = 4
EPS = 1e-6
SPLIT_SIZES = (N_HEADS * HEAD_DIM, N_KV_HEADS * HEAD_DIM, N_KV_HEADS * HEAD_DIM,
               N_IDX_HEADS * D_IDX, D_IDX, N_IDX_HEADS, ATT_WIDTH,
               GM_WIDTH, GM_WIDTH, GM_WIDTH)
D_IN = sum(SPLIT_SIZES)

kernel_name = "hybrid_dsa_gmlp_decoder_step"


def rmsnorm(x, g):
    x32 = x.astype(jnp.float32)
    y = x32 * lax.rsqrt(jnp.mean(x32 * x32, axis=-1, keepdims=True) + EPS)
    return y.astype(x.dtype) * g


def layernorm(x, g, b):
    x32 = x.astype(jnp.float32)
    xc = x32 - jnp.mean(x32, axis=-1, keepdims=True)
    y = xc * lax.rsqrt(jnp.mean(xc * xc, axis=-1, keepdims=True) + EPS)
    return y.astype(x.dtype) * g + b


def partial_rope(x, pos):
    rot = x.shape[-1] // ROPE_FRACTION
    half = rot // 2
    inv = jnp.power(ROPE_THETA, -jnp.arange(half, dtype=jnp.float32) * 2.0 / rot)
    ang = pos.astype(jnp.float32)[:, None] * inv[None, :]
    cos = jnp.cos(ang)[None, :, None, :]
    sin = jnp.sin(ang)[None, :, None, :]
    xr = x[..., :rot].astype(jnp.float32)
    x1, x2 = xr[..., :half], xr[..., half:]
    out = jnp.concatenate([x1 * cos - x2 * sin, x2 * cos + x1 * sin], axis=-1).astype(x.dtype)
    return jnp.concatenate([out, x[..., rot:]], axis=-1)


def chunk_spatial_gate(vn, w_s, b_s):
    B, T = vn.shape[:2]
    n_c = -(-T // CHUNK)
    pad = n_c * CHUNK - T
    vp = jnp.pad(vn, ((0, 0), (0, pad), (0, 0), (0, 0))).reshape(B, n_c, CHUNK, GM_GROUPS, GM_GROUP_DIM)
    causal = jnp.tril(jnp.ones((CHUNK, CHUNK), dtype=bool))
    ws = jnp.where(causal[None], w_s, 0)
    s = jnp.einsum('gij,bcjgd->bcigd', ws, vp) + b_s.T[None, None, :, :, None]
    return s.reshape(B, n_c * CHUNK, GM_GROUPS, GM_GROUP_DIM)[:, :T]


def project(x, pos, norm_g, w_in, qn_g, kn_g, ln_g, ln_b, w_s, b_s):
    B, T, _ = x.shape
    h = rmsnorm(x, norm_g)
    z = h @ w_in
    offs = np.cumsum(SPLIT_SIZES)[:-1].tolist()
    q, k, v, qi, ki, wi, ga, u, vm, gm = jnp.split(z, offs, axis=-1)
    q = partial_rope(rmsnorm(q.reshape(B, T, N_HEADS, HEAD_DIM), qn_g), pos)
    k = partial_rope(rmsnorm(k.reshape(B, T, N_KV_HEADS, HEAD_DIM), kn_g), pos)
    v = v.reshape(B, T, N_KV_HEADS, HEAD_DIM)
    qi = partial_rope(qi.reshape(B, T, N_IDX_HEADS, D_IDX), pos)
    ki = partial_rope(ki[:, :, None, :], pos)[:, :, 0, :]
    wi = wi * (N_IDX_HEADS * D_IDX) ** -0.5
    vn = layernorm(jax.nn.gelu(vm), ln_g, ln_b)
    s = chunk_spatial_gate(vn.reshape(B, T, GM_GROUPS, GM_GROUP_DIM), w_s, b_s).reshape(B, T, GM_WIDTH)
    gm_out = jax.nn.gelu(u) * s * jax.nn.silu(gm)
    return q, k, v, qi, ki, wi, ga, gm_out, vn


def indexer_scores(qi, wi, ki):
    dots = jnp.einsum('bthd,bld->bthl', qi, ki, preferred_element_type=jnp.float32)
    return jnp.einsum('bth,bthl->btl', wi.astype(jnp.float32), jax.nn.relu(dots))


def select_keys(score, q_pos, k_top):
    L = score.shape[-1]
    allowed = jnp.arange(L)[None, None, :] <= q_pos[None, :, None]
    score = jnp.where(allowed, score, -jnp.inf)
    _, idx = lax.top_k(score, k_top)
    valid = idx <= q_pos[None, :, None]
    return idx, valid


def sparse_attend(q, kg, vg, valid):
    B, T = q.shape[:2]
    qg = q.reshape(B, T, N_KV_HEADS, N_HEADS // N_KV_HEADS, HEAD_DIM)
    s = jnp.einsum('btgrd,btkgd->btgrk', qg, kg, preferred_element_type=jnp.float32) * HEAD_DIM ** -0.5
    s = jnp.where(valid[:, :, None, None, :], s, -jnp.inf)
    p = jax.nn.softmax(s, axis=-1).astype(vg.dtype)
    o = jnp.einsum('btgrk,btkgd->btgrd', p, vg)
    return o.reshape(B, T, N_HEADS, HEAD_DIM)


def prompt_attention(q, k, v, qi, ki, wi):
    B, S = q.shape[:2]
    nb = S // Q_BLOCK
    k_top = min(TOPK_MAX, S // 4)
    take = jax.vmap(lambda a, ii: a[ii])

    def blocks(a):
        return jnp.moveaxis(a.reshape((B, nb, Q_BLOCK) + a.shape[2:]), 1, 0)

    pos_blocks = jnp.arange(S).reshape(nb, Q_BLOCK)

    def one_block(args):
        qb, qib, wb, posb = args
        score = indexer_scores(qib, wb, ki)
        idx, valid = select_keys(score, posb, k_top)
        return sparse_attend(qb, take(k, idx), take(v, idx), valid)

    o = lax.map(one_block, (blocks(q), blocks(qi), blocks(wi), pos_blocks))
    return jnp.moveaxis(o, 0, 1).reshape(q.shape)


def sample_attention(q, k_new, v_new, qi, ki_new, wi, cache_k, cache_v, cache_idx_k, layer, page_table):
    DB, T = q.shape[:2]
    past = page_table.shape[1] * PAGE_SIZE
    L = past + T
    ki_past = cache_idx_k[layer, page_table].reshape(DB, past, D_IDX)
    ki_all = jnp.concatenate([ki_past, ki_new.astype(ki_past.dtype)], axis=1)
    q_pos = past + jnp.arange(T)
    score = indexer_scores(qi, wi, ki_all)
    idx, valid = select_keys(score, q_pos, min(TOPK_MAX, L // 4))
    is_past = (idx < past)[..., None, None]
    j = jnp.minimum(idx, past - 1)
    phys = jax.vmap(lambda pt, jj: pt[jj // PAGE_SIZE])(page_table, j)
    off = j % PAGE_SIZE
    jn = jnp.clip(idx - past, 0, T - 1)
    take = jax.vmap(lambda a, ii: a[ii])
    kg = jnp.where(is_past, cache_k[layer, phys, off], take(k_new, jn))
    vg = jnp.where(is_past, cache_v[layer, phys, off], take(v_new, jn))
    return sparse_attend(q, kg, vg, valid)


def finish(x, attn_o, ga, gm_out, w_out, p, ple_norm_g, w_ple_gate, w_ple_proj):
    B, T = x.shape[:2]
    mix = jnp.concatenate([attn_o.reshape(B, T, ATT_WIDTH) * jax.nn.silu(ga), gm_out], axis=-1)
    r = x + mix @ w_out
    gate = jax.nn.sigmoid(rmsnorm(r, ple_norm_g) @ w_ple_gate)
    return r + gate * (p @ w_ple_proj)


def setup_inputs(seed: int = 0) -> dict:
    key = jax.random.key(seed)
    ks = jax.random.split(key, 24)
    n_pages = PAST_LEN // PAGE_SIZE
    used = DEC_BATCH * n_pages
    n_pool = used + used // 4
    nrm = jax.random.normal
    f32 = jnp.float32
    page_table = jax.random.permutation(ks[0], n_pool)[:used].reshape(DEC_BATCH, n_pages).astype(jnp.int32)
    return {
        "x_prompt": nrm(ks[1], (BATCH, SEQ, D_MODEL), f32),
        "x_sample": nrm(ks[2], (DEC_BATCH, DEC_SEQ, D_MODEL), f32),
        "cache_k": nrm(ks[3], (DEPTH, n_pool, PAGE_SIZE, N_KV_HEADS, HEAD_DIM), f32),
        "cache_v": nrm(ks[4], (DEPTH, n_pool, PAGE_SIZE, N_KV_HEADS, HEAD_DIM), f32),
        "cache_idx_k": nrm(ks[5], (DEPTH, n_pool, PAGE_SIZE, D_IDX), f32),
        "page_table": page_table,
        "p_prompt": nrm(ks[6], (DEPTH, BATCH, SEQ, PLE_DIM), f32),
        "p_sample": nrm(ks[7], (DEPTH, DEC_BATCH, DEC_SEQ, PLE_DIM), f32),
        "norm_in_g": 1.0 + 0.01 * nrm(ks[8], (DEPTH, D_MODEL), f32),
        "w_in": nrm(ks[9], (DEPTH, D_MODEL, D_IN), f32) * D_MODEL ** -0.5,
        "q_norm_g": 1.0 + 0.01 * nrm(ks[10], (DEPTH, HEAD_DIM), f32),
        "k_norm_g": 1.0 + 0.01 * nrm(ks[11], (DEPTH, HEAD_DIM), f32),
        "ln_v_g": 1.0 + 0.01 * nrm(ks[12], (DEPTH, GM_WIDTH), f32),
        "ln_v_b": 0.01 * nrm(ks[13], (DEPTH, GM_WIDTH), f32),
        "w_s": nrm(ks[14], (DEPTH, GM_GROUPS, CHUNK, CHUNK), f32) * 0.5 * CHUNK ** -0.5,
        "b_s": 1.0 + 0.01 * nrm(ks[15], (DEPTH, GM_GROUPS, CHUNK), f32),
        "w_out": nrm(ks[16], (DEPTH, D_MIX, D_MODEL), f32) * D_MIX ** -0.5,
        "ple_norm_g": 1.0 + 0.01 * nrm(ks[17], (DEPTH, D_MODEL), f32),
        "w_ple_gate": nrm(ks[18], (DEPTH, D_MODEL, D_MODEL), f32) * D_MODEL ** -0.5,
        "w_ple_proj": nrm(ks[19], (DEPTH, PLE_DIM, D_MODEL), f32) * PLE_DIM ** -0.5,
    }


def reference(x_prompt, x_sample, cache_k, cache_v, cache_idx_k, page_table, p_prompt, p_sample,
              norm_in_g, w_in, q_norm_g, k_norm_g, ln_v_g, ln_v_b, w_s, b_s, w_out,
              ple_norm_g, w_ple_gate, w_ple_proj):
    S = x_prompt.shape[1]
    T = x_sample.shape[1]
    past = page_table.shape[1] * PAGE_SIZE
    pos_p = jnp.arange(S)
    pos_s = past + jnp.arange(T)
    hp, hs = x_prompt, x_sample
    kp, vp, ikp, ksl, vsl, iks, gvs = [], [], [], [], [], [], []
    for i in range(DEPTH):
        lw = (norm_in_g[i], w_in[i], q_norm_g[i], k_norm_g[i], ln_v_g[i], ln_v_b[i], w_s[i], b_s[i])
        q, k, v, qi, ki, wi, ga, gm, vn = project(hp, pos_p, *lw)
        o = prompt_attention(q, k, v, qi, ki, wi)
        hp = finish(hp, o, ga, gm, w_out[i], p_prompt[i], ple_norm_g[i], w_ple_gate[i], w_ple_proj[i])
        kp.append(k)
        vp.append(v)
        ikp.append(ki)
        q, k, v, qi, ki, wi, ga, gm, vn = project(hs, pos_s, *lw)
        o = sample_attention(q, k, v, qi, ki, wi, cache_k, cache_v, cache_idx_k, i, page_table)
        hs = finish(hs, o, ga, gm, w_out[i], p_sample[i], ple_norm_g[i], w_ple_gate[i], w_ple_proj[i])
        ksl.append(k)
        vsl.append(v)
        iks.append(ki)
        gvs.append(vn)
    return (hp, hs, jnp.stack(kp), jnp.stack(vp), jnp.stack(ikp),
            jnp.stack(ksl), jnp.stack(vsl), jnp.stack(iks), jnp.stack(gvs))
```

```python
import functools

import numpy as np
import jax
import jax.numpy as jnp
from jax import lax
from jax.experimental import pallas as pl
from jax.experimental.pallas import tpu as pltpu

F32 = jnp.float32
BF16 = jnp.bfloat16
I32 = jnp.int32

HEAD_DIM = 64
N_HEADS = 8
N_KV_HEADS = 4
N_IDX_HEADS = 8
D_IDX = 64
TOPK_MAX = 256
GM_GROUPS = 8
CHUNK = 128
PAGE_SIZE = 128
ROPE_THETA = 500000.0
ROPE_ROT = HEAD_DIM // 4
ROPE_HALF = ROPE_ROT // 2
EPS = 1e-6

ATT_WIDTH = N_HEADS * HEAD_DIM
KV_WIDTH = N_KV_HEADS * HEAD_DIM
IDX_WIDTH = N_IDX_HEADS * D_IDX
GM_WIDTH = GM_GROUPS * 64
LANES = 128
KEY_TILE = 256
Q_BLOCK = 128
ROW_TILE = 256
VMEM_LIMIT = 56 * 1024 * 1024

C_Q, C_K, C_V, C_QI, C_GA, C_U, C_VM, C_GM, C_KIWI, C_END = (
    0, 512, 768, 1024, 1536, 2048, 2560, 3072, 3584, 3712)
_ORIG_KI, _ORIG_GA, _ORIG_END = 1536, 1608, 3656
NEG_INF = float("-inf")
INT_MIN = -2 ** 31


def _head_rms(xh, seg, g):
    sq = xh * xh
    hi = sq.astype(BF16)
    lo = (sq - hi.astype(F32)).astype(BF16)
    ms = (jnp.dot(hi, seg, preferred_element_type=F32)
          + jnp.dot(lo, seg, preferred_element_type=F32))
    return xh * lax.rsqrt(ms + EPS) * g


def _rope(xh, cos, sa, sb):
    w = xh.shape[-1]
    return xh * cos + pltpu.roll(xh, w - ROPE_HALF, 1) * sa + pltpu.roll(xh, ROPE_HALF, 1) * sb


def _tile_lanes(t, width):
    reps = width // t.shape[-1]
    return t if reps == 1 else jnp.concatenate([t] * reps, axis=1)


def _project_body(emit_vn, x_ref, ng_ref, w_ref, qg_ref, kg_ref, lng_ref, lnb_ref, mixw_ref,
                  bias_ref, cos_ref, sa_ref, sb_ref, seg_ref, *outs):
    q_o, k_o, v_o, kb_o, vb_o, qi_o, kiwi_o, kib_o, ga_o, gm_o = outs[:10]
    x = x_ref[...]
    h = x * lax.rsqrt(jnp.mean(x * x, axis=-1, keepdims=True) + EPS) * ng_ref[...]
    hb = h.astype(BF16)

    def proj(c0, c1):
        return jnp.dot(hb, w_ref[:, c0:c1], preferred_element_type=F32)

    cos, sa, sb = cos_ref[...], sa_ref[...], sb_ref[...]
    seg = seg_ref[...]

    q = _head_rms(proj(C_Q, C_K), seg, qg_ref[...])
    q = _rope(q, _tile_lanes(cos, ATT_WIDTH), _tile_lanes(sa, ATT_WIDTH), _tile_lanes(sb, ATT_WIDTH))
    q_o[...] = (q * HEAD_DIM ** -0.5).astype(BF16)

    k = _head_rms(proj(C_K, C_V), seg[:KV_WIDTH, :KV_WIDTH], kg_ref[...])
    k = _rope(k, _tile_lanes(cos, KV_WIDTH), _tile_lanes(sa, KV_WIDTH), _tile_lanes(sb, KV_WIDTH))
    k_o[...] = k
    kb_o[...] = k.astype(BF16)

    v = proj(C_V, C_QI)
    v_o[...] = v
    vb_o[...] = v.astype(BF16)

    qi = _rope(proj(C_QI, C_GA), _tile_lanes(cos, IDX_WIDTH), _tile_lanes(sa, IDX_WIDTH),
               _tile_lanes(sb, IDX_WIDTH))
    qi_o[...] = qi.astype(BF16)

    lane = lax.broadcasted_iota(I32, cos.shape, 1)
    is_ki = lane < D_IDX
    wi_scale = jnp.where(lane < D_IDX + N_IDX_HEADS, IDX_WIDTH ** -0.5, 1.0)
    kiwi = _rope(proj(C_KIWI, C_END), jnp.where(is_ki, cos, wi_scale),
                 jnp.where(is_ki, sa, 0.0), jnp.where(is_ki, sb, 0.0))
    kiwi_o[...] = kiwi
    kib_o[...] = kiwi.astype(BF16)

    ga_o[...] = jax.nn.silu(proj(C_GA, C_U))

    vmg = jax.nn.gelu(proj(C_VM, C_GM))
    xc = vmg - jnp.mean(vmg, axis=-1, keepdims=True)
    vn = xc * lax.rsqrt(jnp.mean(xc * xc, axis=-1, keepdims=True) + EPS) * lng_ref[...] + lnb_ref[...]
    if emit_vn:
        outs[10][...] = vn

    lane_c = lax.broadcasted_iota(I32, (CHUNK, LANES), 1)
    rows = x.shape[0]
    chunks = []
    for c in range(rows // CHUNK):
        pieces = []
        for p in range(GM_GROUPS // 2):
            t = vn[c * CHUNK:(c + 1) * CHUNK, p * LANES:(p + 1) * LANES]
            rhs = jnp.concatenate([jnp.where(lane_c < 64, t, 0.0), jnp.where(lane_c >= 64, t, 0.0)],
                                  axis=0).astype(BF16)
            pieces.append(jnp.dot(mixw_ref[p], rhs, preferred_element_type=F32))
        chunks.append(jnp.concatenate(pieces, axis=1) + bias_ref[...])
    s = jnp.concatenate(chunks, axis=0)
    gm_o[...] = jax.nn.gelu(proj(C_U, C_VM)) * s * jax.nn.silu(proj(C_GM, C_KIWI))


def _project(x2, pos_tables, mixw, bias, weights, emit_vn):
    rows = x2.shape[0]
    tm = min(ROW_TILE, rows)
    assert rows % tm == 0 and tm % CHUNK == 0
    ng, w, qg, kg, lng, lnb, seg = weights
    cos, sa, sb = pos_tables
    assert cos.shape[0] % tm == 0
    n_pos_blocks = cos.shape[0] // tm

    def row_map(i):
        return (i, 0)

    def pos_map(i):
        return (i % n_pos_blocks, 0)

    def const(i):
        return (0, 0)

    def rspec(width):
        return pl.BlockSpec((tm, width), row_map)

    def cspec(a):
        if a.ndim == 3:
            return pl.BlockSpec(a.shape, lambda i: (0, 0, 0))
        return pl.BlockSpec(a.shape, const)

    out_shapes = [
        jax.ShapeDtypeStruct((rows, ATT_WIDTH), BF16),
        jax.ShapeDtypeStruct((rows, KV_WIDTH), F32),
        jax.ShapeDtypeStruct((rows, KV_WIDTH), F32),
        jax.ShapeDtypeStruct((rows, KV_WIDTH), BF16),
        jax.ShapeDtypeStruct((rows, KV_WIDTH), BF16),
        jax.ShapeDtypeStruct((rows, IDX_WIDTH), BF16),
        jax.ShapeDtypeStruct((rows, LANES), F32),
        jax.ShapeDtypeStruct((rows, LANES), BF16),
        jax.ShapeDtypeStruct((rows, ATT_WIDTH), F32),
        jax.ShapeDtypeStruct((rows, GM_WIDTH), F32),
    ]
    if emit_vn:
        out_shapes.append(jax.ShapeDtypeStruct((rows, GM_WIDTH), F32))
    out_specs = [rspec(s.shape[1]) for s in out_shapes]
    in_specs = [rspec(x2.shape[1]), cspec(ng), cspec(w), cspec(qg), cspec(kg), cspec(lng), cspec(lnb),
                cspec(mixw), cspec(bias),
                pl.BlockSpec((tm, LANES), pos_map), pl.BlockSpec((tm, LANES), pos_map),
                pl.BlockSpec((tm, LANES), pos_map), cspec(seg)]
    return pl.pallas_call(
        functools.partial(_project_body, emit_vn),
        grid=(rows // tm,),
        in_specs=in_specs,
        out_specs=out_specs,
        out_shape=out_shapes,
        compiler_params=pltpu.CompilerParams(dimension_semantics=("arbitrary",),
                                             vmem_limit_bytes=VMEM_LIMIT),
        name="project",
    )(x2, ng, w, qg, kg, lng, lnb, mixw, bias, cos, sa, sb, seg)


def _key_to_float(u):
    bits = jnp.where(u < 0, u ^ I32(INT_MIN), ~u)
    return lax.bitcast_convert_type(bits, F32)


def _count(sc_ref, n_tiles, pred):
    rows, width = sc_ref.shape[1], sc_ref.shape[2]

    def body(j, c):
        return c + jnp.where(pred(sc_ref[j], j), 1.0, 0.0)

    c = lax.fori_loop(0, n_tiles, body, jnp.zeros((rows, width), F32))
    return jnp.sum(c, axis=1, keepdims=True)


def _select_threshold(sc_ref, n_tiles, k_row, n_index_bits):
    rows, width = sc_ref.shape[1], sc_ref.shape[2]
    k_f = k_row.astype(F32)

    def bit_step(i, carry):
        res, cnt_res = carry
        cand = res | lax.shift_left(I32(1), I32(31) - i)
        t = _key_to_float(cand)
        cnt = _count(sc_ref, n_tiles, lambda x, j: x >= t)
        ok = cnt >= k_f
        return jnp.where(ok, cand, res), jnp.where(ok, cnt, cnt_res)

    res, cnt_res = lax.fori_loop(0, 32, bit_step,
                                 (jnp.zeros((rows, 1), I32), jnp.zeros((rows, 1), F32)))
    thr = _key_to_float(res)
    surplus = jnp.max(cnt_res - k_f)

    @pl.when(surplus > 0.0)
    def _():
        n_take = k_f - _count(sc_ref, n_tiles, lambda x, j: x > thr)
        lane = lax.broadcasted_iota(I32, (rows, width), 1)

        def idx_step(i, m):
            cand = m | lax.shift_left(I32(1), I32(n_index_bits - 1) - i)
            cnt = _count(sc_ref, n_tiles, lambda x, j: (x == thr) & (j * width + lane < cand))
            return jnp.where(cnt <= n_take, cand, m)

        cut = lax.fori_loop(0, n_index_bits, idx_step, jnp.zeros((rows, 1), I32))

        def drop(j, _):
            x = sc_ref[j]
            sc_ref[j] = jnp.where((x == thr) & (j * width + lane >= cut), NEG_INF, x)
            return 0

        lax.fori_loop(0, n_tiles, drop, 0)

    return thr


def _attend_init(m_ref, l_ref, acc_ref):
    m_ref[...] = jnp.full(m_ref.shape, NEG_INF, F32)
    l_ref[...] = jnp.zeros(l_ref.shape, F32)
    acc_ref[...] = jnp.zeros(acc_ref.shape, F32)


def _softmax_step(s, m_prev, l_prev):
    m_new = jnp.maximum(m_prev, jnp.max(s, axis=1, keepdims=True))
    m_safe = jnp.where(m_new == NEG_INF, 0.0, m_new)
    alpha = jnp.exp(m_prev - m_safe)
    p = jnp.exp(s - _tile_lanes(m_safe, s.shape[1]))
    l_new = alpha * l_prev + jnp.sum(p, axis=1, keepdims=True)
    return p, m_new, l_new, alpha


_NT = (((1,), (1,)), ((), ()))


def _prompt_attend_body(k_top, n_index_bits, qbd_ref, qih_ref, kiwi_ref, kb_ref, vb_ref, kib_ref,
                        o_ref, sc_ref, wb_ref, m_ref, l_ref, acc_ref):
    qb = pl.program_id(1)
    n_tiles = (qb * Q_BLOCK + Q_BLOCK + KEY_TILE - 1) // KEY_TILE
    qpos = qb * Q_BLOCK + lax.broadcasted_iota(I32, (Q_BLOCK, 1), 0)

    kiwi = kiwi_ref[...]
    for h in range(N_IDX_HEADS):
        wb_ref[h] = jnp.broadcast_to(kiwi[:, D_IDX + h:D_IDX + h + 1], (Q_BLOCK, KEY_TILE))

    kcol = lax.broadcasted_iota(I32, (Q_BLOCK, KEY_TILE), 1)

    def score_tile(j, _):
        start = pl.multiple_of(j * KEY_TILE, KEY_TILE)
        ki_t = kib_ref[pl.ds(start, KEY_TILE), :]
        acc = jnp.zeros((Q_BLOCK, KEY_TILE), F32)
        for h in range(N_IDX_HEADS):
            d = lax.dot_general(qih_ref[h], ki_t, _NT, preferred_element_type=F32)
            acc = acc + wb_ref[h] * jnp.maximum(d, 0.0)
        sc_ref[j] = jnp.where(start + kcol <= qpos, acc, NEG_INF)
        return 0

    lax.fori_loop(0, n_tiles, score_tile, 0)

    k_row = jnp.minimum(k_top, qpos + 1)
    thr = _select_threshold(sc_ref, n_tiles, k_row, n_index_bits)
    thr2 = jnp.concatenate([thr, thr], axis=0)

    _attend_init(m_ref, l_ref, acc_ref)
    lane = lax.broadcasted_iota(I32, (2 * Q_BLOCK, LANES), 1)
    low = lane < HEAD_DIM

    def attend_tile(j, _):
        start = pl.multiple_of(j * KEY_TILE, KEY_TILE)
        k_t = kb_ref[pl.ds(start, KEY_TILE), :]
        v_t = vb_ref[pl.ds(start, KEY_TILE), :]
        sc = sc_ref[j]
        keep = jnp.concatenate([sc, sc], axis=0) >= thr2
        for pair in range(N_KV_HEADS // 2):
            v_p = v_t[:, pair * LANES:(pair + 1) * LANES]
            pv, al = [], []
            for g in (2 * pair, 2 * pair + 1):
                s = lax.dot_general(qbd_ref[g], k_t, _NT, preferred_element_type=F32)
                s = jnp.where(keep, s, NEG_INF)
                p, m_new, l_new, alpha = _softmax_step(s, m_ref[g], l_ref[g])
                m_ref[g] = m_new
                l_ref[g] = l_new
                pv.append(jnp.dot(p.astype(BF16), v_p, preferred_element_type=F32))
                al.append(alpha)
            acc_ref[pair] = (acc_ref[pair] * jnp.where(low, al[0], al[1])
                             + jnp.where(low, pv[0], pv[1]))
        return 0

    lax.fori_loop(0, n_tiles, attend_tile, 0)

    low_q = lax.broadcasted_iota(I32, (Q_BLOCK, LANES), 1) < HEAD_DIM
    for pair in range(N_KV_HEADS // 2):
        a = acc_ref[pair] / jnp.where(low, l_ref[2 * pair], l_ref[2 * pair + 1])
        a0, a1 = a[:Q_BLOCK], a[Q_BLOCK:]
        o_ref[:, (2 * pair) * LANES:(2 * pair + 1) * LANES] = jnp.where(
            low_q, a0, pltpu.roll(a1, HEAD_DIM, 1))
        o_ref[:, (2 * pair + 1) * LANES:(2 * pair + 2) * LANES] = jnp.where(
            low_q, pltpu.roll(a0, HEAD_DIM, 1), a1)


def _prompt_attend(qbd, qih, kiwi, kb, vb, kib, k_top):
    bsz, nqb = qbd.shape[:2]
    seq = kb.shape[1]
    n_index_bits = int(seq).bit_length()
    resident = dict(pipeline_mode=pl.Buffered(1))
    return pl.pallas_call(
        functools.partial(_prompt_attend_body, k_top, n_index_bits),
        grid=(bsz, nqb),
        in_specs=[
            pl.BlockSpec((None, None, N_KV_HEADS, 2 * Q_BLOCK, KV_WIDTH), lambda b, i: (b, i, 0, 0, 0)),
            pl.BlockSpec((None, None, N_IDX_HEADS, Q_BLOCK, LANES), lambda b, i: (b, i, 0, 0, 0)),
            pl.BlockSpec((Q_BLOCK, LANES), lambda b, i: (b * nqb + i, 0)),
            pl.BlockSpec((None, seq, KV_WIDTH), lambda b, i: (b, 0, 0), **resident),
            pl.BlockSpec((None, seq, KV_WIDTH), lambda b, i: (b, 0, 0), **resident),
            pl.BlockSpec((None, seq, LANES), lambda b, i: (b, 0, 0), **resident),
        ],
        out_specs=pl.BlockSpec((Q_BLOCK, ATT_WIDTH), lambda b, i: (b * nqb + i, 0)),
        out_shape=jax.ShapeDtypeStruct((bsz * seq, ATT_WIDTH), F32),
        scratch_shapes=[
            pltpu.VMEM((seq // KEY_TILE, Q_BLOCK, KEY_TILE), F32),
            pltpu.VMEM((N_IDX_HEADS, Q_BLOCK, KEY_TILE), F32),
            pltpu.VMEM((N_KV_HEADS, 2 * Q_BLOCK, LANES), F32),
            pltpu.VMEM((N_KV_HEADS, 2 * Q_BLOCK, LANES), F32),
            pltpu.VMEM((N_KV_HEADS // 2, 2 * Q_BLOCK, LANES), F32),
        ],
        compiler_params=pltpu.CompilerParams(dimension_semantics=("arbitrary", "arbitrary"),
                                             vmem_limit_bytes=VMEM_LIMIT),
        name="prompt_attend",
    )(qbd, qih, kiwi, kb, vb, kib)


def _sample_attend_body(k_top, n_index_bits, n_pages, t_new,
                        pt_ref, qbd_ref, qih_ref, w_ref, knew_ref, vnew_ref, kinew_ref,
                        cik_hbm, ck_hbm, cv_hbm, o_ref,
                        ibuf, kbuf, vbuf, sem, sc_ref, m_ref, l_ref, acc_ref):
    b = pl.program_id(0)
    nb = pl.num_programs(0)
    slot = b % 2
    n_past_tiles = n_pages * PAGE_SIZE // KEY_TILE
    n_tiles = n_past_tiles + 1
    rows = N_HEADS * t_new

    def page_copies(bb, s, p):
        page = pt_ref[bb, p]
        dst = pl.ds(p * PAGE_SIZE, PAGE_SIZE)
        return (pltpu.make_async_copy(cik_hbm.at[page], ibuf.at[s, dst], sem.at[s, 0]),
                pltpu.make_async_copy(ck_hbm.at[page], kbuf.at[s, dst], sem.at[s, 1]),
                pltpu.make_async_copy(cv_hbm.at[page], vbuf.at[s, dst], sem.at[s, 2]))

    def start_fetch(bb, s):
        def body(p, _):
            for c in page_copies(bb, s, p):
                c.start()
            return 0
        lax.fori_loop(0, n_pages, body, 0)

    def wait_fetch(bb, s):
        def body(p, _):
            for c in page_copies(bb, s, p):
                c.wait()
            return 0
        lax.fori_loop(0, n_pages, body, 0)

    @pl.when(b == 0)
    def _():
        start_fetch(b, slot)

    @pl.when(b + 1 < nb)
    def _():
        start_fetch(b + 1, 1 - slot)

    wait_fetch(b, slot)

    qih = qih_ref[...]
    w = _tile_lanes(w_ref[...], KEY_TILE)
    tq = lax.broadcasted_iota(I32, (t_new, KEY_TILE), 0)
    col = lax.broadcasted_iota(I32, (t_new, KEY_TILE), 1)

    def head_sum(ki_t):
        d = lax.dot_general(qih, ki_t, _NT, preferred_element_type=F32)
        d = w * jnp.maximum(d, 0.0)
        acc = d[0:t_new]
        for h in range(1, N_IDX_HEADS):
            acc = acc + d[h * t_new:(h + 1) * t_new]
        return acc

    def score_tile(j, _):
        start = pl.multiple_of(j * KEY_TILE, KEY_TILE)
        sc_ref[j] = head_sum(ibuf[slot, pl.ds(start, KEY_TILE), :].astype(BF16))
        return 0

    lax.fori_loop(0, n_past_tiles, score_tile, 0)

    def pad_rows(a):
        return jnp.concatenate([a, jnp.zeros((KEY_TILE - a.shape[0], a.shape[1]), a.dtype)], axis=0)

    ki_new = pad_rows(kinew_ref[...][:, :D_IDX]).astype(BF16)
    sc_ref[n_past_tiles] = jnp.where((col < t_new) & (col <= tq), head_sum(ki_new), NEG_INF)

    qpos = n_pages * PAGE_SIZE + lax.broadcasted_iota(I32, (t_new, 1), 0)
    k_row = jnp.minimum(k_top, qpos + 1)
    thr = _select_threshold(sc_ref, n_tiles, k_row, n_index_bits)

    _attend_init(m_ref, l_ref, acc_ref)
    qbd = qbd_ref[...]

    def attend(j, k_t, v_t):
        bias = jnp.where(sc_ref[j] >= thr, 0.0, NEG_INF)
        s = lax.dot_general(qbd, k_t, _NT, preferred_element_type=F32)
        s = s + jnp.concatenate([bias] * N_HEADS, axis=0)
        p, m_new, l_new, alpha = _softmax_step(s, m_ref[...], l_ref[...])
        m_ref[...] = m_new
        l_ref[...] = l_new
        acc_ref[...] = (acc_ref[...] * _tile_lanes(alpha, KV_WIDTH)
                        + jnp.dot(p.astype(BF16), v_t, preferred_element_type=F32))

    def attend_tile(j, _):
        start = pl.multiple_of(j * KEY_TILE, KEY_TILE)
        attend(j, kbuf[slot, pl.ds(start, KEY_TILE), :].astype(BF16),
               vbuf[slot, pl.ds(start, KEY_TILE), :].astype(BF16))
        return 0

    lax.fori_loop(0, n_past_tiles, attend_tile, 0)
    attend(n_past_tiles, pad_rows(knew_ref[...]).astype(BF16), pad_rows(vnew_ref[...]).astype(BF16))

    o_ref[...] = acc_ref[...] / _tile_lanes(l_ref[...], KV_WIDTH)


def _sample_attend(page_table, qbd, qih, w_rows, knew, vnew, kinew, cik, ck, cv, k_top):
    dbsz, n_pages = page_table.shape
    t_new = knew.shape[1]
    rows = N_HEADS * t_new
    past = n_pages * PAGE_SIZE
    assert past % KEY_TILE == 0
    n_index_bits = int(past + KEY_TILE).bit_length()
    n_tiles = past // KEY_TILE + 1

    def bmap(b, pt):
        return (b, 0, 0)

    grid_spec = pltpu.PrefetchScalarGridSpec(
        num_scalar_prefetch=1,
        grid=(dbsz,),
        in_specs=[
            pl.BlockSpec((None, rows, KV_WIDTH), bmap),
            pl.BlockSpec((None, rows, D_IDX), bmap),
            pl.BlockSpec((None, rows, LANES), bmap),
            pl.BlockSpec((None, t_new, KV_WIDTH), bmap),
            pl.BlockSpec((None, t_new, KV_WIDTH), bmap),
            pl.BlockSpec((None, t_new, LANES), bmap),
            pl.BlockSpec(memory_space=pl.ANY),
            pl.BlockSpec(memory_space=pl.ANY),
            pl.BlockSpec(memory_space=pl.ANY),
        ],
        out_specs=pl.BlockSpec((None, rows, KV_WIDTH), bmap),
        scratch_shapes=[
            pltpu.VMEM((2, past, D_IDX), F32),
            pltpu.VMEM((2, past, KV_WIDTH), F32),
            pltpu.VMEM((2, past, KV_WIDTH), F32),
            pltpu.SemaphoreType.DMA((2, 3)),
            pltpu.VMEM((n_tiles, t_new, KEY_TILE), F32),
            pltpu.VMEM((rows, LANES), F32),
            pltpu.VMEM((rows, LANES), F32),
            pltpu.VMEM((rows, KV_WIDTH), F32),
        ],
    )
    return pl.pallas_call(
        functools.partial(_sample_attend_body, k_top, n_index_bits, n_pages, t_new),
        grid_spec=grid_spec,
        out_shape=jax.ShapeDtypeStruct((dbsz, rows, KV_WIDTH), F32),
        compiler_params=pltpu.CompilerParams(dimension_semantics=("arbitrary",),
                                             vmem_limit_bytes=VMEM_LIMIT),
        name="sample_attend",
    )(page_table, qbd, qih, w_rows, knew, vnew, kinew, cik, ck, cv)


def _finish_body(x_ref, o_ref, ga_ref, gm_ref, p_ref, wo_ref, pg_ref, wg_ref, wp_ref, y_ref):
    att = (o_ref[...] * ga_ref[...]).astype(BF16)
    r = (x_ref[...]
         + jnp.dot(att, wo_ref[:ATT_WIDTH, :], preferred_element_type=F32)
         + jnp.dot(gm_ref[...].astype(BF16), wo_ref[ATT_WIDTH:, :], preferred_element_type=F32))
    rn = r * lax.rsqrt(jnp.mean(r * r, axis=-1, keepdims=True) + EPS) * pg_ref[...]
    gate = jax.nn.sigmoid(jnp.dot(rn.astype(BF16), wg_ref[...], preferred_element_type=F32))
    y_ref[...] = r + gate * jnp.dot(p_ref[...].astype(BF16), wp_ref[...], preferred_element_type=F32)


def _finish(x2, o, ga, gm, p2, wo, pg, wg, wp):
    rows, d_model = x2.shape
    tm = min(ROW_TILE, rows)

    def rspec(a):
        return pl.BlockSpec((tm, a.shape[1]), lambda i: (i, 0))

    def cspec(a):
        return pl.BlockSpec(a.shape, lambda i: (0, 0))

    return pl.pallas_call(
        _finish_body,
        grid=(rows // tm,),
        in_specs=[rspec(x2), rspec(o), rspec(ga), rspec(gm), rspec(p2),
                  cspec(wo), cspec(pg), cspec(wg), cspec(wp)],
        out_specs=pl.BlockSpec((tm, d_model), lambda i: (i, 0)),
        out_shape=jax.ShapeDtypeStruct((rows, d_model), F32),
        compiler_params=pltpu.CompilerParams(dimension_semantics=("arbitrary",),
                                             vmem_limit_bytes=VMEM_LIMIT),
        name="finish",
    )(x2, o, ga, gm, p2, wo, pg, wg, wp)


def _rope_tables(pos):
    inv = jnp.power(ROPE_THETA, -jnp.arange(ROPE_HALF, dtype=F32) * 2.0 / ROPE_ROT)
    ang = pos.astype(F32)[:, None] * inv[None, :]
    cos, sin = jnp.cos(ang), jnp.sin(ang)
    n = pos.shape[0]
    one = jnp.ones((n, HEAD_DIM - ROPE_ROT), F32)
    zero8 = jnp.zeros((n, ROPE_HALF), F32)
    zero = jnp.zeros((n, HEAD_DIM - ROPE_ROT), F32)
    cos_h = jnp.concatenate([cos, cos, one], axis=1)
    sa_h = jnp.concatenate([-sin, zero8, zero], axis=1)
    sb_h = jnp.concatenate([zero8, sin, zero], axis=1)
    return tuple(jnp.concatenate([t, t], axis=1) for t in (cos_h, sa_h, sb_h))


def _pair_mix(ws):
    g = ws.shape[0]
    return ws.reshape(g // 2, 2, CHUNK, CHUNK).transpose(0, 2, 1, 3).reshape(g // 2, CHUNK, 2 * CHUNK).astype(BF16)


def _group_diag(q, lead):
    t = q.shape[-2]
    n = len(lead)
    q6 = q.reshape(*lead, t, N_KV_HEADS, 2, HEAD_DIM)
    q6 = jnp.moveaxis(q6, n, n + 2)
    q6 = q6.reshape(*lead, N_KV_HEADS, 2 * t, HEAD_DIM)
    eye = jnp.eye(N_KV_HEADS, dtype=q.dtype)
    qd = q6[..., None, :] * eye[:, None, :, None]
    return qd.reshape(*lead, N_KV_HEADS, 2 * t, KV_WIDTH)


def kernel(x_prompt, x_sample, cache_k, cache_v, cache_idx_k, page_table, p_prompt, p_sample,
           norm_in_g, w_in, q_norm_g, k_norm_g, ln_v_g, ln_v_b, w_s, b_s, w_out,
           ple_norm_g, w_ple_gate, w_ple_proj):
    depth = w_in.shape[0]
    assert depth == 1
    bsz, seq, d_model = x_prompt.shape
    dbsz, t_new, _ = x_sample.shape
    n_pages = page_table.shape[1]
    past = n_pages * PAGE_SIZE
    assert seq % KEY_TILE == 0 and CHUNK % t_new == 0 and (dbsz * t_new) % CHUNK == 0

    perm = np.concatenate([np.arange(0, _ORIG_KI), np.arange(_ORIG_GA, _ORIG_END),
                           np.arange(_ORIG_KI, _ORIG_GA)])
    w = jnp.pad(w_in[0][:, perm], ((0, 0), (0, C_END - _ORIG_END))).astype(BF16)
    seg = jnp.asarray(np.kron(np.eye(N_HEADS), np.full((HEAD_DIM, HEAD_DIM), 1.0 / HEAD_DIM)), BF16)
    weights = (norm_in_g[0][None, :], w, jnp.tile(q_norm_g[0], N_HEADS)[None, :],
               jnp.tile(k_norm_g[0], N_KV_HEADS)[None, :], ln_v_g[0][None, :], ln_v_b[0][None, :], seg)
    ws_tril = jnp.where(jnp.tril(jnp.ones((CHUNK, CHUNK), bool))[None], w_s[0], 0.0)
    bias_p = jnp.repeat(b_s[0].T, GM_WIDTH // GM_GROUPS, axis=1)
    reps = CHUNK // t_new
    ws_s = jnp.einsum("ab,gij->gaibj", jnp.eye(reps, dtype=F32),
                      ws_tril[:, :t_new, :t_new]).reshape(GM_GROUPS, CHUNK, CHUNK)
    bias_s = jnp.tile(bias_p[:t_new], (reps, 1))
    wo = w_out[0].astype(BF16)
    wg = w_ple_gate[0].astype(BF16)
    wp = w_ple_proj[0].astype(BF16)
    pg = ple_norm_g[0][None, :]

    xp = x_prompt.reshape(bsz * seq, d_model)
    (q, k, v, kb, vb, qi, kiwi, kib, ga, gm) = _project(
        xp, _rope_tables(jnp.arange(seq)), _pair_mix(ws_tril), bias_p, weights, emit_vn=False)
    nqb = seq // Q_BLOCK
    k_top_p = min(TOPK_MAX, seq // 4)
    qbd = _group_diag(q.reshape(bsz, nqb, Q_BLOCK, ATT_WIDTH), (bsz, nqb))
    qih = qi.reshape(bsz, nqb, Q_BLOCK, N_IDX_HEADS, D_IDX).transpose(0, 1, 3, 2, 4)
    qih = jnp.pad(qih, ((0, 0),) * 4 + ((0, LANES - D_IDX),))
    o = _prompt_attend(qbd, qih, kiwi, kb.reshape(bsz, seq, KV_WIDTH), vb.reshape(bsz, seq, KV_WIDTH),
                       kib.reshape(bsz, seq, LANES), k_top_p)
    y_prompt = _finish(xp, o, ga, gm, p_prompt[0].reshape(bsz * seq, -1), wo, pg, wg, wp)

    xs = x_sample.reshape(dbsz * t_new, d_model)
    pos_s = past + jnp.arange(min(ROW_TILE, dbsz * t_new)) % t_new
    (q, k_s, v_s, kb_s, vb_s, qi, kiwi_s, _, ga, gm, vn_s) = _project(
        xs, _rope_tables(pos_s), _pair_mix(ws_s), bias_s, weights, emit_vn=True)
    k_top_s = min(TOPK_MAX, (past + t_new) // 4)
    qbd = _group_diag(q.reshape(dbsz, t_new, ATT_WIDTH), (dbsz,)).reshape(dbsz, N_HEADS * t_new, KV_WIDTH)
    qih = qi.reshape(dbsz, t_new, N_IDX_HEADS, D_IDX).transpose(0, 2, 1, 3).reshape(
        dbsz, N_IDX_HEADS * t_new, D_IDX)
    kiwi3 = kiwi_s.reshape(dbsz, t_new, LANES)
    w_rows = kiwi3[:, :, D_IDX:D_IDX + N_IDX_HEADS].transpose(0, 2, 1).reshape(dbsz, N_IDX_HEADS * t_new, 1)
    w_rows = jnp.broadcast_to(w_rows, (dbsz, N_IDX_HEADS * t_new, LANES))
    acc = _sample_attend(page_table, qbd, qih, w_rows,
                         k_s.reshape(dbsz, t_new, KV_WIDTH), v_s.reshape(dbsz, t_new, KV_WIDTH), kiwi3,
                         cache_idx_k[0], cache_k[0].reshape(-1, PAGE_SIZE, KV_WIDTH),
                         cache_v[0].reshape(-1, PAGE_SIZE, KV_WIDTH), k_top_s)
    acc = acc.reshape(dbsz, N_KV_HEADS, 2, t_new, N_KV_HEADS, HEAD_DIM)
    o_s = jnp.stack([acc[:, g, :, :, g, :] for g in range(N_KV_HEADS)], axis=1)
    o_s = o_s.transpose(0, 3, 1, 2, 4).reshape(dbsz * t_new, ATT_WIDTH)
    y_sample = _finish(xs, o_s, ga, gm, p_sample[0].reshape(dbsz * t_new, -1), wo, pg, wg, wp)

    return (y_prompt.reshape(bsz, seq, d_model),
            y_sample.reshape(dbsz, t_new, d_model),
            k.reshape(1, bsz, seq, N_KV_HEADS, HEAD_DIM),
            v.reshape(1, bsz, seq, N_KV_HEADS, HEAD_DIM),
            kiwi[:, :D_IDX].reshape(1, bsz, seq, D_IDX),
            k_s.reshape(1, dbsz, t_new, N_KV_HEADS, HEAD_DIM),
            v_s.reshape(1, dbsz, t_new, N_KV_HEADS, HEAD_DIM),
            kiwi_s[:, :D_IDX].reshape(1, dbsz, t_new, D_IDX),
            vn_s.reshape(1, dbsz, t_new, GM_WIDTH))
```

```python
import functools

import numpy as np
import jax
import jax.numpy as jnp
from jax import lax
from jax.experimental import pallas as pl
from jax.experimental.pallas import tpu as pltpu

F32 = jnp.float32
BF16 = jnp.bfloat16
I32 = jnp.int32

HEAD_DIM = 64
N_HEADS = 8
N_KV_HEADS = 4
N_IDX_HEADS = 8
D_IDX = 64
TOPK_MAX = 256
GM_GROUPS = 8
CHUNK = 128
PAGE_SIZE = 128
ROPE_THETA = 500000.0
ROPE_ROT = HEAD_DIM // 4
ROPE_HALF = ROPE_ROT // 2
EPS = 1e-6

ATT_WIDTH = N_HEADS * HEAD_DIM
KV_WIDTH = N_KV_HEADS * HEAD_DIM
IDX_WIDTH = N_IDX_HEADS * D_IDX
GM_WIDTH = GM_GROUPS * 64
LANES = 128
SUBLANES = 8
BF16_SUBLANES = 16
KEY_TILE = 256
Q_BLOCK = 128
ROW_TILE = 256
VMEM_LIMIT = 56 * 1024 * 1024

C_Q, C_K, C_V, C_QI, C_GA, C_U, C_VM, C_GM, C_KIWI, C_END = (
    0, 512, 768, 1024, 1536, 2048, 2560, 3072, 3584, 3712)
_ORIG_KI, _ORIG_GA, _ORIG_END = 1536, 1608, 3656
LOG2_E = 1.4426950408889634
NEG_INF = float("-inf")
INT_MIN = -2 ** 31


def _head_rms(xh, seg, g):
    sq = xh * xh
    hi = sq.astype(BF16)
    lo = (sq - hi.astype(F32)).astype(BF16)
    ms = (jnp.dot(hi, seg, preferred_element_type=F32)
          + jnp.dot(lo, seg, preferred_element_type=F32))
    return xh * lax.rsqrt(ms + EPS) * g


def _rope(xh, cos, sa, sb):
    w = xh.shape[-1]
    return xh * cos + pltpu.roll(xh, w - ROPE_HALF, 1) * sa + pltpu.roll(xh, ROPE_HALF, 1) * sb


def _tile_lanes(t, width):
    reps = width // t.shape[-1]
    return t if reps == 1 else jnp.concatenate([t] * reps, axis=1)


def _project_body(emit_vn, emit_vt, x_ref, ng_ref, w_ref, qg_ref, kg_ref, lng_ref, lnb_ref, mixw_ref,
                  bias_ref, cos_ref, sa_ref, sb_ref, seg_ref, *outs):
    q_o, k_o, v_o, kb_o, qi_o, kiwi_o, kib_o, ga_o, gm_o = outs[:9]
    extra = list(outs[9:])
    vn_o = extra.pop(0) if emit_vn else None
    vt_o = extra.pop(0) if emit_vt else None
    x = x_ref[...]
    h = x * lax.rsqrt(jnp.mean(x * x, axis=-1, keepdims=True) + EPS) * ng_ref[...]
    hb = h.astype(BF16)

    def proj(c0, c1):
        return jnp.dot(hb, w_ref[:, c0:c1], preferred_element_type=F32)

    cos, sa, sb = cos_ref[...], sa_ref[...], sb_ref[...]
    seg = seg_ref[...]

    q = _head_rms(proj(C_Q, C_K), seg, qg_ref[...])
    q = _rope(q, _tile_lanes(cos, ATT_WIDTH), _tile_lanes(sa, ATT_WIDTH), _tile_lanes(sb, ATT_WIDTH))
    q_o[...] = (q * (HEAD_DIM ** -0.5 * LOG2_E)).astype(BF16)

    k = _head_rms(proj(C_K, C_V), seg[:KV_WIDTH, :KV_WIDTH], kg_ref[...])
    k = _rope(k, _tile_lanes(cos, KV_WIDTH), _tile_lanes(sa, KV_WIDTH), _tile_lanes(sb, KV_WIDTH))
    k_o[...] = k
    kb_o[...] = k.astype(BF16)

    v = proj(C_V, C_QI)
    v_o[...] = v
    if emit_vt:
        vt_o[...] = v.T.astype(BF16)

    qi = _rope(proj(C_QI, C_GA), _tile_lanes(cos, IDX_WIDTH), _tile_lanes(sa, IDX_WIDTH),
               _tile_lanes(sb, IDX_WIDTH))
    qi_o[...] = qi.astype(BF16)

    lane = lax.broadcasted_iota(I32, cos.shape, 1)
    is_ki = lane < D_IDX
    wi_scale = jnp.where(lane < D_IDX + N_IDX_HEADS, IDX_WIDTH ** -0.5, 1.0)
    kiwi = _rope(proj(C_KIWI, C_END), jnp.where(is_ki, cos, wi_scale),
                 jnp.where(is_ki, sa, 0.0), jnp.where(is_ki, sb, 0.0))
    kiwi_o[...] = kiwi
    kib_o[...] = kiwi.astype(BF16)

    ga_o[...] = jax.nn.silu(proj(C_GA, C_U))

    vmg = jax.nn.gelu(proj(C_VM, C_GM))
    xc = vmg - jnp.mean(vmg, axis=-1, keepdims=True)
    vn = xc * lax.rsqrt(jnp.mean(xc * xc, axis=-1, keepdims=True) + EPS) * lng_ref[...] + lnb_ref[...]
    if emit_vn:
        vn_o[...] = vn

    lane_c = lax.broadcasted_iota(I32, (CHUNK, LANES), 1)
    rows = x.shape[0]
    chunks = []
    for c in range(rows // CHUNK):
        pieces = []
        for p in range(GM_GROUPS // 2):
            t = vn[c * CHUNK:(c + 1) * CHUNK, p * LANES:(p + 1) * LANES]
            rhs = jnp.concatenate([jnp.where(lane_c < 64, t, 0.0), jnp.where(lane_c >= 64, t, 0.0)],
                                  axis=0).astype(BF16)
            pieces.append(jnp.dot(mixw_ref[p], rhs, preferred_element_type=F32))
        chunks.append(jnp.concatenate(pieces, axis=1) + bias_ref[...])
    s = jnp.concatenate(chunks, axis=0)
    gm_o[...] = jax.nn.gelu(proj(C_U, C_VM)) * s * jax.nn.silu(proj(C_GM, C_KIWI))


def _project(x2, pos_tables, mixw, bias, weights, emit_vn, emit_vt):
    rows = x2.shape[0]
    tm = min(ROW_TILE, rows)
    assert rows % tm == 0 and tm % CHUNK == 0
    ng, w, qg, kg, lng, lnb, seg = weights
    cos, sa, sb = pos_tables
    assert cos.shape[0] % tm == 0
    n_pos_blocks = cos.shape[0] // tm

    def row_map(i):
        return (i, 0)

    def pos_map(i):
        return (i % n_pos_blocks, 0)

    def const(i):
        return (0, 0)

    def rspec(width):
        return pl.BlockSpec((tm, width), row_map)

    def cspec(a):
        if a.ndim == 3:
            return pl.BlockSpec(a.shape, lambda i: (0, 0, 0))
        return pl.BlockSpec(a.shape, const)

    out_shapes = [
        jax.ShapeDtypeStruct((rows, ATT_WIDTH), BF16),
        jax.ShapeDtypeStruct((rows, KV_WIDTH), F32),
        jax.ShapeDtypeStruct((rows, KV_WIDTH), F32),
        jax.ShapeDtypeStruct((rows, KV_WIDTH), BF16),
        jax.ShapeDtypeStruct((rows, IDX_WIDTH), BF16),
        jax.ShapeDtypeStruct((rows, LANES), F32),
        jax.ShapeDtypeStruct((rows, LANES), BF16),
        jax.ShapeDtypeStruct((rows, ATT_WIDTH), F32),
        jax.ShapeDtypeStruct((rows, GM_WIDTH), F32),
    ]
    if emit_vn:
        out_shapes.append(jax.ShapeDtypeStruct((rows, GM_WIDTH), F32))
    out_specs = [rspec(s.shape[1]) for s in out_shapes]
    if emit_vt:
        assert tm == KEY_TILE
        out_shapes.append(jax.ShapeDtypeStruct((rows // tm, KV_WIDTH, tm), BF16))
        out_specs.append(pl.BlockSpec((None, KV_WIDTH, tm), lambda i: (i, 0, 0)))
    in_specs = [rspec(x2.shape[1]), cspec(ng), cspec(w), cspec(qg), cspec(kg), cspec(lng), cspec(lnb),
                cspec(mixw), cspec(bias),
                pl.BlockSpec((tm, LANES), pos_map), pl.BlockSpec((tm, LANES), pos_map),
                pl.BlockSpec((tm, LANES), pos_map), cspec(seg)]
    return pl.pallas_call(
        functools.partial(_project_body, emit_vn, emit_vt),
        grid=(rows // tm,),
        in_specs=in_specs,
        out_specs=out_specs,
        out_shape=out_shapes,
        compiler_params=pltpu.CompilerParams(dimension_semantics=("arbitrary",),
                                             vmem_limit_bytes=VMEM_LIMIT),
        name="project",
    )(x2, ng, w, qg, kg, lng, lnb, mixw, bias, cos, sa, sb, seg)


def _key_to_float(u):
    bits = jnp.where(u < 0, u ^ I32(INT_MIN), ~u)
    return lax.bitcast_convert_type(bits, F32)


def _count(sc_ref, n_tiles, key_axis, pred, unroll=1):
    a, b = sc_ref.shape[1], sc_ref.shape[2]

    def fold(ind):
        if key_axis == 1:
            return ind
        parts = [ind[i * SUBLANES:(i + 1) * SUBLANES] for i in range(a // SUBLANES)]
        while len(parts) > 1:
            parts = [parts[i] + parts[i + 1] for i in range(0, len(parts), 2)]
        return parts[0]

    def body(i, c):
        for u in range(unroll):
            j = i * unroll + u
            c = c + fold(jnp.where(pred(sc_ref[j], j), 1.0, 0.0))
        return c

    init = jnp.zeros((a, b) if key_axis == 1 else (SUBLANES, b), F32)
    c = lax.fori_loop(0, n_tiles // unroll, body, init)
    return jnp.sum(c, axis=key_axis, keepdims=True)


def _select_threshold(sc_ref, n_tiles, k_row, n_index_bits, key_axis, unroll=1):
    a, b = sc_ref.shape[1], sc_ref.shape[2]
    tile_keys = sc_ref.shape[1 + key_axis]
    k_f = k_row.astype(F32)

    def bit_step(i, carry):
        res, cnt_res = carry
        cand = res | lax.shift_left(I32(1), I32(31) - i)
        t = _key_to_float(cand)
        cnt = _count(sc_ref, n_tiles, key_axis, lambda x, j: x >= t, unroll)
        ok = cnt >= k_f
        return jnp.where(ok, cand, res), jnp.where(ok, cnt, cnt_res)

    res, cnt_res = lax.fori_loop(0, 32, bit_step,
                                 (jnp.zeros(k_row.shape, I32), jnp.zeros(k_row.shape, F32)))
    thr = _key_to_float(res)
    surplus = jnp.max(cnt_res - k_f)

    @pl.when(surplus > 0.0)
    def _():
        n_take = k_f - _count(sc_ref, n_tiles, key_axis, lambda x, j: x > thr)
        kidx = lax.broadcasted_iota(I32, (a, b), key_axis)

        def idx_step(i, m):
            cand = m | lax.shift_left(I32(1), I32(n_index_bits - 1) - i)
            cnt = _count(sc_ref, n_tiles, key_axis,
                         lambda x, j: (x == thr) & (j * tile_keys + kidx < cand))
            return jnp.where(cnt <= n_take, cand, m)

        cut = lax.fori_loop(0, n_index_bits, idx_step, jnp.zeros(k_row.shape, I32))

        def drop(j, _):
            x = sc_ref[j]
            sc_ref[j] = jnp.where((x == thr) & (j * tile_keys + kidx >= cut), NEG_INF, x)
            return 0

        lax.fori_loop(0, n_tiles, drop, 0)

    return thr


def _attend_init(m_ref, l_ref, acc_ref):
    m_ref[...] = jnp.full(m_ref.shape, NEG_INF, F32)
    l_ref[...] = jnp.zeros(l_ref.shape, F32)
    acc_ref[...] = jnp.zeros(acc_ref.shape, F32)


def _softmax_step(s, m_prev, l_prev):
    m_new = jnp.maximum(m_prev, jnp.max(s, axis=1, keepdims=True))
    m_safe = jnp.where(m_new == NEG_INF, 0.0, m_new)
    alpha = jnp.exp2(m_prev - m_safe)
    p = jnp.exp2(s - _tile_lanes(m_safe, s.shape[1]))
    l_new = alpha * l_prev + jnp.sum(p, axis=1, keepdims=True)
    return p, m_new, l_new, alpha


_NT = (((1,), (1,)), ((), ()))


def _two_stage(n_tiles, produce, consume, carry, buf_a, buf_b):
    produce(0, buf_a)

    def pair(i, c):
        j0 = 2 * i
        produce(j0 + 1, buf_b)
        c = consume(j0, buf_a, c)
        produce(jnp.minimum(j0 + 2, n_tiles - 1), buf_a)
        c = consume(j0 + 1, buf_b, c)
        return c

    return lax.fori_loop(0, n_tiles // 2, pair, carry)


def _prompt_attend_body(k_top, n_index_bits, qbdt_ref, qiht_ref, kiwi_ref, kb_ref, vt_ref, kib_ref,
                        o_ref, sc_ref, acc_ref, buf_a, buf_b):
    qb = pl.program_id(1)
    n_tiles = 2 * ((qb * Q_BLOCK + Q_BLOCK + 2 * KEY_TILE - 1) // (2 * KEY_TILE))
    qpos = qb * Q_BLOCK + lax.broadcasted_iota(I32, (1, Q_BLOCK), 1)

    w_t = kiwi_ref[...].T[D_IDX:D_IDX + N_IDX_HEADS, :]
    w_pairs = [jnp.concatenate([w_t[2 * p:2 * p + 1], w_t[2 * p + 1:2 * p + 2]], axis=1)
               for p in range(N_IDX_HEADS // 2)]
    krow = lax.broadcasted_iota(I32, (KEY_TILE, Q_BLOCK), 0)

    def score_dots(j, buf):
        start = pl.multiple_of(j * KEY_TILE, KEY_TILE)
        ki_t = kib_ref[pl.ds(start, KEY_TILE), :]
        for p in range(N_IDX_HEADS // 2):
            buf[p] = jnp.dot(ki_t, qiht_ref[p], preferred_element_type=F32)

    def score_sum(j, buf, c):
        acc = jnp.zeros((KEY_TILE, Q_BLOCK), F32)
        for p in range(N_IDX_HEADS // 2):
            d = jnp.maximum(buf[p], 0.0) * w_pairs[p]
            acc = acc + d[:, :Q_BLOCK] + d[:, Q_BLOCK:]
        sc_ref[j] = jnp.where(j * KEY_TILE + krow <= qpos, acc, NEG_INF)
        return c

    _two_stage(n_tiles, score_dots, score_sum, 0, buf_a, buf_b)

    k_row = jnp.minimum(k_top, qpos + 1)
    thr = _select_threshold(sc_ref, n_tiles, k_row, n_index_bits, key_axis=0, unroll=2)
    thr2 = jnp.concatenate([thr, thr], axis=1)

    acc_ref[...] = jnp.zeros(acc_ref.shape, F32)
    m0 = tuple(jnp.full((1, 2 * Q_BLOCK), NEG_INF, F32) for _ in range(N_KV_HEADS))
    l0 = tuple(jnp.zeros((1, 2 * Q_BLOCK), F32) for _ in range(N_KV_HEADS))

    ones_rows = jnp.ones((BF16_SUBLANES, KEY_TILE), BF16)

    def logits(j, buf):
        start = pl.multiple_of(j * KEY_TILE, KEY_TILE)
        k_t = kb_ref[pl.ds(start, KEY_TILE), :]
        for g in range(N_KV_HEADS):
            buf[g] = jnp.dot(k_t, qbdt_ref[g], preferred_element_type=F32)

    def softmax_pv(j, buf, carry):
        ms, ls = carry
        v_t = vt_ref[j]
        sc = sc_ref[j]
        keep = jnp.concatenate([sc, sc], axis=1) >= thr2
        new_m, new_l = [], []
        for g in range(N_KV_HEADS):
            s = jnp.where(keep, buf[g], NEG_INF)
            m_new = jnp.maximum(ms[g], jnp.max(s, axis=0, keepdims=True))
            m_safe = jnp.where(m_new == NEG_INF, 0.0, m_new)
            alpha = jnp.exp2(ms[g] - m_safe)
            p = jnp.exp2(s - m_safe).astype(BF16)
            lhs = jnp.concatenate([v_t[g * HEAD_DIM:(g + 1) * HEAD_DIM, :], ones_rows], axis=0)
            pv = jnp.dot(lhs, p, preferred_element_type=F32)
            new_m.append(m_new)
            new_l.append(alpha * ls[g] + pv[HEAD_DIM:HEAD_DIM + 1])
            acc_ref[g] = acc_ref[g] * alpha + pv[:HEAD_DIM]
        return tuple(new_m), tuple(new_l)

    _, ls = _two_stage(n_tiles, logits, softmax_pv, (m0, l0), buf_a, buf_b)

    blocks = []
    for g in range(N_KV_HEADS):
        og = acc_ref[g] / ls[g]
        blocks += [og[:, :Q_BLOCK], og[:, Q_BLOCK:]]
    o_ref[...] = jnp.concatenate(blocks, axis=0).T


def _prompt_attend(qbdt, qiht, kiwi, kb, vt, kib, k_top):
    bsz, nqb = qbdt.shape[:2]
    seq = kb.shape[1]
    n_index_bits = int(seq).bit_length()
    resident = dict(pipeline_mode=pl.Buffered(1))
    return pl.pallas_call(
        functools.partial(_prompt_attend_body, k_top, n_index_bits),
        grid=(bsz, nqb),
        in_specs=[
            pl.BlockSpec((None, None, N_KV_HEADS, KV_WIDTH, 2 * Q_BLOCK), lambda b, i: (b, i, 0, 0, 0)),
            pl.BlockSpec((None, None, N_IDX_HEADS // 2, LANES, 2 * Q_BLOCK), lambda b, i: (b, i, 0, 0, 0)),
            pl.BlockSpec((Q_BLOCK, LANES), lambda b, i: (b * nqb + i, 0)),
            pl.BlockSpec((None, seq, KV_WIDTH), lambda b, i: (b, 0, 0), **resident),
            pl.BlockSpec((None, seq // KEY_TILE, KV_WIDTH, KEY_TILE), lambda b, i: (b, 0, 0, 0), **resident),
            pl.BlockSpec((None, seq, LANES), lambda b, i: (b, 0, 0), **resident),
        ],
        out_specs=pl.BlockSpec((Q_BLOCK, ATT_WIDTH), lambda b, i: (b * nqb + i, 0)),
        out_shape=jax.ShapeDtypeStruct((bsz * seq, ATT_WIDTH), F32),
        scratch_shapes=[
            pltpu.VMEM((seq // KEY_TILE, KEY_TILE, Q_BLOCK), F32),
            pltpu.VMEM((N_KV_HEADS, HEAD_DIM, 2 * Q_BLOCK), F32),
            pltpu.VMEM((N_KV_HEADS, KEY_TILE, 2 * Q_BLOCK), F32),
            pltpu.VMEM((N_KV_HEADS, KEY_TILE, 2 * Q_BLOCK), F32),
        ],
        compiler_params=pltpu.CompilerParams(dimension_semantics=("arbitrary", "arbitrary"),
                                             vmem_limit_bytes=VMEM_LIMIT),
        name="prompt_attend",
    )(qbdt, qiht, kiwi, kb, vt, kib)


def _sample_attend_body(k_top, n_index_bits, n_pages, t_new,
                        pt_ref, qbd_ref, qih_ref, w_ref, knew_ref, vnew_ref, kinew_ref,
                        cik_hbm, ck_hbm, cv_hbm, o_ref,
                        ibuf, kbuf, vbuf, sem, sc_ref, m_ref, l_ref, acc_ref):
    b = pl.program_id(0)
    nb = pl.num_programs(0)
    slot = b % 2
    n_past_tiles = n_pages * PAGE_SIZE // KEY_TILE
    n_tiles = n_past_tiles + 1
    rows = N_HEADS * t_new

    def page_copies(bb, s, p):
        page = pt_ref[bb, p]
        dst = pl.ds(p * PAGE_SIZE, PAGE_SIZE)
        return (pltpu.make_async_copy(cik_hbm.at[page], ibuf.at[s, dst], sem.at[s, 0]),
                pltpu.make_async_copy(ck_hbm.at[page], kbuf.at[s, dst], sem.at[s, 1]),
                pltpu.make_async_copy(cv_hbm.at[page], vbuf.at[s, dst], sem.at[s, 2]))

    def start_fetch(bb, s):
        def body(p, _):
            for c in page_copies(bb, s, p):
                c.start()
            return 0
        lax.fori_loop(0, n_pages, body, 0)

    def wait_fetch(bb, s):
        def body(p, _):
            for c in page_copies(bb, s, p):
                c.wait()
            return 0
        lax.fori_loop(0, n_pages, body, 0)

    @pl.when(b == 0)
    def _():
        start_fetch(b, slot)

    @pl.when(b + 1 < nb)
    def _():
        start_fetch(b + 1, 1 - slot)

    wait_fetch(b, slot)

    qih = qih_ref[...]
    w = _tile_lanes(w_ref[...], KEY_TILE)
    tq = lax.broadcasted_iota(I32, (t_new, KEY_TILE), 0)
    col = lax.broadcasted_iota(I32, (t_new, KEY_TILE), 1)

    def head_sum(ki_t):
        d = lax.dot_general(qih, ki_t, _NT, preferred_element_type=F32)
        d = w * jnp.maximum(d, 0.0)
        acc = d[0:t_new]
        for h in range(1, N_IDX_HEADS):
            acc = acc + d[h * t_new:(h + 1) * t_new]
        return acc

    def score_tile(j, _):
        start = pl.multiple_of(j * KEY_TILE, KEY_TILE)
        sc_ref[j] = head_sum(ibuf[slot, pl.ds(start, KEY_TILE), :].astype(BF16))
        return 0

    lax.fori_loop(0, n_past_tiles, score_tile, 0)

    def pad_rows(a):
        return jnp.concatenate([a, jnp.zeros((KEY_TILE - a.shape[0], a.shape[1]), a.dtype)], axis=0)

    ki_new = pad_rows(kinew_ref[...][:, :D_IDX]).astype(BF16)
    sc_ref[n_past_tiles] = jnp.where((col < t_new) & (col <= tq), head_sum(ki_new), NEG_INF)

    qpos = n_pages * PAGE_SIZE + lax.broadcasted_iota(I32, (t_new, 1), 0)
    k_row = jnp.minimum(k_top, qpos + 1)
    thr = _select_threshold(sc_ref, n_tiles, k_row, n_index_bits, key_axis=1)

    _attend_init(m_ref, l_ref, acc_ref)
    qbd = qbd_ref[...]

    def attend(j, k_t, v_t):
        bias = jnp.where(sc_ref[j] >= thr, 0.0, NEG_INF)
        s = lax.dot_general(qbd, k_t, _NT, preferred_element_type=F32)
        s = s + jnp.concatenate([bias] * N_HEADS, axis=0)
        p, m_new, l_new, alpha = _softmax_step(s, m_ref[...], l_ref[...])
        m_ref[...] = m_new
        l_ref[...] = l_new
        acc_ref[...] = (acc_ref[...] * _tile_lanes(alpha, KV_WIDTH)
                        + jnp.dot(p.astype(BF16), v_t, preferred_element_type=F32))

    def attend_tile(j, _):
        start = pl.multiple_of(j * KEY_TILE, KEY_TILE)
        attend(j, kbuf[slot, pl.ds(start, KEY_TILE), :].astype(BF16),
               vbuf[slot, pl.ds(start, KEY_TILE), :].astype(BF16))
        return 0

    lax.fori_loop(0, n_past_tiles, attend_tile, 0)
    attend(n_past_tiles, pad_rows(knew_ref[...]).astype(BF16), pad_rows(vnew_ref[...]).astype(BF16))

    o_ref[...] = acc_ref[...] / _tile_lanes(l_ref[...], KV_WIDTH)


def _sample_attend(page_table, qbd, qih, w_rows, knew, vnew, kinew, cik, ck, cv, k_top):
    dbsz, n_pages = page_table.shape
    t_new = knew.shape[1]
    rows = N_HEADS * t_new
    past = n_pages * PAGE_SIZE
    assert past % KEY_TILE == 0
    n_index_bits = int(past + KEY_TILE).bit_length()
    n_tiles = past // KEY_TILE + 1

    def bmap(b, pt):
        return (b, 0, 0)

    grid_spec = pltpu.PrefetchScalarGridSpec(
        num_scalar_prefetch=1,
        grid=(dbsz,),
        in_specs=[
            pl.BlockSpec((None, rows, KV_WIDTH), bmap),
            pl.BlockSpec((None, rows, D_IDX), bmap),
            pl.BlockSpec((None, rows, LANES), bmap),
            pl.BlockSpec((None, t_new, KV_WIDTH), bmap),
            pl.BlockSpec((None, t_new, KV_WIDTH), bmap),
            pl.BlockSpec((None, t_new, LANES), bmap),
            pl.BlockSpec(memory_space=pl.ANY),
            pl.BlockSpec(memory_space=pl.ANY),
            pl.BlockSpec(memory_space=pl.ANY),
        ],
        out_specs=pl.BlockSpec((None, rows, KV_WIDTH), bmap),
        scratch_shapes=[
            pltpu.VMEM((2, past, D_IDX), F32),
            pltpu.VMEM((2, past, KV_WIDTH), F32),
            pltpu.VMEM((2, past, KV_WIDTH), F32),
            pltpu.SemaphoreType.DMA((2, 3)),
            pltpu.VMEM((n_tiles, t_new, KEY_TILE), F32),
            pltpu.VMEM((rows, LANES), F32),
            pltpu.VMEM((rows, LANES), F32),
            pltpu.VMEM((rows, KV_WIDTH), F32),
        ],
    )
    return pl.pallas_call(
        functools.partial(_sample_attend_body, k_top, n_index_bits, n_pages, t_new),
        grid_spec=grid_spec,
        out_shape=jax.ShapeDtypeStruct((dbsz, rows, KV_WIDTH), F32),
        compiler_params=pltpu.CompilerParams(dimension_semantics=("arbitrary",),
                                             vmem_limit_bytes=VMEM_LIMIT),
        name="sample_attend",
    )(page_table, qbd, qih, w_rows, knew, vnew, kinew, cik, ck, cv)


def _finish_body(x_ref, o_ref, ga_ref, gm_ref, p_ref, wo_ref, pg_ref, wg_ref, wp_ref, y_ref):
    att = (o_ref[...] * ga_ref[...]).astype(BF16)
    r = (x_ref[...]
         + jnp.dot(att, wo_ref[:ATT_WIDTH, :], preferred_element_type=F32)
         + jnp.dot(gm_ref[...].astype(BF16), wo_ref[ATT_WIDTH:, :], preferred_element_type=F32))
    rn = r * lax.rsqrt(jnp.mean(r * r, axis=-1, keepdims=True) + EPS) * pg_ref[...]
    gate = jax.nn.sigmoid(jnp.dot(rn.astype(BF16), wg_ref[...], preferred_element_type=F32))
    y_ref[...] = r + gate * jnp.dot(p_ref[...].astype(BF16), wp_ref[...], preferred_element_type=F32)


def _finish(x2, o, ga, gm, p2, wo, pg, wg, wp):
    rows, d_model = x2.shape
    tm = min(ROW_TILE, rows)

    def rspec(a):
        return pl.BlockSpec((tm, a.shape[1]), lambda i: (i, 0))

    def cspec(a):
        return pl.BlockSpec(a.shape, lambda i: (0, 0))

    return pl.pallas_call(
        _finish_body,
        grid=(rows // tm,),
        in_specs=[rspec(x2), rspec(o), rspec(ga), rspec(gm), rspec(p2),
                  cspec(wo), cspec(pg), cspec(wg), cspec(wp)],
        out_specs=pl.BlockSpec((tm, d_model), lambda i: (i, 0)),
        out_shape=jax.ShapeDtypeStruct((rows, d_model), F32),
        compiler_params=pltpu.CompilerParams(dimension_semantics=("arbitrary",),
                                             vmem_limit_bytes=VMEM_LIMIT),
        name="finish",
    )(x2, o, ga, gm, p2, wo, pg, wg, wp)


def _rope_tables(pos):
    inv = jnp.power(ROPE_THETA, -jnp.arange(ROPE_HALF, dtype=F32) * 2.0 / ROPE_ROT)
    ang = pos.astype(F32)[:, None] * inv[None, :]
    cos, sin = jnp.cos(ang), jnp.sin(ang)
    n = pos.shape[0]
    one = jnp.ones((n, HEAD_DIM - ROPE_ROT), F32)
    zero8 = jnp.zeros((n, ROPE_HALF), F32)
    zero = jnp.zeros((n, HEAD_DIM - ROPE_ROT), F32)
    cos_h = jnp.concatenate([cos, cos, one], axis=1)
    sa_h = jnp.concatenate([-sin, zero8, zero], axis=1)
    sb_h = jnp.concatenate([zero8, sin, zero], axis=1)
    return tuple(jnp.concatenate([t, t], axis=1) for t in (cos_h, sa_h, sb_h))


def _pair_mix(ws):
    g = ws.shape[0]
    return ws.reshape(g // 2, 2, CHUNK, CHUNK).transpose(0, 2, 1, 3).reshape(g // 2, CHUNK, 2 * CHUNK).astype(BF16)


def _group_diag(q, lead):
    t = q.shape[-2]
    n = len(lead)
    q6 = q.reshape(*lead, t, N_KV_HEADS, 2, HEAD_DIM)
    q6 = jnp.moveaxis(q6, n, n + 2)
    q6 = q6.reshape(*lead, N_KV_HEADS, 2 * t, HEAD_DIM)
    eye = jnp.eye(N_KV_HEADS, dtype=q.dtype)
    qd = q6[..., None, :] * eye[:, None, :, None]
    return qd.reshape(*lead, N_KV_HEADS, 2 * t, KV_WIDTH)


def _heads_to_cols(x):
    b, n, nq, width = x.shape
    x6 = x.reshape(b, n, nq, width // (2 * HEAD_DIM), 2, HEAD_DIM)
    return x6.transpose(0, 1, 3, 5, 4, 2).reshape(b, n, width // (2 * HEAD_DIM), HEAD_DIM, 2 * nq)


def kernel(x_prompt, x_sample, cache_k, cache_v, cache_idx_k, page_table, p_prompt, p_sample,
           norm_in_g, w_in, q_norm_g, k_norm_g, ln_v_g, ln_v_b, w_s, b_s, w_out,
           ple_norm_g, w_ple_gate, w_ple_proj):
    depth = w_in.shape[0]
    assert depth == 1
    bsz, seq, d_model = x_prompt.shape
    dbsz, t_new, _ = x_sample.shape
    n_pages = page_table.shape[1]
    past = n_pages * PAGE_SIZE
    assert seq % KEY_TILE == 0 and CHUNK % t_new == 0 and (dbsz * t_new) % CHUNK == 0

    perm = np.concatenate([np.arange(0, _ORIG_KI), np.arange(_ORIG_GA, _ORIG_END),
                           np.arange(_ORIG_KI, _ORIG_GA)])
    w = jnp.pad(w_in[0][:, perm], ((0, 0), (0, C_END - _ORIG_END))).astype(BF16)
    seg = jnp.asarray(np.kron(np.eye(N_HEADS), np.full((HEAD_DIM, HEAD_DIM), 1.0 / HEAD_DIM)), BF16)
    weights = (norm_in_g[0][None, :], w, jnp.tile(q_norm_g[0], N_HEADS)[None, :],
               jnp.tile(k_norm_g[0], N_KV_HEADS)[None, :], ln_v_g[0][None, :], ln_v_b[0][None, :], seg)
    ws_tril = jnp.where(jnp.tril(jnp.ones((CHUNK, CHUNK), bool))[None], w_s[0], 0.0)
    bias_p = jnp.repeat(b_s[0].T, GM_WIDTH // GM_GROUPS, axis=1)
    reps = CHUNK // t_new
    ws_s = jnp.einsum("ab,gij->gaibj", jnp.eye(reps, dtype=F32),
                      ws_tril[:, :t_new, :t_new]).reshape(GM_GROUPS, CHUNK, CHUNK)
    bias_s = jnp.tile(bias_p[:t_new], (reps, 1))
    wo = w_out[0].astype(BF16)
    wg = w_ple_gate[0].astype(BF16)
    wp = w_ple_proj[0].astype(BF16)
    pg = ple_norm_g[0][None, :]

    xp = x_prompt.reshape(bsz * seq, d_model)
    (q, k, v, kb, qi, kiwi, kib, ga, gm, vt) = _project(
        xp, _rope_tables(jnp.arange(seq)), _pair_mix(ws_tril), bias_p, weights,
        emit_vn=False, emit_vt=True)
    nqb = seq // Q_BLOCK
    k_top_p = min(TOPK_MAX, seq // 4)
    eye = jnp.eye(N_KV_HEADS, dtype=BF16)
    qt = _heads_to_cols(q.reshape(bsz, nqb, Q_BLOCK, ATT_WIDTH))
    qbdt = (qt[:, :, :, None] * eye[:, :, None, None]).reshape(bsz, nqb, N_KV_HEADS, KV_WIDTH, 2 * Q_BLOCK)
    qiht = jnp.pad(_heads_to_cols(qi.reshape(bsz, nqb, Q_BLOCK, IDX_WIDTH)),
                   ((0, 0),) * 3 + ((0, LANES - D_IDX), (0, 0)))
    o = _prompt_attend(qbdt, qiht, kiwi, kb.reshape(bsz, seq, KV_WIDTH),
                       vt.reshape(bsz, seq // KEY_TILE, KV_WIDTH, KEY_TILE),
                       kib.reshape(bsz, seq, LANES), k_top_p)
    y_prompt = _finish(xp, o, ga, gm, p_prompt[0].reshape(bsz * seq, -1), wo, pg, wg, wp)

    xs = x_sample.reshape(dbsz * t_new, d_model)
    pos_s = past + jnp.arange(min(ROW_TILE, dbsz * t_new)) % t_new
    (q, k_s, v_s, _, qi, kiwi_s, _, ga, gm, vn_s) = _project(
        xs, _rope_tables(pos_s), _pair_mix(ws_s), bias_s, weights, emit_vn=True, emit_vt=False)
    k_top_s = min(TOPK_MAX, (past + t_new) // 4)
    qbd = _group_diag(q.reshape(dbsz, t_new, ATT_WIDTH), (dbsz,)).reshape(dbsz, N_HEADS * t_new, KV_WIDTH)
    qih = qi.reshape(dbsz, t_new, N_IDX_HEADS, D_IDX).transpose(0, 2, 1, 3).reshape(
        dbsz, N_IDX_HEADS * t_new, D_IDX)
    kiwi3 = kiwi_s.reshape(dbsz, t_new, LANES)
    w_rows = kiwi3[:, :, D_IDX:D_IDX + N_IDX_HEADS].transpose(0, 2, 1).reshape(dbsz, N_IDX_HEADS * t_new, 1)
    w_rows = jnp.broadcast_to(w_rows, (dbsz, N_IDX_HEADS * t_new, LANES))
    acc = _sample_attend(page_table, qbd, qih, w_rows,
                         k_s.reshape(dbsz, t_new, KV_WIDTH), v_s.reshape(dbsz, t_new, KV_WIDTH), kiwi3,
                         cache_idx_k[0], cache_k[0].reshape(-1, PAGE_SIZE, KV_WIDTH),
                         cache_v[0].reshape(-1, PAGE_SIZE, KV_WIDTH), k_top_s)
    acc = acc.reshape(dbsz, N_KV_HEADS, 2, t_new, N_KV_HEADS, HEAD_DIM)
    o_s = jnp.stack([acc[:, g, :, :, g, :] for g in range(N_KV_HEADS)], axis=1)
    o_s = o_s.transpose(0, 3, 1, 2, 4).reshape(dbsz * t_new, ATT_WIDTH)
    y_sample = _finish(xs, o_s, ga, gm, p_sample[0].reshape(dbsz * t_new, -1), wo, pg, wg, wp)

    return (y_prompt.reshape(bsz, seq, d_model),
            y_sample.reshape(dbsz, t_new, d_model),
            k.reshape(1, bsz, seq, N_KV_HEADS, HEAD_DIM),
            v.reshape(1, bsz, seq, N_KV_HEADS, HEAD_DIM),
            kiwi[:, :D_IDX].reshape(1, bsz, seq, D_IDX),
            k_s.reshape(1, dbsz, t_new, N_KV_HEADS, HEAD_DIM),
            v_s.reshape(1, dbsz, t_new, N_KV_HEADS, HEAD_DIM),
            kiwi_s[:, :D_IDX].reshape(1, dbsz, t_new, D_IDX),
            vn_s.reshape(1, dbsz, t_new, GM_WIDTH))
```

```python
import functools

import numpy as np
import jax
import jax.numpy as jnp
from jax import lax
from jax.experimental import pallas as pl
from jax.experimental.pallas import tpu as pltpu

F32 = jnp.float32
BF16 = jnp.bfloat16
I32 = jnp.int32

HEAD_DIM = 64
N_HEADS = 8
N_KV_HEADS = 4
N_IDX_HEADS = 8
D_IDX = 64
TOPK_MAX = 256
GM_GROUPS = 8
CHUNK = 128
PAGE_SIZE = 128
ROPE_THETA = 500000.0
ROPE_ROT = HEAD_DIM // 4
ROPE_HALF = ROPE_ROT // 2
EPS = 1e-6

ATT_WIDTH = N_HEADS * HEAD_DIM
KV_WIDTH = N_KV_HEADS * HEAD_DIM
IDX_WIDTH = N_IDX_HEADS * D_IDX
GM_WIDTH = GM_GROUPS * 64
LANES = 128
SUBLANES = 8
BF16_SUBLANES = 16
KEY_TILE = 256
Q_BLOCK = 128
ROW_TILE = 256
SAMPLE_CHUNK_PAGES = 8
VMEM_LIMIT = 56 * 1024 * 1024

C_Q, C_K, C_V, C_QI, C_GA, C_U, C_VM, C_GM, C_KIWI, C_END = (
    0, 512, 768, 1024, 1536, 2048, 2560, 3072, 3584, 3712)
_ORIG_KI, _ORIG_GA, _ORIG_END = 1536, 1608, 3656
LOG2_E = 1.4426950408889634
NEG_INF = float("-inf")
INT_MIN = -2 ** 31


def _head_rms(xh, seg, g):
    sq = xh * xh
    hi = sq.astype(BF16)
    lo = (sq - hi.astype(F32)).astype(BF16)
    ms = (jnp.dot(hi, seg, preferred_element_type=F32)
          + jnp.dot(lo, seg, preferred_element_type=F32))
    return xh * lax.rsqrt(ms + EPS) * g


def _rope(xh, cos, sa, sb):
    w = xh.shape[-1]
    return xh * cos + pltpu.roll(xh, w - ROPE_HALF, 1) * sa + pltpu.roll(xh, ROPE_HALF, 1) * sb


def _tile_lanes(t, width):
    reps = width // t.shape[-1]
    return t if reps == 1 else jnp.concatenate([t] * reps, axis=1)


def _project_body(emit_vn, emit_vt, x_ref, ng_ref, w_ref, qg_ref, kg_ref, lng_ref, lnb_ref, mixw_ref,
                  bias_ref, cos_ref, sa_ref, sb_ref, seg_ref, *outs):
    q_o, k_o, v_o, kb_o, qi_o, kiwi_o, kib_o, ga_o, gm_o = outs[:9]
    extra = list(outs[9:])
    vn_o = extra.pop(0) if emit_vn else None
    vt_o = extra.pop(0) if emit_vt else None
    x = x_ref[...]
    h = x * lax.rsqrt(jnp.mean(x * x, axis=-1, keepdims=True) + EPS) * ng_ref[...]
    hb = h.astype(BF16)

    def proj(c0, c1):
        return jnp.dot(hb, w_ref[:, c0:c1], preferred_element_type=F32)

    cos, sa, sb = cos_ref[...], sa_ref[...], sb_ref[...]
    seg = seg_ref[...]

    q = _head_rms(proj(C_Q, C_K), seg, qg_ref[...])
    q = _rope(q, _tile_lanes(cos, ATT_WIDTH), _tile_lanes(sa, ATT_WIDTH), _tile_lanes(sb, ATT_WIDTH))
    q_o[...] = (q * (HEAD_DIM ** -0.5 * LOG2_E)).astype(BF16)

    k = _head_rms(proj(C_K, C_V), seg[:KV_WIDTH, :KV_WIDTH], kg_ref[...])
    k = _rope(k, _tile_lanes(cos, KV_WIDTH), _tile_lanes(sa, KV_WIDTH), _tile_lanes(sb, KV_WIDTH))
    k_o[...] = k
    kb_o[...] = k.astype(BF16)

    v = proj(C_V, C_QI)
    v_o[...] = v
    if emit_vt:
        vt_o[...] = v.T.astype(BF16)

    qi = _rope(proj(C_QI, C_GA), _tile_lanes(cos, IDX_WIDTH), _tile_lanes(sa, IDX_WIDTH),
               _tile_lanes(sb, IDX_WIDTH))
    qi_o[...] = qi.astype(BF16)

    lane = lax.broadcasted_iota(I32, cos.shape, 1)
    is_ki = lane < D_IDX
    wi_scale = jnp.where(lane < D_IDX + N_IDX_HEADS, IDX_WIDTH ** -0.5, 1.0)
    kiwi = _rope(proj(C_KIWI, C_END), jnp.where(is_ki, cos, wi_scale),
                 jnp.where(is_ki, sa, 0.0), jnp.where(is_ki, sb, 0.0))
    kiwi_o[...] = kiwi
    kib_o[...] = kiwi.astype(BF16)

    ga_o[...] = jax.nn.silu(proj(C_GA, C_U))

    vmg = jax.nn.gelu(proj(C_VM, C_GM))
    xc = vmg - jnp.mean(vmg, axis=-1, keepdims=True)
    vn = xc * lax.rsqrt(jnp.mean(xc * xc, axis=-1, keepdims=True) + EPS) * lng_ref[...] + lnb_ref[...]
    if emit_vn:
        vn_o[...] = vn

    lane_c = lax.broadcasted_iota(I32, (CHUNK, LANES), 1)
    rows = x.shape[0]
    chunks = []
    for c in range(rows // CHUNK):
        pieces = []
        for p in range(GM_GROUPS // 2):
            t = vn[c * CHUNK:(c + 1) * CHUNK, p * LANES:(p + 1) * LANES]
            rhs = jnp.concatenate([jnp.where(lane_c < 64, t, 0.0), jnp.where(lane_c >= 64, t, 0.0)],
                                  axis=0).astype(BF16)
            pieces.append(jnp.dot(mixw_ref[p], rhs, preferred_element_type=F32))
        chunks.append(jnp.concatenate(pieces, axis=1) + bias_ref[...])
    s = jnp.concatenate(chunks, axis=0)
    gm_o[...] = jax.nn.gelu(proj(C_U, C_VM)) * s * jax.nn.silu(proj(C_GM, C_KIWI))


def _project(x2, pos_tables, mixw, bias, weights, emit_vn, emit_vt):
    rows = x2.shape[0]
    tm = min(ROW_TILE, rows)
    assert rows % tm == 0 and tm % CHUNK == 0
    ng, w, qg, kg, lng, lnb, seg = weights
    cos, sa, sb = pos_tables
    assert cos.shape[0] % tm == 0
    n_pos_blocks = cos.shape[0] // tm

    def row_map(i):
        return (i, 0)

    def pos_map(i):
        return (i % n_pos_blocks, 0)

    def const(i):
        return (0, 0)

    def rspec(width):
        return pl.BlockSpec((tm, width), row_map)

    def cspec(a):
        if a.ndim == 3:
            return pl.BlockSpec(a.shape, lambda i: (0, 0, 0))
        return pl.BlockSpec(a.shape, const)

    out_shapes = [
        jax.ShapeDtypeStruct((rows, ATT_WIDTH), BF16),
        jax.ShapeDtypeStruct((rows, KV_WIDTH), F32),
        jax.ShapeDtypeStruct((rows, KV_WIDTH), F32),
        jax.ShapeDtypeStruct((rows, KV_WIDTH), BF16),
        jax.ShapeDtypeStruct((rows, IDX_WIDTH), BF16),
        jax.ShapeDtypeStruct((rows, LANES), F32),
        jax.ShapeDtypeStruct((rows, LANES), BF16),
        jax.ShapeDtypeStruct((rows, ATT_WIDTH), F32),
        jax.ShapeDtypeStruct((rows, GM_WIDTH), F32),
    ]
    if emit_vn:
        out_shapes.append(jax.ShapeDtypeStruct((rows, GM_WIDTH), F32))
    out_specs = [rspec(s.shape[1]) for s in out_shapes]
    if emit_vt:
        assert tm == KEY_TILE
        out_shapes.append(jax.ShapeDtypeStruct((rows // tm, KV_WIDTH, tm), BF16))
        out_specs.append(pl.BlockSpec((None, KV_WIDTH, tm), lambda i: (i, 0, 0)))
    in_specs = [rspec(x2.shape[1]), cspec(ng), cspec(w), cspec(qg), cspec(kg), cspec(lng), cspec(lnb),
                cspec(mixw), cspec(bias),
                pl.BlockSpec((tm, LANES), pos_map), pl.BlockSpec((tm, LANES), pos_map),
                pl.BlockSpec((tm, LANES), pos_map), cspec(seg)]
    return pl.pallas_call(
        functools.partial(_project_body, emit_vn, emit_vt),
        grid=(rows // tm,),
        in_specs=in_specs,
        out_specs=out_specs,
        out_shape=out_shapes,
        compiler_params=pltpu.CompilerParams(dimension_semantics=("arbitrary",),
                                             vmem_limit_bytes=VMEM_LIMIT),
        name="project",
    )(x2, ng, w, qg, kg, lng, lnb, mixw, bias, cos, sa, sb, seg)


def _key_to_float(u):
    bits = jnp.where(u < 0, u ^ I32(INT_MIN), ~u)
    return lax.bitcast_convert_type(bits, F32)


def _count(sc_ref, n_tiles, key_axis, pred, unroll=1):
    a, b = sc_ref.shape[1], sc_ref.shape[2]

    def fold(ind):
        if key_axis == 1:
            return ind
        parts = [ind[i * SUBLANES:(i + 1) * SUBLANES] for i in range(a // SUBLANES)]
        while len(parts) > 1:
            parts = [parts[i] + parts[i + 1] for i in range(0, len(parts), 2)]
        return parts[0]

    def body(i, c):
        for u in range(unroll):
            j = i * unroll + u
            c = c + fold(jnp.where(pred(sc_ref[j], j), 1.0, 0.0))
        return c

    init = jnp.zeros((a, b) if key_axis == 1 else (SUBLANES, b), F32)
    c = lax.fori_loop(0, n_tiles // unroll, body, init)
    return jnp.sum(c, axis=key_axis, keepdims=True)


def _select_threshold(sc_ref, n_tiles, k_row, n_index_bits, key_axis, unroll=1):
    a, b = sc_ref.shape[1], sc_ref.shape[2]
    tile_keys = sc_ref.shape[1 + key_axis]
    k_f = k_row.astype(F32)

    def bit_step(i, carry):
        res, cnt_res = carry
        cand = res | lax.shift_left(I32(1), I32(31) - i)
        t = _key_to_float(cand)
        cnt = _count(sc_ref, n_tiles, key_axis, lambda x, j: x >= t, unroll)
        ok = cnt >= k_f
        return jnp.where(ok, cand, res), jnp.where(ok, cnt, cnt_res)

    res, cnt_res = lax.fori_loop(0, 32, bit_step,
                                 (jnp.zeros(k_row.shape, I32), jnp.zeros(k_row.shape, F32)))
    thr = _key_to_float(res)
    surplus = jnp.max(cnt_res - k_f)

    @pl.when(surplus > 0.0)
    def _():
        n_take = k_f - _count(sc_ref, n_tiles, key_axis, lambda x, j: x > thr)
        kidx = lax.broadcasted_iota(I32, (a, b), key_axis)

        def idx_step(i, m):
            cand = m | lax.shift_left(I32(1), I32(n_index_bits - 1) - i)
            cnt = _count(sc_ref, n_tiles, key_axis,
                         lambda x, j: (x == thr) & (j * tile_keys + kidx < cand))
            return jnp.where(cnt <= n_take, cand, m)

        cut = lax.fori_loop(0, n_index_bits, idx_step, jnp.zeros(k_row.shape, I32))

        def drop(j, _):
            x = sc_ref[j]
            sc_ref[j] = jnp.where((x == thr) & (j * tile_keys + kidx >= cut), NEG_INF, x)
            return 0

        lax.fori_loop(0, n_tiles, drop, 0)

    return thr


def _attend_init(m_ref, l_ref, acc_ref):
    m_ref[...] = jnp.full(m_ref.shape, NEG_INF, F32)
    l_ref[...] = jnp.zeros(l_ref.shape, F32)
    acc_ref[...] = jnp.zeros(acc_ref.shape, F32)


def _softmax_step(s, m_prev, l_prev):
    m_new = jnp.maximum(m_prev, jnp.max(s, axis=1, keepdims=True))
    m_safe = jnp.where(m_new == NEG_INF, 0.0, m_new)
    alpha = jnp.exp2(m_prev - m_safe)
    p = jnp.exp2(s - _tile_lanes(m_safe, s.shape[1]))
    l_new = alpha * l_prev + jnp.sum(p, axis=1, keepdims=True)
    return p, m_new, l_new, alpha


_NT = (((1,), (1,)), ((), ()))


def _two_stage(n_tiles, produce, consume, carry, buf_a, buf_b):
    produce(0, buf_a)

    def pair(i, c):
        j0 = 2 * i
        produce(j0 + 1, buf_b)
        c = consume(j0, buf_a, c)
        produce(jnp.minimum(j0 + 2, n_tiles - 1), buf_a)
        c = consume(j0 + 1, buf_b, c)
        return c

    return lax.fori_loop(0, n_tiles // 2, pair, carry)


def _prompt_attend_body(k_top, n_index_bits, qbdt_ref, qiht_ref, kiwi_ref, kb_ref, vt_ref, kib_ref,
                        o_ref, sc_ref, acc_ref, buf_a, buf_b):
    qb = pl.program_id(1)
    n_tiles = 2 * ((qb * Q_BLOCK + Q_BLOCK + 2 * KEY_TILE - 1) // (2 * KEY_TILE))
    qpos = qb * Q_BLOCK + lax.broadcasted_iota(I32, (1, Q_BLOCK), 1)

    w_t = kiwi_ref[...].T[D_IDX:D_IDX + N_IDX_HEADS, :]
    w_pairs = [jnp.concatenate([w_t[2 * p:2 * p + 1], w_t[2 * p + 1:2 * p + 2]], axis=1)
               for p in range(N_IDX_HEADS // 2)]
    krow = lax.broadcasted_iota(I32, (KEY_TILE, Q_BLOCK), 0)

    def score_dots(j, buf):
        start = pl.multiple_of(j * KEY_TILE, KEY_TILE)
        ki_t = kib_ref[pl.ds(start, KEY_TILE), :]
        for p in range(N_IDX_HEADS // 2):
            buf[p] = jnp.dot(ki_t, qiht_ref[p], preferred_element_type=F32)

    def score_sum(j, buf, c):
        acc = jnp.zeros((KEY_TILE, Q_BLOCK), F32)
        for p in range(N_IDX_HEADS // 2):
            d = jnp.maximum(buf[p], 0.0) * w_pairs[p]
            acc = acc + d[:, :Q_BLOCK] + d[:, Q_BLOCK:]
        sc_ref[j] = jnp.where(j * KEY_TILE + krow <= qpos, acc, NEG_INF)
        return c

    _two_stage(n_tiles, score_dots, score_sum, 0, buf_a, buf_b)

    k_row = jnp.minimum(k_top, qpos + 1)
    thr = _select_threshold(sc_ref, n_tiles, k_row, n_index_bits, key_axis=0, unroll=2)
    thr2 = jnp.concatenate([thr, thr], axis=1)

    acc_ref[...] = jnp.zeros(acc_ref.shape, F32)
    m0 = tuple(jnp.full((1, 2 * Q_BLOCK), NEG_INF, F32) for _ in range(N_KV_HEADS))
    l0 = tuple(jnp.zeros((1, 2 * Q_BLOCK), F32) for _ in range(N_KV_HEADS))

    ones_rows = jnp.ones((BF16_SUBLANES, KEY_TILE), BF16)

    def logits(j, buf):
        start = pl.multiple_of(j * KEY_TILE, KEY_TILE)
        k_t = kb_ref[pl.ds(start, KEY_TILE), :]
        for g in range(N_KV_HEADS):
            buf[g] = jnp.dot(k_t, qbdt_ref[g], preferred_element_type=F32)

    def softmax_pv(j, buf, carry):
        ms, ls = carry
        v_t = vt_ref[j]
        sc = sc_ref[j]
        keep = jnp.concatenate([sc, sc], axis=1) >= thr2
        new_m, new_l = [], []
        for g in range(N_KV_HEADS):
            s = jnp.where(keep, buf[g], NEG_INF)
            m_new = jnp.maximum(ms[g], jnp.max(s, axis=0, keepdims=True))
            m_safe = jnp.where(m_new == NEG_INF, 0.0, m_new)
            alpha = jnp.exp2(ms[g] - m_safe)
            p = jnp.exp2(s - m_safe).astype(BF16)
            lhs = jnp.concatenate([v_t[g * HEAD_DIM:(g + 1) * HEAD_DIM, :], ones_rows], axis=0)
            pv = jnp.dot(lhs, p, preferred_element_type=F32)
            new_m.append(m_new)
            new_l.append(alpha * ls[g] + pv[HEAD_DIM:HEAD_DIM + 1])
            acc_ref[g] = acc_ref[g] * alpha + pv[:HEAD_DIM]
        return tuple(new_m), tuple(new_l)

    _, ls = _two_stage(n_tiles, logits, softmax_pv, (m0, l0), buf_a, buf_b)

    blocks = []
    for g in range(N_KV_HEADS):
        og = acc_ref[g] / ls[g]
        blocks += [og[:, :Q_BLOCK], og[:, Q_BLOCK:]]
    o_ref[...] = jnp.concatenate(blocks, axis=0).T


def _prompt_attend(qbdt, qiht, kiwi, kb, vt, kib, k_top):
    bsz, nqb = qbdt.shape[:2]
    seq = kb.shape[1]
    n_index_bits = int(seq).bit_length()
    resident = dict(pipeline_mode=pl.Buffered(1))
    return pl.pallas_call(
        functools.partial(_prompt_attend_body, k_top, n_index_bits),
        grid=(bsz, nqb),
        in_specs=[
            pl.BlockSpec((None, None, N_KV_HEADS, KV_WIDTH, 2 * Q_BLOCK), lambda b, i: (b, i, 0, 0, 0)),
            pl.BlockSpec((None, None, N_IDX_HEADS // 2, LANES, 2 * Q_BLOCK), lambda b, i: (b, i, 0, 0, 0)),
            pl.BlockSpec((Q_BLOCK, LANES), lambda b, i: (b * nqb + i, 0)),
            pl.BlockSpec((None, seq, KV_WIDTH), lambda b, i: (b, 0, 0), **resident),
            pl.BlockSpec((None, seq // KEY_TILE, KV_WIDTH, KEY_TILE), lambda b, i: (b, 0, 0, 0), **resident),
            pl.BlockSpec((None, seq, LANES), lambda b, i: (b, 0, 0), **resident),
        ],
        out_specs=pl.BlockSpec((Q_BLOCK, ATT_WIDTH), lambda b, i: (b * nqb + i, 0)),
        out_shape=jax.ShapeDtypeStruct((bsz * seq, ATT_WIDTH), F32),
        scratch_shapes=[
            pltpu.VMEM((seq // KEY_TILE, KEY_TILE, Q_BLOCK), F32),
            pltpu.VMEM((N_KV_HEADS, HEAD_DIM, 2 * Q_BLOCK), F32),
            pltpu.VMEM((N_KV_HEADS, KEY_TILE, 2 * Q_BLOCK), F32),
            pltpu.VMEM((N_KV_HEADS, KEY_TILE, 2 * Q_BLOCK), F32),
        ],
        compiler_params=pltpu.CompilerParams(dimension_semantics=("arbitrary", "arbitrary"),
                                             vmem_limit_bytes=VMEM_LIMIT),
        name="prompt_attend",
    )(qbdt, qiht, kiwi, kb, vt, kib)


def _sample_attend_body(k_top, n_index_bits, n_pages, t_new,
                        pt_ref, qbd_ref, qih_ref, w_ref, knew_ref, vnew_ref, kinew_ref,
                        cik_hbm, ck_hbm, cv_hbm, o_ref,
                        ibuf, kbuf, vbuf, sem, sc_ref, m_ref, l_ref, acc_ref):
    b = pl.program_id(0)
    nb = pl.num_programs(0)
    slot = b % 2
    cp = SAMPLE_CHUNK_PAGES
    n_chunks = n_pages // cp
    width = cp * PAGE_SIZE

    def page_copies(bb, s, p):
        page = pt_ref[bb, p]
        return (pltpu.make_async_copy(cik_hbm.at[page], ibuf.at[s, p], sem.at[s, 0]),
                pltpu.make_async_copy(ck_hbm.at[page], kbuf.at[s, p], sem.at[s, 1]),
                pltpu.make_async_copy(cv_hbm.at[page], vbuf.at[s, p], sem.at[s, 2]))

    def start_fetch(bb, s):
        def body(p, _):
            for c in page_copies(bb, s, p):
                c.start()
            return 0
        lax.fori_loop(0, n_pages, body, 0)

    def wait_fetch(bb, s):
        def body(p, _):
            for c in page_copies(bb, s, p):
                c.wait()
            return 0
        lax.fori_loop(0, n_pages, body, 0)

    @pl.when(b == 0)
    def _():
        start_fetch(b, slot)

    @pl.when(b + 1 < nb)
    def _():
        start_fetch(b + 1, 1 - slot)

    wait_fetch(b, slot)

    def chunk(buf, c):
        return jnp.concatenate([buf[slot, c * cp + i] for i in range(cp)], axis=1).astype(BF16)

    def new_cols(a):
        padded = jnp.concatenate([a, jnp.zeros((LANES - a.shape[0], a.shape[1]), a.dtype)], axis=0)
        return padded.T.astype(BF16)

    qih = qih_ref[...]
    w = _tile_lanes(w_ref[...], width)

    def head_sum(ki_c):
        d = jnp.dot(qih, ki_c, preferred_element_type=F32)
        d = w[:, :ki_c.shape[1]] * jnp.maximum(d, 0.0)
        acc = d[0:t_new]
        for h in range(1, N_IDX_HEADS):
            acc = acc + d[h * t_new:(h + 1) * t_new]
        return acc

    def score_chunk(c, _):
        sc_ref[c] = head_sum(chunk(ibuf, c))
        return 0

    lax.fori_loop(0, n_chunks, score_chunk, 0)

    tq = lax.broadcasted_iota(I32, (t_new, LANES), 0)
    col = lax.broadcasted_iota(I32, (t_new, LANES), 1)
    sc_new = jnp.where((col < t_new) & (col <= tq), head_sum(new_cols(kinew_ref[...])[:D_IDX]), NEG_INF)
    sc_ref[n_chunks] = jnp.concatenate([sc_new, jnp.full((t_new, width - LANES), NEG_INF, F32)], axis=1)

    qpos = n_pages * PAGE_SIZE + lax.broadcasted_iota(I32, (t_new, 1), 0)
    k_row = jnp.minimum(k_top, qpos + 1)
    thr = _select_threshold(sc_ref, n_chunks + 1, k_row, n_index_bits, key_axis=1, unroll=n_chunks + 1)

    _attend_init(m_ref, l_ref, acc_ref)
    qbd = qbd_ref[...]

    def attend(sc, k_c, v_c):
        bias = jnp.where(sc >= thr, 0.0, NEG_INF)
        s = jnp.dot(qbd, k_c, preferred_element_type=F32) + jnp.concatenate([bias] * N_HEADS, axis=0)
        p, m_new, l_new, alpha = _softmax_step(s, m_ref[...], l_ref[...])
        m_ref[...] = m_new
        l_ref[...] = l_new
        acc_ref[...] = (acc_ref[...] * _tile_lanes(alpha, KV_WIDTH)
                        + lax.dot_general(p.astype(BF16), v_c, _NT, preferred_element_type=F32))

    def attend_chunk(c, _):
        attend(sc_ref[c], chunk(kbuf, c), chunk(vbuf, c))
        return 0

    lax.fori_loop(0, n_chunks, attend_chunk, 0)
    attend(sc_ref[n_chunks][:, :LANES], new_cols(knew_ref[...]), new_cols(vnew_ref[...]))

    o_ref[...] = acc_ref[...] / _tile_lanes(l_ref[...], KV_WIDTH)


def _sample_attend(page_table, qbd, qih, w_rows, knew, vnew, kinew, cik_t, ck_t, cv_t, k_top):
    dbsz, n_pages = page_table.shape
    t_new = knew.shape[1]
    rows = N_HEADS * t_new
    past = n_pages * PAGE_SIZE
    assert n_pages % SAMPLE_CHUNK_PAGES == 0
    width = SAMPLE_CHUNK_PAGES * PAGE_SIZE
    n_chunks = n_pages // SAMPLE_CHUNK_PAGES
    n_index_bits = int(past + width).bit_length()

    def bmap(b, pt):
        return (b, 0, 0)

    grid_spec = pltpu.PrefetchScalarGridSpec(
        num_scalar_prefetch=1,
        grid=(dbsz,),
        in_specs=[
            pl.BlockSpec((None, rows, KV_WIDTH), bmap),
            pl.BlockSpec((None, rows, D_IDX), bmap),
            pl.BlockSpec((None, rows, LANES), bmap),
            pl.BlockSpec((None, t_new, KV_WIDTH), bmap),
            pl.BlockSpec((None, t_new, KV_WIDTH), bmap),
            pl.BlockSpec((None, t_new, LANES), bmap),
            pl.BlockSpec(memory_space=pl.ANY),
            pl.BlockSpec(memory_space=pl.ANY),
            pl.BlockSpec(memory_space=pl.ANY),
        ],
        out_specs=pl.BlockSpec((None, rows, KV_WIDTH), bmap),
        scratch_shapes=[
            pltpu.VMEM((2, n_pages, D_IDX, PAGE_SIZE), F32),
            pltpu.VMEM((2, n_pages, KV_WIDTH, PAGE_SIZE), F32),
            pltpu.VMEM((2, n_pages, KV_WIDTH, PAGE_SIZE), F32),
            pltpu.SemaphoreType.DMA((2, 3)),
            pltpu.VMEM((n_chunks + 1, t_new, width), F32),
            pltpu.VMEM((rows, LANES), F32),
            pltpu.VMEM((rows, LANES), F32),
            pltpu.VMEM((rows, KV_WIDTH), F32),
        ],
    )
    return pl.pallas_call(
        functools.partial(_sample_attend_body, k_top, n_index_bits, n_pages, t_new),
        grid_spec=grid_spec,
        out_shape=jax.ShapeDtypeStruct((dbsz, rows, KV_WIDTH), F32),
        compiler_params=pltpu.CompilerParams(dimension_semantics=("arbitrary",),
                                             vmem_limit_bytes=VMEM_LIMIT),
        name="sample_attend",
    )(page_table, qbd, qih, w_rows, knew, vnew, kinew, cik_t, ck_t, cv_t)


def _finish_body(x_ref, o_ref, ga_ref, gm_ref, p_ref, wo_ref, pg_ref, wg_ref, wp_ref, y_ref):
    att = (o_ref[...] * ga_ref[...]).astype(BF16)
    r = (x_ref[...]
         + jnp.dot(att, wo_ref[:ATT_WIDTH, :], preferred_element_type=F32)
         + jnp.dot(gm_ref[...].astype(BF16), wo_ref[ATT_WIDTH:, :], preferred_element_type=F32))
    rn = r * lax.rsqrt(jnp.mean(r * r, axis=-1, keepdims=True) + EPS) * pg_ref[...]
    gate = jax.nn.sigmoid(jnp.dot(rn.astype(BF16), wg_ref[...], preferred_element_type=F32))
    y_ref[...] = r + gate * jnp.dot(p_ref[...].astype(BF16), wp_ref[...], preferred_element_type=F32)


def _finish(x2, o, ga, gm, p2, wo, pg, wg, wp):
    rows, d_model = x2.shape
    tm = min(ROW_TILE, rows)

    def rspec(a):
        return pl.BlockSpec((tm, a.shape[1]), lambda i: (i, 0))

    def cspec(a):
        return pl.BlockSpec(a.shape, lambda i: (0, 0))

    return pl.pallas_call(
        _finish_body,
        grid=(rows // tm,),
        in_specs=[rspec(x2), rspec(o), rspec(ga), rspec(gm), rspec(p2),
                  cspec(wo), cspec(pg), cspec(wg), cspec(wp)],
        out_specs=pl.BlockSpec((tm, d_model), lambda i: (i, 0)),
        out_shape=jax.ShapeDtypeStruct((rows, d_model), F32),
        compiler_params=pltpu.CompilerParams(dimension_semantics=("arbitrary",),
                                             vmem_limit_bytes=VMEM_LIMIT),
        name="finish",
    )(x2, o, ga, gm, p2, wo, pg, wg, wp)


def _rope_tables(pos):
    inv = jnp.power(ROPE_THETA, -jnp.arange(ROPE_HALF, dtype=F32) * 2.0 / ROPE_ROT)
    ang = pos.astype(F32)[:, None] * inv[None, :]
    cos, sin = jnp.cos(ang), jnp.sin(ang)
    n = pos.shape[0]
    one = jnp.ones((n, HEAD_DIM - ROPE_ROT), F32)
    zero8 = jnp.zeros((n, ROPE_HALF), F32)
    zero = jnp.zeros((n, HEAD_DIM - ROPE_ROT), F32)
    cos_h = jnp.concatenate([cos, cos, one], axis=1)
    sa_h = jnp.concatenate([-sin, zero8, zero], axis=1)
    sb_h = jnp.concatenate([zero8, sin, zero], axis=1)
    return tuple(jnp.concatenate([t, t], axis=1) for t in (cos_h, sa_h, sb_h))


def _pair_mix(ws):
    g = ws.shape[0]
    return ws.reshape(g // 2, 2, CHUNK, CHUNK).transpose(0, 2, 1, 3).reshape(g // 2, CHUNK, 2 * CHUNK).astype(BF16)


def _group_diag(q, lead):
    t = q.shape[-2]
    n = len(lead)
    q6 = q.reshape(*lead, t, N_KV_HEADS, 2, HEAD_DIM)
    q6 = jnp.moveaxis(q6, n, n + 2)
    q6 = q6.reshape(*lead, N_KV_HEADS, 2 * t, HEAD_DIM)
    eye = jnp.eye(N_KV_HEADS, dtype=q.dtype)
    qd = q6[..., None, :] * eye[:, None, :, None]
    return qd.reshape(*lead, N_KV_HEADS, 2 * t, KV_WIDTH)


def _heads_to_cols(x):
    b, n, nq, width = x.shape
    x6 = x.reshape(b, n, nq, width // (2 * HEAD_DIM), 2, HEAD_DIM)
    return x6.transpose(0, 1, 3, 5, 4, 2).reshape(b, n, width // (2 * HEAD_DIM), HEAD_DIM, 2 * nq)


def kernel(x_prompt, x_sample, cache_k, cache_v, cache_idx_k, page_table, p_prompt, p_sample,
           norm_in_g, w_in, q_norm_g, k_norm_g, ln_v_g, ln_v_b, w_s, b_s, w_out,
           ple_norm_g, w_ple_gate, w_ple_proj):
    depth = w_in.shape[0]
    assert depth == 1
    bsz, seq, d_model = x_prompt.shape
    dbsz, t_new, _ = x_sample.shape
    n_pages = page_table.shape[1]
    past = n_pages * PAGE_SIZE
    assert seq % KEY_TILE == 0 and CHUNK % t_new == 0 and (dbsz * t_new) % CHUNK == 0

    perm = np.concatenate([np.arange(0, _ORIG_KI), np.arange(_ORIG_GA, _ORIG_END),
                           np.arange(_ORIG_KI, _ORIG_GA)])
    w = jnp.pad(w_in[0][:, perm], ((0, 0), (0, C_END - _ORIG_END))).astype(BF16)
    seg = jnp.asarray(np.kron(np.eye(N_HEADS), np.full((HEAD_DIM, HEAD_DIM), 1.0 / HEAD_DIM)), BF16)
    weights = (norm_in_g[0][None, :], w, jnp.tile(q_norm_g[0], N_HEADS)[None, :],
               jnp.tile(k_norm_g[0], N_KV_HEADS)[None, :], ln_v_g[0][None, :], ln_v_b[0][None, :], seg)
    ws_tril = jnp.where(jnp.tril(jnp.ones((CHUNK, CHUNK), bool))[None], w_s[0], 0.0)
    bias_p = jnp.repeat(b_s[0].T, GM_WIDTH // GM_GROUPS, axis=1)
    reps = CHUNK // t_new
    ws_s = jnp.einsum("ab,gij->gaibj", jnp.eye(reps, dtype=F32),
                      ws_tril[:, :t_new, :t_new]).reshape(GM_GROUPS, CHUNK, CHUNK)
    bias_s = jnp.tile(bias_p[:t_new], (reps, 1))
    wo = w_out[0].astype(BF16)
    wg = w_ple_gate[0].astype(BF16)
    wp = w_ple_proj[0].astype(BF16)
    pg = ple_norm_g[0][None, :]

    xp = x_prompt.reshape(bsz * seq, d_model)
    (q, k, v, kb, qi, kiwi, kib, ga, gm, vt) = _project(
        xp, _rope_tables(jnp.arange(seq)), _pair_mix(ws_tril), bias_p, weights,
        emit_vn=False, emit_vt=True)
    nqb = seq // Q_BLOCK
    k_top_p = min(TOPK_MAX, seq // 4)
    eye = jnp.eye(N_KV_HEADS, dtype=BF16)
    qt = _heads_to_cols(q.reshape(bsz, nqb, Q_BLOCK, ATT_WIDTH))
    qbdt = (qt[:, :, :, None] * eye[:, :, None, None]).reshape(bsz, nqb, N_KV_HEADS, KV_WIDTH, 2 * Q_BLOCK)
    qiht = jnp.pad(_heads_to_cols(qi.reshape(bsz, nqb, Q_BLOCK, IDX_WIDTH)),
                   ((0, 0),) * 3 + ((0, LANES - D_IDX), (0, 0)))
    o = _prompt_attend(qbdt, qiht, kiwi, kb.reshape(bsz, seq, KV_WIDTH),
                       vt.reshape(bsz, seq // KEY_TILE, KV_WIDTH, KEY_TILE),
                       kib.reshape(bsz, seq, LANES), k_top_p)
    y_prompt = _finish(xp, o, ga, gm, p_prompt[0].reshape(bsz * seq, -1), wo, pg, wg, wp)

    xs = x_sample.reshape(dbsz * t_new, d_model)
    pos_s = past + jnp.arange(min(ROW_TILE, dbsz * t_new)) % t_new
    (q, k_s, v_s, _, qi, kiwi_s, _, ga, gm, vn_s) = _project(
        xs, _rope_tables(pos_s), _pair_mix(ws_s), bias_s, weights, emit_vn=True, emit_vt=False)
    k_top_s = min(TOPK_MAX, (past + t_new) // 4)
    qbd = _group_diag(q.reshape(dbsz, t_new, ATT_WIDTH), (dbsz,)).reshape(dbsz, N_HEADS * t_new, KV_WIDTH)
    qih = qi.reshape(dbsz, t_new, N_IDX_HEADS, D_IDX).transpose(0, 2, 1, 3).reshape(
        dbsz, N_IDX_HEADS * t_new, D_IDX)
    kiwi3 = kiwi_s.reshape(dbsz, t_new, LANES)
    w_rows = kiwi3[:, :, D_IDX:D_IDX + N_IDX_HEADS].transpose(0, 2, 1).reshape(dbsz, N_IDX_HEADS * t_new, 1)
    w_rows = jnp.broadcast_to(w_rows, (dbsz, N_IDX_HEADS * t_new, LANES))
    acc = _sample_attend(page_table, qbd, qih, w_rows,
                         k_s.reshape(dbsz, t_new, KV_WIDTH), v_s.reshape(dbsz, t_new, KV_WIDTH), kiwi3,
                         cache_idx_k[0].transpose(0, 2, 1),
                         cache_k[0].transpose(0, 2, 3, 1).reshape(-1, KV_WIDTH, PAGE_SIZE),
                         cache_v[0].transpose(0, 2, 3, 1).reshape(-1, KV_WIDTH, PAGE_SIZE), k_top_s)
    acc = acc.reshape(dbsz, N_KV_HEADS, 2, t_new, N_KV_HEADS, HEAD_DIM)
    o_s = jnp.stack([acc[:, g, :, :, g, :] for g in range(N_KV_HEADS)], axis=1)
    o_s = o_s.transpose(0, 3, 1, 2, 4).reshape(dbsz * t_new, ATT_WIDTH)
    y_sample = _finish(xs, o_s, ga, gm, p_sample[0].reshape(dbsz * t_new, -1), wo, pg, wg, wp)

    return (y_prompt.reshape(bsz, seq, d_model),
            y_sample.reshape(dbsz, t_new, d_model),
            k.reshape(1, bsz, seq, N_KV_HEADS, HEAD_DIM),
            v.reshape(1, bsz, seq, N_KV_HEADS, HEAD_DIM),
            kiwi[:, :D_IDX].reshape(1, bsz, seq, D_IDX),
            k_s.reshape(1, dbsz, t_new, N_KV_HEADS, HEAD_DIM),
            v_s.reshape(1, dbsz, t_new, N_KV_HEADS, HEAD_DIM),
            kiwi_s[:, :D_IDX].reshape(1, dbsz, t_new, D_IDX),
            vn_s.reshape(1, dbsz, t_new, GM_WIDTH))
```

```python
import functools

import numpy as np
import jax
import jax.numpy as jnp
from jax import lax
from jax.experimental import pallas as pl
from jax.experimental.pallas import tpu as pltpu

F32 = jnp.float32
BF16 = jnp.bfloat16
I32 = jnp.int32
I16 = jnp.int16

HEAD_DIM = 64
N_HEADS = 8
N_KV_HEADS = 4
N_IDX_HEADS = 8
D_IDX = 64
TOPK_MAX = 256
GM_GROUPS = 8
CHUNK = 128
PAGE_SIZE = 128
ROPE_THETA = 500000.0
ROPE_ROT = HEAD_DIM // 4
ROPE_HALF = ROPE_ROT // 2
EPS = 1e-6

ATT_WIDTH = N_HEADS * HEAD_DIM
KV_WIDTH = N_KV_HEADS * HEAD_DIM
IDX_WIDTH = N_IDX_HEADS * D_IDX
GM_WIDTH = GM_GROUPS * 64
LANES = 128
SUBLANES = 8
BF16_SUBLANES = 16
KEY_TILE = 256
Q_BLOCK = 128
ROW_TILE = 256
SAMPLE_CHUNK_PAGES = 8
VMEM_LIMIT = 56 * 1024 * 1024

C_Q, C_K, C_V, C_QI, C_GA, C_U, C_VM, C_GM, C_KIWI, C_END = (
    0, 512, 768, 1024, 1536, 2048, 2560, 3072, 3584, 3712)
_ORIG_KI, _ORIG_GA, _ORIG_END = 1536, 1608, 3656
LOG2_E = 1.4426950408889634
NEG_INF = float("-inf")
INT_MIN = -2 ** 31


def _head_rms(xh, seg, g):
    sq = xh * xh
    hi = sq.astype(BF16)
    lo = (sq - hi.astype(F32)).astype(BF16)
    ms = (jnp.dot(hi, seg, preferred_element_type=F32)
          + jnp.dot(lo, seg, preferred_element_type=F32))
    return xh * lax.rsqrt(ms + EPS) * g


def _rope(xh, cos, sa, sb):
    w = xh.shape[-1]
    return xh * cos + pltpu.roll(xh, w - ROPE_HALF, 1) * sa + pltpu.roll(xh, ROPE_HALF, 1) * sb


def _tile_lanes(t, width):
    reps = width // t.shape[-1]
    return t if reps == 1 else jnp.concatenate([t] * reps, axis=1)


def _project_body(emit_vn, emit_vt, x_ref, ng_ref, w_ref, qg_ref, kg_ref, lng_ref, lnb_ref, mixw_ref,
                  bias_ref, cos_ref, sa_ref, sb_ref, seg_ref, *outs):
    q_o, k_o, v_o, kb_o, qi_o, kiwi_o, kib_o, ga_o, gm_o = outs[:9]
    extra = list(outs[9:])
    vn_o = extra.pop(0) if emit_vn else None
    vt_o = extra.pop(0) if emit_vt else None
    x = x_ref[...]
    h = x * lax.rsqrt(jnp.mean(x * x, axis=-1, keepdims=True) + EPS) * ng_ref[...]
    hb = h.astype(BF16)

    def proj(c0, c1):
        return jnp.dot(hb, w_ref[:, c0:c1], preferred_element_type=F32)

    cos, sa, sb = cos_ref[...], sa_ref[...], sb_ref[...]
    seg = seg_ref[...]

    q = _head_rms(proj(C_Q, C_K), seg, qg_ref[...])
    q = _rope(q, _tile_lanes(cos, ATT_WIDTH), _tile_lanes(sa, ATT_WIDTH), _tile_lanes(sb, ATT_WIDTH))
    q_o[...] = (q * (HEAD_DIM ** -0.5 * LOG2_E)).astype(BF16)

    k = _head_rms(proj(C_K, C_V), seg[:KV_WIDTH, :KV_WIDTH], kg_ref[...])
    k = _rope(k, _tile_lanes(cos, KV_WIDTH), _tile_lanes(sa, KV_WIDTH), _tile_lanes(sb, KV_WIDTH))
    k_o[...] = k
    kb_o[...] = k.astype(BF16)

    v = proj(C_V, C_QI)
    v_o[...] = v
    if emit_vt:
        vt_o[...] = v.T.astype(BF16)

    qi = _rope(proj(C_QI, C_GA), _tile_lanes(cos, IDX_WIDTH), _tile_lanes(sa, IDX_WIDTH),
               _tile_lanes(sb, IDX_WIDTH))
    qi_o[...] = qi.astype(BF16)

    lane = lax.broadcasted_iota(I32, cos.shape, 1)
    is_ki = lane < D_IDX
    wi_scale = jnp.where(lane < D_IDX + N_IDX_HEADS, IDX_WIDTH ** -0.5, 1.0)
    kiwi = _rope(proj(C_KIWI, C_END), jnp.where(is_ki, cos, wi_scale),
                 jnp.where(is_ki, sa, 0.0), jnp.where(is_ki, sb, 0.0))
    kiwi_o[...] = kiwi
    kib_o[...] = kiwi.astype(BF16)

    ga_o[...] = jax.nn.silu(proj(C_GA, C_U))

    vmg = jax.nn.gelu(proj(C_VM, C_GM))
    xc = vmg - jnp.mean(vmg, axis=-1, keepdims=True)
    vn = xc * lax.rsqrt(jnp.mean(xc * xc, axis=-1, keepdims=True) + EPS) * lng_ref[...] + lnb_ref[...]
    if emit_vn:
        vn_o[...] = vn

    lane_c = lax.broadcasted_iota(I32, (CHUNK, LANES), 1)
    rows = x.shape[0]
    chunks = []
    for c in range(rows // CHUNK):
        pieces = []
        for p in range(GM_GROUPS // 2):
            t = vn[c * CHUNK:(c + 1) * CHUNK, p * LANES:(p + 1) * LANES]
            rhs = jnp.concatenate([jnp.where(lane_c < 64, t, 0.0), jnp.where(lane_c >= 64, t, 0.0)],
                                  axis=0).astype(BF16)
            pieces.append(jnp.dot(mixw_ref[p], rhs, preferred_element_type=F32))
        chunks.append(jnp.concatenate(pieces, axis=1) + bias_ref[...])
    s = jnp.concatenate(chunks, axis=0)
    gm_o[...] = jax.nn.gelu(proj(C_U, C_VM)) * s * jax.nn.silu(proj(C_GM, C_KIWI))


def _project(x2, pos_tables, mixw, bias, weights, emit_vn, emit_vt):
    rows = x2.shape[0]
    tm = min(ROW_TILE, rows)
    assert rows % tm == 0 and tm % CHUNK == 0
    ng, w, qg, kg, lng, lnb, seg = weights
    cos, sa, sb = pos_tables
    assert cos.shape[0] % tm == 0
    n_pos_blocks = cos.shape[0] // tm

    def row_map(i):
        return (i, 0)

    def pos_map(i):
        return (i % n_pos_blocks, 0)

    def const(i):
        return (0, 0)

    def rspec(width):
        return pl.BlockSpec((tm, width), row_map)

    def cspec(a):
        if a.ndim == 3:
            return pl.BlockSpec(a.shape, lambda i: (0, 0, 0))
        return pl.BlockSpec(a.shape, const)

    out_shapes = [
        jax.ShapeDtypeStruct((rows, ATT_WIDTH), BF16),
        jax.ShapeDtypeStruct((rows, KV_WIDTH), F32),
        jax.ShapeDtypeStruct((rows, KV_WIDTH), F32),
        jax.ShapeDtypeStruct((rows, KV_WIDTH), BF16),
        jax.ShapeDtypeStruct((rows, IDX_WIDTH), BF16),
        jax.ShapeDtypeStruct((rows, LANES), F32),
        jax.ShapeDtypeStruct((rows, LANES), BF16),
        jax.ShapeDtypeStruct((rows, ATT_WIDTH), F32),
        jax.ShapeDtypeStruct((rows, GM_WIDTH), F32),
    ]
    if emit_vn:
        out_shapes.append(jax.ShapeDtypeStruct((rows, GM_WIDTH), F32))
    out_specs = [rspec(s.shape[1]) for s in out_shapes]
    if emit_vt:
        assert tm == KEY_TILE
        out_shapes.append(jax.ShapeDtypeStruct((rows // tm, KV_WIDTH, tm), BF16))
        out_specs.append(pl.BlockSpec((None, KV_WIDTH, tm), lambda i: (i, 0, 0)))
    in_specs = [rspec(x2.shape[1]), cspec(ng), cspec(w), cspec(qg), cspec(kg), cspec(lng), cspec(lnb),
                cspec(mixw), cspec(bias),
                pl.BlockSpec((tm, LANES), pos_map), pl.BlockSpec((tm, LANES), pos_map),
                pl.BlockSpec((tm, LANES), pos_map), cspec(seg)]
    return pl.pallas_call(
        functools.partial(_project_body, emit_vn, emit_vt),
        grid=(rows // tm,),
        in_specs=in_specs,
        out_specs=out_specs,
        out_shape=out_shapes,
        compiler_params=pltpu.CompilerParams(dimension_semantics=("arbitrary",),
                                             vmem_limit_bytes=VMEM_LIMIT),
        name="project",
    )(x2, ng, w, qg, kg, lng, lnb, mixw, bias, cos, sa, sb, seg)


def _key_to_float(u):
    bits = jnp.where(u < 0, u ^ I32(INT_MIN), ~u)
    return lax.bitcast_convert_type(bits, F32)


def _count(sc_ref, n_tiles, key_axis, pred, unroll=1):
    a, b = sc_ref.shape[1], sc_ref.shape[2]

    def fold(ind):
        if key_axis == 1:
            return ind
        parts = [ind[i * SUBLANES:(i + 1) * SUBLANES] for i in range(a // SUBLANES)]
        while len(parts) > 1:
            parts = [parts[i] + parts[i + 1] for i in range(0, len(parts), 2)]
        return parts[0]

    def body(i, c):
        for u in range(unroll):
            j = i * unroll + u
            c = c + fold(jnp.where(pred(sc_ref[j], j), 1.0, 0.0))
        return c

    init = jnp.zeros((a, b) if key_axis == 1 else (SUBLANES, b), F32)
    c = lax.fori_loop(0, n_tiles // unroll, body, init)
    return jnp.sum(c, axis=key_axis, keepdims=True)


def _select_threshold(sc_ref, n_tiles, k_row, n_index_bits, key_axis, unroll=1):
    k_f = k_row.astype(F32)

    def bit_step(i, carry):
        res, cnt_res = carry
        cand = res | lax.shift_left(I32(1), I32(31) - i)
        t = _key_to_float(cand)
        cnt = _count(sc_ref, n_tiles, key_axis, lambda x, j: x >= t, unroll)
        ok = cnt >= k_f
        return jnp.where(ok, cand, res), jnp.where(ok, cnt, cnt_res)

    res, cnt_res = lax.fori_loop(0, 32, bit_step,
                                 (jnp.zeros(k_row.shape, I32), jnp.zeros(k_row.shape, F32)))
    thr = _key_to_float(res)
    _drop_surplus_ties(sc_ref, n_tiles, thr, k_f, cnt_res, n_index_bits, key_axis)
    return thr


def _drop_surplus_ties(sc_ref, n_tiles, thr, k_f, cnt_ge, n_index_bits, key_axis):
    a, b = sc_ref.shape[1], sc_ref.shape[2]
    tile_keys = sc_ref.shape[1 + key_axis]
    surplus = jnp.max(cnt_ge - k_f)

    @pl.when(surplus > 0.0)
    def _():
        n_take = k_f - _count(sc_ref, n_tiles, key_axis, lambda x, j: x > thr)
        kidx = lax.broadcasted_iota(I32, (a, b), key_axis)

        def idx_step(i, m):
            cand = m | lax.shift_left(I32(1), I32(n_index_bits - 1) - i)
            cnt = _count(sc_ref, n_tiles, key_axis,
                         lambda x, j: (x == thr) & (j * tile_keys + kidx < cand))
            return jnp.where(cnt <= n_take, cand, m)

        cut = lax.fori_loop(0, n_index_bits, idx_step, jnp.zeros(thr.shape, I32))

        def drop(j, _):
            x = sc_ref[j]
            sc_ref[j] = jnp.where((x == thr) & (j * tile_keys + kidx >= cut), NEG_INF, x)
            return 0

        lax.fori_loop(0, n_tiles, drop, 0)


def _float_key(bits):
    return bits ^ ((bits >> 31) & I32(0x7FFFFFFF))


def _count16(ref, n_tiles, pred, unroll):
    a = ref.shape[1]

    def fold(ind):
        parts = [ind[i * BF16_SUBLANES:(i + 1) * BF16_SUBLANES] for i in range(a // BF16_SUBLANES)]
        while len(parts) > 1:
            parts = [parts[i] + parts[i + 1] for i in range(0, len(parts), 2)]
        return parts[0]

    def body(i, c):
        for u in range(unroll):
            c = c + fold(jnp.where(pred(ref[i * unroll + u]), I16(1), I16(0)))
        return c

    c = lax.fori_loop(0, n_tiles // unroll, body, jnp.zeros((BF16_SUBLANES, ref.shape[2]), I16))
    return jnp.sum(c.astype(I32), axis=0, keepdims=True)


def _search16(ref, n_tiles, k_need, cnt_init, unroll):
    def bit_step(i, carry):
        res, cnt_res = carry
        cand = res | lax.shift_left(I32(1), I32(15) - i)
        t = (cand - I32(2 ** 15)).astype(I16)
        cnt = _count16(ref, n_tiles, lambda x: x >= t, unroll)
        ok = cnt >= k_need
        return jnp.where(ok, cand, res), jnp.where(ok, cnt, cnt_res)

    res, cnt = lax.fori_loop(0, 16, bit_step, (jnp.zeros(k_need.shape, I32), cnt_init))
    return res - I32(2 ** 15), cnt


def _select_threshold16(sc_ref, hi_ref, lo_ref, n_tiles, k_row, n_index_bits, unroll):
    t_hi, c_from = _search16(hi_ref, n_tiles, k_row, jnp.zeros(k_row.shape, I32), unroll)
    t_hi16 = t_hi.astype(I16)
    c_above = _count16(hi_ref, n_tiles, lambda x: x > t_hi16, unroll)

    def restrict(j, _):
        lo_ref[j] = jnp.where(hi_ref[j] == t_hi16, lo_ref[j], I16(-2 ** 15))
        return 0

    lax.fori_loop(0, n_tiles, restrict, 0)
    t_lo, c_lo = _search16(lo_ref, n_tiles, k_row - c_above, c_from - c_above, unroll)
    key = lax.shift_left(t_hi, I32(16)) | (t_lo + I32(2 ** 15))
    thr = lax.bitcast_convert_type(_float_key(key), F32)
    _drop_surplus_ties(sc_ref, n_tiles, thr, k_row.astype(F32), (c_above + c_lo).astype(F32),
                       n_index_bits, key_axis=0)
    return thr


def _attend_init(m_ref, l_ref, acc_ref):
    m_ref[...] = jnp.full(m_ref.shape, NEG_INF, F32)
    l_ref[...] = jnp.zeros(l_ref.shape, F32)
    acc_ref[...] = jnp.zeros(acc_ref.shape, F32)


def _softmax_step(s, m_prev, l_prev):
    m_new = jnp.maximum(m_prev, jnp.max(s, axis=1, keepdims=True))
    m_safe = jnp.where(m_new == NEG_INF, 0.0, m_new)
    alpha = jnp.exp2(m_prev - m_safe)
    p = jnp.exp2(s - _tile_lanes(m_safe, s.shape[1]))
    l_new = alpha * l_prev + jnp.sum(p, axis=1, keepdims=True)
    return p, m_new, l_new, alpha


_NT = (((1,), (1,)), ((), ()))


def _two_stage(n_tiles, produce, consume, carry, buf_a, buf_b):
    produce(0, buf_a)

    def pair(i, c):
        j0 = 2 * i
        produce(j0 + 1, buf_b)
        c = consume(j0, buf_a, c)
        produce(jnp.minimum(j0 + 2, n_tiles - 1), buf_a)
        c = consume(j0 + 1, buf_b, c)
        return c

    return lax.fori_loop(0, n_tiles // 2, pair, carry)


def _prompt_attend_body(k_top, n_index_bits, qbdt_ref, qiht_ref, kiwi_ref, kb_ref, vt_ref, kib_ref,
                        o_ref, sc_ref, hi_ref, lo_ref, acc_ref, buf_a, buf_b):
    qb = pl.program_id(1)
    n_tiles = 2 * ((qb * Q_BLOCK + Q_BLOCK + 2 * KEY_TILE - 1) // (2 * KEY_TILE))
    qpos = qb * Q_BLOCK + lax.broadcasted_iota(I32, (1, Q_BLOCK), 1)

    w_t = kiwi_ref[...].T[D_IDX:D_IDX + N_IDX_HEADS, :]
    w_pairs = [jnp.concatenate([w_t[2 * p:2 * p + 1], w_t[2 * p + 1:2 * p + 2]], axis=1)
               for p in range(N_IDX_HEADS // 2)]
    krow = lax.broadcasted_iota(I32, (KEY_TILE, Q_BLOCK), 0)

    def score_dots(j, buf):
        start = pl.multiple_of(j * KEY_TILE, KEY_TILE)
        ki_t = kib_ref[pl.ds(start, KEY_TILE), :]
        for p in range(N_IDX_HEADS // 2):
            buf[p] = jnp.dot(ki_t, qiht_ref[p], preferred_element_type=F32)

    def score_sum(j, buf, c):
        acc = jnp.zeros((KEY_TILE, Q_BLOCK), F32)
        for p in range(N_IDX_HEADS // 2):
            d = jnp.maximum(buf[p], 0.0) * w_pairs[p]
            acc = acc + d[:, :Q_BLOCK] + d[:, Q_BLOCK:]
        x = jnp.where(j * KEY_TILE + krow <= qpos, acc, NEG_INF)
        x = jnp.where(x == 0.0, 0.0, x)
        sc_ref[j] = x
        key = _float_key(lax.bitcast_convert_type(x, I32))
        hi_ref[j] = (key >> 16).astype(I16)
        lo_ref[j] = ((key & I32(0xFFFF)) - I32(2 ** 15)).astype(I16)
        return c

    _two_stage(n_tiles, score_dots, score_sum, 0, buf_a, buf_b)

    k_row = jnp.minimum(k_top, qpos + 1)
    thr = _select_threshold16(sc_ref, hi_ref, lo_ref, n_tiles, k_row, n_index_bits, unroll=2)
    thr2 = jnp.concatenate([thr, thr], axis=1)

    acc_ref[...] = jnp.zeros(acc_ref.shape, F32)
    m0 = tuple(jnp.full((1, 2 * Q_BLOCK), NEG_INF, F32) for _ in range(N_KV_HEADS))
    l0 = tuple(jnp.zeros((1, 2 * Q_BLOCK), F32) for _ in range(N_KV_HEADS))

    ones_rows = jnp.ones((BF16_SUBLANES, KEY_TILE), BF16)

    def logits(j, buf):
        start = pl.multiple_of(j * KEY_TILE, KEY_TILE)
        k_t = kb_ref[pl.ds(start, KEY_TILE), :]
        for g in range(N_KV_HEADS):
            buf[g] = jnp.dot(k_t, qbdt_ref[g], preferred_element_type=F32)

    def softmax_pv(j, buf, carry):
        ms, ls = carry
        v_t = vt_ref[j]
        sc = sc_ref[j]
        keep = jnp.concatenate([sc, sc], axis=1) >= thr2
        new_m, new_l = [], []
        for g in range(N_KV_HEADS):
            s = jnp.where(keep, buf[g], NEG_INF)
            m_new = jnp.maximum(ms[g], jnp.max(s, axis=0, keepdims=True))
            m_safe = jnp.where(m_new == NEG_INF, 0.0, m_new)
            alpha = jnp.exp2(ms[g] - m_safe)
            p = jnp.exp2(s - m_safe).astype(BF16)
            lhs = jnp.concatenate([v_t[g * HEAD_DIM:(g + 1) * HEAD_DIM, :], ones_rows], axis=0)
            pv = jnp.dot(lhs, p, preferred_element_type=F32)
            new_m.append(m_new)
            new_l.append(alpha * ls[g] + pv[HEAD_DIM:HEAD_DIM + 1])
            acc_ref[g] = acc_ref[g] * alpha + pv[:HEAD_DIM]
        return tuple(new_m), tuple(new_l)

    _, ls = _two_stage(n_tiles, logits, softmax_pv, (m0, l0), buf_a, buf_b)

    blocks = []
    for g in range(N_KV_HEADS):
        og = acc_ref[g] / ls[g]
        blocks += [og[:, :Q_BLOCK], og[:, Q_BLOCK:]]
    o_ref[...] = jnp.concatenate(blocks, axis=0).T


def _prompt_attend(qbdt, qiht, kiwi, kb, vt, kib, k_top):
    bsz, nqb = qbdt.shape[:2]
    seq = kb.shape[1]
    n_index_bits = int(seq).bit_length()
    resident = dict(pipeline_mode=pl.Buffered(1))
    return pl.pallas_call(
        functools.partial(_prompt_attend_body, k_top, n_index_bits),
        grid=(bsz, nqb),
        in_specs=[
            pl.BlockSpec((None, None, N_KV_HEADS, KV_WIDTH, 2 * Q_BLOCK), lambda b, i: (b, i, 0, 0, 0)),
            pl.BlockSpec((None, None, N_IDX_HEADS // 2, LANES, 2 * Q_BLOCK), lambda b, i: (b, i, 0, 0, 0)),
            pl.BlockSpec((Q_BLOCK, LANES), lambda b, i: (b * nqb + i, 0)),
            pl.BlockSpec((None, seq, KV_WIDTH), lambda b, i: (b, 0, 0), **resident),
            pl.BlockSpec((None, seq // KEY_TILE, KV_WIDTH, KEY_TILE), lambda b, i: (b, 0, 0, 0), **resident),
            pl.BlockSpec((None, seq, LANES), lambda b, i: (b, 0, 0), **resident),
        ],
        out_specs=pl.BlockSpec((Q_BLOCK, ATT_WIDTH), lambda b, i: (b * nqb + i, 0)),
        out_shape=jax.ShapeDtypeStruct((bsz * seq, ATT_WIDTH), F32),
        scratch_shapes=[
            pltpu.VMEM((seq // KEY_TILE, KEY_TILE, Q_BLOCK), F32),
            pltpu.VMEM((seq // KEY_TILE, KEY_TILE, Q_BLOCK), I16),
            pltpu.VMEM((seq // KEY_TILE, KEY_TILE, Q_BLOCK), I16),
            pltpu.VMEM((N_KV_HEADS, HEAD_DIM, 2 * Q_BLOCK), F32),
            pltpu.VMEM((N_KV_HEADS, KEY_TILE, 2 * Q_BLOCK), F32),
            pltpu.VMEM((N_KV_HEADS, KEY_TILE, 2 * Q_BLOCK), F32),
        ],
        compiler_params=pltpu.CompilerParams(dimension_semantics=("arbitrary", "arbitrary"),
                                             vmem_limit_bytes=VMEM_LIMIT),
        name="prompt_attend",
    )(qbdt, qiht, kiwi, kb, vt, kib)


def _sample_attend_body(k_top, n_index_bits, n_pages, t_new,
                        pt_ref, qbd_ref, qih_ref, w_ref, knew_ref, vnew_ref, kinew_ref,
                        cik_hbm, ck_hbm, cv_hbm, o_ref,
                        ibuf, kbuf, vbuf, sem, sc_ref, m_ref, l_ref, acc_ref):
    b = pl.program_id(0)
    nb = pl.num_programs(0)
    slot = b % 2
    cp = SAMPLE_CHUNK_PAGES
    n_chunks = n_pages // cp
    width = cp * PAGE_SIZE

    def page_copies(bb, s, p):
        page = pt_ref[bb, p]
        return (pltpu.make_async_copy(cik_hbm.at[page], ibuf.at[s, p], sem.at[s, 0]),
                pltpu.make_async_copy(ck_hbm.at[page], kbuf.at[s, p], sem.at[s, 1]),
                pltpu.make_async_copy(cv_hbm.at[page], vbuf.at[s, p], sem.at[s, 2]))

    def start_fetch(bb, s):
        def body(p, _):
            for c in page_copies(bb, s, p):
                c.start()
            return 0
        lax.fori_loop(0, n_pages, body, 0)

    def wait_fetch(bb, s):
        def body(p, _):
            for c in page_copies(bb, s, p):
                c.wait()
            return 0
        lax.fori_loop(0, n_pages, body, 0)

    @pl.when(b == 0)
    def _():
        start_fetch(b, slot)

    @pl.when(b + 1 < nb)
    def _():
        start_fetch(b + 1, 1 - slot)

    wait_fetch(b, slot)

    def chunk(buf, c):
        return jnp.concatenate([buf[slot, c * cp + i] for i in range(cp)], axis=1).astype(BF16)

    def new_cols(a):
        padded = jnp.concatenate([a, jnp.zeros((LANES - a.shape[0], a.shape[1]), a.dtype)], axis=0)
        return padded.T.astype(BF16)

    qih = qih_ref[...]
    w = _tile_lanes(w_ref[...], width)

    def head_sum(ki_c):
        d = jnp.dot(qih, ki_c, preferred_element_type=F32)
        d = w[:, :ki_c.shape[1]] * jnp.maximum(d, 0.0)
        acc = d[0:t_new]
        for h in range(1, N_IDX_HEADS):
            acc = acc + d[h * t_new:(h + 1) * t_new]
        return acc

    def score_chunk(c, _):
        sc_ref[c] = head_sum(chunk(ibuf, c))
        return 0

    lax.fori_loop(0, n_chunks, score_chunk, 0)

    tq = lax.broadcasted_iota(I32, (t_new, LANES), 0)
    col = lax.broadcasted_iota(I32, (t_new, LANES), 1)
    sc_new = jnp.where((col < t_new) & (col <= tq), head_sum(new_cols(kinew_ref[...])[:D_IDX]), NEG_INF)
    sc_ref[n_chunks] = jnp.concatenate([sc_new, jnp.full((t_new, width - LANES), NEG_INF, F32)], axis=1)

    qpos = n_pages * PAGE_SIZE + lax.broadcasted_iota(I32, (t_new, 1), 0)
    k_row = jnp.minimum(k_top, qpos + 1)
    thr = _select_threshold(sc_ref, n_chunks + 1, k_row, n_index_bits, key_axis=1, unroll=n_chunks + 1)

    _attend_init(m_ref, l_ref, acc_ref)
    qbd = qbd_ref[...]

    def attend(sc, k_c, v_c):
        bias = jnp.where(sc >= thr, 0.0, NEG_INF)
        s = jnp.dot(qbd, k_c, preferred_element_type=F32) + jnp.concatenate([bias] * N_HEADS, axis=0)
        p, m_new, l_new, alpha = _softmax_step(s, m_ref[...], l_ref[...])
        m_ref[...] = m_new
        l_ref[...] = l_new
        acc_ref[...] = (acc_ref[...] * _tile_lanes(alpha, KV_WIDTH)
                        + lax.dot_general(p.astype(BF16), v_c, _NT, preferred_element_type=F32))

    def attend_chunk(c, _):
        attend(sc_ref[c], chunk(kbuf, c), chunk(vbuf, c))
        return 0

    lax.fori_loop(0, n_chunks, attend_chunk, 0)
    attend(sc_ref[n_chunks][:, :LANES], new_cols(knew_ref[...]), new_cols(vnew_ref[...]))

    o_ref[...] = acc_ref[...] / _tile_lanes(l_ref[...], KV_WIDTH)


def _sample_attend(page_table, qbd, qih, w_rows, knew, vnew, kinew, cik_t, ck_t, cv_t, k_top):
    dbsz, n_pages = page_table.shape
    t_new = knew.shape[1]
    rows = N_HEADS * t_new
    past = n_pages * PAGE_SIZE
    assert n_pages % SAMPLE_CHUNK_PAGES == 0
    width = SAMPLE_CHUNK_PAGES * PAGE_SIZE
    n_chunks = n_pages // SAMPLE_CHUNK_PAGES
    n_index_bits = int(past + width).bit_length()

    def bmap(b, pt):
        return (b, 0, 0)

    grid_spec = pltpu.PrefetchScalarGridSpec(
        num_scalar_prefetch=1,
        grid=(dbsz,),
        in_specs=[
            pl.BlockSpec((None, rows, KV_WIDTH), bmap),
            pl.BlockSpec((None, rows, D_IDX), bmap),
            pl.BlockSpec((None, rows, LANES), bmap),
            pl.BlockSpec((None, t_new, KV_WIDTH), bmap),
            pl.BlockSpec((None, t_new, KV_WIDTH), bmap),
            pl.BlockSpec((None, t_new, LANES), bmap),
            pl.BlockSpec(memory_space=pl.ANY),
            pl.BlockSpec(memory_space=pl.ANY),
            pl.BlockSpec(memory_space=pl.ANY),
        ],
        out_specs=pl.BlockSpec((None, rows, KV_WIDTH), bmap),
        scratch_shapes=[
            pltpu.VMEM((2, n_pages, D_IDX, PAGE_SIZE), F32),
            pltpu.VMEM((2, n_pages, KV_WIDTH, PAGE_SIZE), F32),
            pltpu.VMEM((2, n_pages, KV_WIDTH, PAGE_SIZE), F32),
            pltpu.SemaphoreType.DMA((2, 3)),
            pltpu.VMEM((n_chunks + 1, t_new, width), F32),
            pltpu.VMEM((rows, LANES), F32),
            pltpu.VMEM((rows, LANES), F32),
            pltpu.VMEM((rows, KV_WIDTH), F32),
        ],
    )
    return pl.pallas_call(
        functools.partial(_sample_attend_body, k_top, n_index_bits, n_pages, t_new),
        grid_spec=grid_spec,
        out_shape=jax.ShapeDtypeStruct((dbsz, rows, KV_WIDTH), F32),
        compiler_params=pltpu.CompilerParams(dimension_semantics=("arbitrary",),
                                             vmem_limit_bytes=VMEM_LIMIT),
        name="sample_attend",
    )(page_table, qbd, qih, w_rows, knew, vnew, kinew, cik_t, ck_t, cv_t)


def _finish_body(x_ref, o_ref, ga_ref, gm_ref, p_ref, wo_ref, pg_ref, wg_ref, wp_ref, y_ref):
    att = (o_ref[...] * ga_ref[...]).astype(BF16)
    r = (x_ref[...]
         + jnp.dot(att, wo_ref[:ATT_WIDTH, :], preferred_element_type=F32)
         + jnp.dot(gm_ref[...].astype(BF16), wo_ref[ATT_WIDTH:, :], preferred_element_type=F32))
    rn = r * lax.rsqrt(jnp.mean(r * r, axis=-1, keepdims=True) + EPS) * pg_ref[...]
    gate = jax.nn.sigmoid(jnp.dot(rn.astype(BF16), wg_ref[...], preferred_element_type=F32))
    y_ref[...] = r + gate * jnp.dot(p_ref[...].astype(BF16), wp_ref[...], preferred_element_type=F32)


def _finish(x2, o, ga, gm, p2, wo, pg, wg, wp):
    rows, d_model = x2.shape
    tm = min(ROW_TILE, rows)

    def rspec(a):
        return pl.BlockSpec((tm, a.shape[1]), lambda i: (i, 0))

    def cspec(a):
        return pl.BlockSpec(a.shape, lambda i: (0, 0))

    return pl.pallas_call(
        _finish_body,
        grid=(rows // tm,),
        in_specs=[rspec(x2), rspec(o), rspec(ga), rspec(gm), rspec(p2),
                  cspec(wo), cspec(pg), cspec(wg), cspec(wp)],
        out_specs=pl.BlockSpec((tm, d_model), lambda i: (i, 0)),
        out_shape=jax.ShapeDtypeStruct((rows, d_model), F32),
        compiler_params=pltpu.CompilerParams(dimension_semantics=("arbitrary",),
                                             vmem_limit_bytes=VMEM_LIMIT),
        name="finish",
    )(x2, o, ga, gm, p2, wo, pg, wg, wp)


def _rope_tables(pos):
    inv = jnp.power(ROPE_THETA, -jnp.arange(ROPE_HALF, dtype=F32) * 2.0 / ROPE_ROT)
    ang = pos.astype(F32)[:, None] * inv[None, :]
    cos, sin = jnp.cos(ang), jnp.sin(ang)
    n = pos.shape[0]
    one = jnp.ones((n, HEAD_DIM - ROPE_ROT), F32)
    zero8 = jnp.zeros((n, ROPE_HALF), F32)
    zero = jnp.zeros((n, HEAD_DIM - ROPE_ROT), F32)
    cos_h = jnp.concatenate([cos, cos, one], axis=1)
    sa_h = jnp.concatenate([-sin, zero8, zero], axis=1)
    sb_h = jnp.concatenate([zero8, sin, zero], axis=1)
    return tuple(jnp.concatenate([t, t], axis=1) for t in (cos_h, sa_h, sb_h))


def _pair_mix(ws):
    g = ws.shape[0]
    return ws.reshape(g // 2, 2, CHUNK, CHUNK).transpose(0, 2, 1, 3).reshape(g // 2, CHUNK, 2 * CHUNK).astype(BF16)


def _group_diag(q, lead):
    t = q.shape[-2]
    n = len(lead)
    q6 = q.reshape(*lead, t, N_KV_HEADS, 2, HEAD_DIM)
    q6 = jnp.moveaxis(q6, n, n + 2)
    q6 = q6.reshape(*lead, N_KV_HEADS, 2 * t, HEAD_DIM)
    eye = jnp.eye(N_KV_HEADS, dtype=q.dtype)
    qd = q6[..., None, :] * eye[:, None, :, None]
    return qd.reshape(*lead, N_KV_HEADS, 2 * t, KV_WIDTH)


def _heads_to_cols(x):
    b, n, nq, width = x.shape
    x6 = x.reshape(b, n, nq, width // (2 * HEAD_DIM), 2, HEAD_DIM)
    return x6.transpose(0, 1, 3, 5, 4, 2).reshape(b, n, width // (2 * HEAD_DIM), HEAD_DIM, 2 * nq)


def kernel(x_prompt, x_sample, cache_k, cache_v, cache_idx_k, page_table, p_prompt, p_sample,
           norm_in_g, w_in, q_norm_g, k_norm_g, ln_v_g, ln_v_b, w_s, b_s, w_out,
           ple_norm_g, w_ple_gate, w_ple_proj):
    depth = w_in.shape[0]
    assert depth == 1
    bsz, seq, d_model = x_prompt.shape
    dbsz, t_new, _ = x_sample.shape
    n_pages = page_table.shape[1]
    past = n_pages * PAGE_SIZE
    assert seq % KEY_TILE == 0 and CHUNK % t_new == 0 and (dbsz * t_new) % CHUNK == 0

    perm = np.concatenate([np.arange(0, _ORIG_KI), np.arange(_ORIG_GA, _ORIG_END),
                           np.arange(_ORIG_KI, _ORIG_GA)])
    w = jnp.pad(w_in[0][:, perm], ((0, 0), (0, C_END - _ORIG_END))).astype(BF16)
    seg = jnp.asarray(np.kron(np.eye(N_HEADS), np.full((HEAD_DIM, HEAD_DIM), 1.0 / HEAD_DIM)), BF16)
    weights = (norm_in_g[0][None, :], w, jnp.tile(q_norm_g[0], N_HEADS)[None, :],
               jnp.tile(k_norm_g[0], N_KV_HEADS)[None, :], ln_v_g[0][None, :], ln_v_b[0][None, :], seg)
    ws_tril = jnp.where(jnp.tril(jnp.ones((CHUNK, CHUNK), bool))[None], w_s[0], 0.0)
    bias_p = jnp.repeat(b_s[0].T, GM_WIDTH // GM_GROUPS, axis=1)
    reps = CHUNK // t_new
    ws_s = jnp.einsum("ab,gij->gaibj", jnp.eye(reps, dtype=F32),
                      ws_tril[:, :t_new, :t_new]).reshape(GM_GROUPS, CHUNK, CHUNK)
    bias_s = jnp.tile(bias_p[:t_new], (reps, 1))
    wo = w_out[0].astype(BF16)
    wg = w_ple_gate[0].astype(BF16)
    wp = w_ple_proj[0].astype(BF16)
    pg = ple_norm_g[0][None, :]

    xp = x_prompt.reshape(bsz * seq, d_model)
    (q, k, v, kb, qi, kiwi, kib, ga, gm, vt) = _project(
        xp, _rope_tables(jnp.arange(seq)), _pair_mix(ws_tril), bias_p, weights,
        emit_vn=False, emit_vt=True)
    nqb = seq // Q_BLOCK
    k_top_p = min(TOPK_MAX, seq // 4)
    eye = jnp.eye(N_KV_HEADS, dtype=BF16)
    qt = _heads_to_cols(q.reshape(bsz, nqb, Q_BLOCK, ATT_WIDTH))
    qbdt = (qt[:, :, :, None] * eye[:, :, None, None]).reshape(bsz, nqb, N_KV_HEADS, KV_WIDTH, 2 * Q_BLOCK)
    qiht = jnp.pad(_heads_to_cols(qi.reshape(bsz, nqb, Q_BLOCK, IDX_WIDTH)),
                   ((0, 0),) * 3 + ((0, LANES - D_IDX), (0, 0)))
    o = _prompt_attend(qbdt, qiht, kiwi, kb.reshape(bsz, seq, KV_WIDTH),
                       vt.reshape(bsz, seq // KEY_TILE, KV_WIDTH, KEY_TILE),
                       kib.reshape(bsz, seq, LANES), k_top_p)
    y_prompt = _finish(xp, o, ga, gm, p_prompt[0].reshape(bsz * seq, -1), wo, pg, wg, wp)

    xs = x_sample.reshape(dbsz * t_new, d_model)
    pos_s = past + jnp.arange(min(ROW_TILE, dbsz * t_new)) % t_new
    (q, k_s, v_s, _, qi, kiwi_s, _, ga, gm, vn_s) = _project(
        xs, _rope_tables(pos_s), _pair_mix(ws_s), bias_s, weights, emit_vn=True, emit_vt=False)
    k_top_s = min(TOPK_MAX, (past + t_new) // 4)
    qbd = _group_diag(q.reshape(dbsz, t_new, ATT_WIDTH), (dbsz,)).reshape(dbsz, N_HEADS * t_new, KV_WIDTH)
    qih = qi.reshape(dbsz, t_new, N_IDX_HEADS, D_IDX).transpose(0, 2, 1, 3).reshape(
        dbsz, N_IDX_HEADS * t_new, D_IDX)
    kiwi3 = kiwi_s.reshape(dbsz, t_new, LANES)
    w_rows = kiwi3[:, :, D_IDX:D_IDX + N_IDX_HEADS].transpose(0, 2, 1).reshape(dbsz, N_IDX_HEADS * t_new, 1)
    w_rows = jnp.broadcast_to(w_rows, (dbsz, N_IDX_HEADS * t_new, LANES))
    acc = _sample_attend(page_table, qbd, qih, w_rows,
                         k_s.reshape(dbsz, t_new, KV_WIDTH), v_s.reshape(dbsz, t_new, KV_WIDTH), kiwi3,
                         cache_idx_k[0].transpose(0, 2, 1),
                         cache_k[0].transpose(0, 2, 3, 1).reshape(-1, KV_WIDTH, PAGE_SIZE),
                         cache_v[0].transpose(0, 2, 3, 1).reshape(-1, KV_WIDTH, PAGE_SIZE), k_top_s)
    acc = acc.reshape(dbsz, N_KV_HEADS, 2, t_new, N_KV_HEADS, HEAD_DIM)
    o_s = jnp.stack([acc[:, g, :, :, g, :] for g in range(N_KV_HEADS)], axis=1)
    o_s = o_s.transpose(0, 3, 1, 2, 4).reshape(dbsz * t_new, ATT_WIDTH)
    y_sample = _finish(xs, o_s, ga, gm, p_sample[0].reshape(dbsz * t_new, -1), wo, pg, wg, wp)

    return (y_prompt.reshape(bsz, seq, d_model),
            y_sample.reshape(dbsz, t_new, d_model),
            k.reshape(1, bsz, seq, N_KV_HEADS, HEAD_DIM),
            v.reshape(1, bsz, seq, N_KV_HEADS, HEAD_DIM),
            kiwi[:, :D_IDX].reshape(1, bsz, seq, D_IDX),
            k_s.reshape(1, dbsz, t_new, N_KV_HEADS, HEAD_DIM),
            v_s.reshape(1, dbsz, t_new, N_KV_HEADS, HEAD_DIM),
            kiwi_s[:, :D_IDX].reshape(1, dbsz, t_new, D_IDX),
            vn_s.reshape(1, dbsz, t_new, GM_WIDTH))
```

```python
import functools

import numpy as np
import jax
import jax.numpy as jnp
from jax import lax
from jax.experimental import pallas as pl
from jax.experimental.pallas import tpu as pltpu

F32 = jnp.float32
BF16 = jnp.bfloat16
I32 = jnp.int32

HEAD_DIM = 64
N_HEADS = 8
N_KV_HEADS = 4
N_IDX_HEADS = 8
D_IDX = 64
TOPK_MAX = 256
GM_GROUPS = 8
CHUNK = 128
PAGE_SIZE = 128
ROPE_THETA = 500000.0
ROPE_ROT = HEAD_DIM // 4
ROPE_HALF = ROPE_ROT // 2
EPS = 1e-6

ATT_WIDTH = N_HEADS * HEAD_DIM
KV_WIDTH = N_KV_HEADS * HEAD_DIM
IDX_WIDTH = N_IDX_HEADS * D_IDX
GM_WIDTH = GM_GROUPS * 64
LANES = 128
SUBLANES = 8
BF16_SUBLANES = 16
KEY_TILE = 256
Q_BLOCK = 128
ROW_TILE = 256
SAMPLE_CHUNK_PAGES = 8
VMEM_LIMIT = 56 * 1024 * 1024

C_Q, C_K, C_V, C_QI, C_GA, C_U, C_VM, C_GM, C_KIWI, C_END = (
    0, 512, 768, 1024, 1536, 2048, 2560, 3072, 3584, 3712)
_ORIG_KI, _ORIG_GA, _ORIG_END = 1536, 1608, 3656
LOG2_E = 1.4426950408889634
MAX_UNSHIFTED_LOG2 = 60.0
BOUND_MARGIN = 1.05
NEG_INF = float("-inf")
INT_MIN = -2 ** 31


def _head_rms(xh, seg, g):
    sq = xh * xh
    hi = sq.astype(BF16)
    lo = (sq - hi.astype(F32)).astype(BF16)
    ms = (jnp.dot(hi, seg, preferred_element_type=F32)
          + jnp.dot(lo, seg, preferred_element_type=F32))
    return xh * lax.rsqrt(ms + EPS) * g


def _rope(xh, cos, sa, sb):
    w = xh.shape[-1]
    return xh * cos + pltpu.roll(xh, w - ROPE_HALF, 1) * sa + pltpu.roll(xh, ROPE_HALF, 1) * sb


def _tile_lanes(t, width):
    reps = width // t.shape[-1]
    return t if reps == 1 else jnp.concatenate([t] * reps, axis=1)


def _project_body(emit_vn, emit_vt, x_ref, ng_ref, w_ref, qg_ref, kg_ref, lng_ref, lnb_ref, mixw_ref,
                  bias_ref, cos_ref, sa_ref, sb_ref, seg_ref, *outs):
    q_o, k_o, v_o, kb_o, qi_o, kiwi_o, kib_o, ga_o, gm_o = outs[:9]
    extra = list(outs[9:])
    vn_o = extra.pop(0) if emit_vn else None
    vt_o = extra.pop(0) if emit_vt else None
    x = x_ref[...]
    h = x * lax.rsqrt(jnp.mean(x * x, axis=-1, keepdims=True) + EPS) * ng_ref[...]
    hb = h.astype(BF16)

    def proj(c0, c1):
        return jnp.dot(hb, w_ref[:, c0:c1], preferred_element_type=F32)

    cos, sa, sb = cos_ref[...], sa_ref[...], sb_ref[...]
    seg = seg_ref[...]

    q = _head_rms(proj(C_Q, C_K), seg, qg_ref[...])
    q = _rope(q, _tile_lanes(cos, ATT_WIDTH), _tile_lanes(sa, ATT_WIDTH), _tile_lanes(sb, ATT_WIDTH))
    q_o[...] = (q * (HEAD_DIM ** -0.5 * LOG2_E)).astype(BF16)

    k = _head_rms(proj(C_K, C_V), seg[:KV_WIDTH, :KV_WIDTH], kg_ref[...])
    k = _rope(k, _tile_lanes(cos, KV_WIDTH), _tile_lanes(sa, KV_WIDTH), _tile_lanes(sb, KV_WIDTH))
    k_o[...] = k
    kb_o[...] = k.astype(BF16)

    v = proj(C_V, C_QI)
    v_o[...] = v
    if emit_vt:
        vt_o[...] = v.T.astype(BF16)

    qi = _rope(proj(C_QI, C_GA), _tile_lanes(cos, IDX_WIDTH), _tile_lanes(sa, IDX_WIDTH),
               _tile_lanes(sb, IDX_WIDTH))
    qi_o[...] = qi.astype(BF16)

    lane = lax.broadcasted_iota(I32, cos.shape, 1)
    is_ki = lane < D_IDX
    wi_scale = jnp.where(lane < D_IDX + N_IDX_HEADS, IDX_WIDTH ** -0.5, 1.0)
    kiwi = _rope(proj(C_KIWI, C_END), jnp.where(is_ki, cos, wi_scale),
                 jnp.where(is_ki, sa, 0.0), jnp.where(is_ki, sb, 0.0))
    kiwi_o[...] = kiwi
    kib_o[...] = kiwi.astype(BF16)

    ga_o[...] = jax.nn.silu(proj(C_GA, C_U))

    vmg = jax.nn.gelu(proj(C_VM, C_GM))
    xc = vmg - jnp.mean(vmg, axis=-1, keepdims=True)
    vn = xc * lax.rsqrt(jnp.mean(xc * xc, axis=-1, keepdims=True) + EPS) * lng_ref[...] + lnb_ref[...]
    if emit_vn:
        vn_o[...] = vn

    lane_c = lax.broadcasted_iota(I32, (CHUNK, LANES), 1)
    rows = x.shape[0]
    chunks = []
    for c in range(rows // CHUNK):
        pieces = []
        for p in range(GM_GROUPS // 2):
            t = vn[c * CHUNK:(c + 1) * CHUNK, p * LANES:(p + 1) * LANES]
            rhs = jnp.concatenate([jnp.where(lane_c < 64, t, 0.0), jnp.where(lane_c >= 64, t, 0.0)],
                                  axis=0).astype(BF16)
            pieces.append(jnp.dot(mixw_ref[p], rhs, preferred_element_type=F32))
        chunks.append(jnp.concatenate(pieces, axis=1) + bias_ref[...])
    s = jnp.concatenate(chunks, axis=0)
    gm_o[...] = jax.nn.gelu(proj(C_U, C_VM)) * s * jax.nn.silu(proj(C_GM, C_KIWI))


def _project(x2, pos_tables, mixw, bias, weights, emit_vn, emit_vt):
    rows = x2.shape[0]
    tm = min(ROW_TILE, rows)
    assert rows % tm == 0 and tm % CHUNK == 0
    ng, w, qg, kg, lng, lnb, seg = weights
    cos, sa, sb = pos_tables
    assert cos.shape[0] % tm == 0
    n_pos_blocks = cos.shape[0] // tm

    def row_map(i):
        return (i, 0)

    def pos_map(i):
        return (i % n_pos_blocks, 0)

    def const(i):
        return (0, 0)

    def rspec(width):
        return pl.BlockSpec((tm, width), row_map)

    def cspec(a):
        if a.ndim == 3:
            return pl.BlockSpec(a.shape, lambda i: (0, 0, 0))
        return pl.BlockSpec(a.shape, const)

    out_shapes = [
        jax.ShapeDtypeStruct((rows, ATT_WIDTH), BF16),
        jax.ShapeDtypeStruct((rows, KV_WIDTH), F32),
        jax.ShapeDtypeStruct((rows, KV_WIDTH), F32),
        jax.ShapeDtypeStruct((rows, KV_WIDTH), BF16),
        jax.ShapeDtypeStruct((rows, IDX_WIDTH), BF16),
        jax.ShapeDtypeStruct((rows, LANES), F32),
        jax.ShapeDtypeStruct((rows, LANES), BF16),
        jax.ShapeDtypeStruct((rows, ATT_WIDTH), F32),
        jax.ShapeDtypeStruct((rows, GM_WIDTH), F32),
    ]
    if emit_vn:
        out_shapes.append(jax.ShapeDtypeStruct((rows, GM_WIDTH), F32))
    out_specs = [rspec(s.shape[1]) for s in out_shapes]
    if emit_vt:
        assert tm == KEY_TILE
        out_shapes.append(jax.ShapeDtypeStruct((rows // tm, KV_WIDTH, tm), BF16))
        out_specs.append(pl.BlockSpec((None, KV_WIDTH, tm), lambda i: (i, 0, 0)))
    in_specs = [rspec(x2.shape[1]), cspec(ng), cspec(w), cspec(qg), cspec(kg), cspec(lng), cspec(lnb),
                cspec(mixw), cspec(bias),
                pl.BlockSpec((tm, LANES), pos_map), pl.BlockSpec((tm, LANES), pos_map),
                pl.BlockSpec((tm, LANES), pos_map), cspec(seg)]
    return pl.pallas_call(
        functools.partial(_project_body, emit_vn, emit_vt),
        grid=(rows // tm,),
        in_specs=in_specs,
        out_specs=out_specs,
        out_shape=out_shapes,
        compiler_params=pltpu.CompilerParams(dimension_semantics=("arbitrary",),
                                             vmem_limit_bytes=VMEM_LIMIT),
        name="project",
    )(x2, ng, w, qg, kg, lng, lnb, mixw, bias, cos, sa, sb, seg)


def _key_to_float(u):
    bits = jnp.where(u < 0, u ^ I32(INT_MIN), ~u)
    return lax.bitcast_convert_type(bits, F32)


def _count(sc_ref, n_tiles, key_axis, pred, unroll=1):
    a, b = sc_ref.shape[1], sc_ref.shape[2]

    def fold(ind):
        if key_axis == 1:
            return ind
        parts = [ind[i * SUBLANES:(i + 1) * SUBLANES] for i in range(a // SUBLANES)]
        while len(parts) > 1:
            parts = [parts[i] + parts[i + 1] for i in range(0, len(parts), 2)]
        return parts[0]

    def body(i, c):
        for u in range(unroll):
            j = i * unroll + u
            c = c + fold(jnp.where(pred(sc_ref[j], j), 1.0, 0.0))
        return c

    init = jnp.zeros((a, b) if key_axis == 1 else (SUBLANES, b), F32)
    c = lax.fori_loop(0, n_tiles // unroll, body, init)
    return jnp.sum(c, axis=key_axis, keepdims=True)


def _select_threshold(sc_ref, n_tiles, k_row, key_axis, unroll=1):
    k_f = k_row.astype(F32)

    def bit_step(i, carry):
        res, cnt_res = carry
        cand = res | lax.shift_left(I32(1), I32(31) - i)
        t = _key_to_float(cand)
        cnt = _count(sc_ref, n_tiles, key_axis, lambda x, j: x >= t, unroll)
        ok = cnt >= k_f
        return jnp.where(ok, cand, res), jnp.where(ok, cnt, cnt_res)

    res, cnt_res = lax.fori_loop(0, 32, bit_step,
                                 (jnp.zeros(k_row.shape, I32), jnp.zeros(k_row.shape, F32)))
    thr = _key_to_float(res)
    _drop_surplus_ties(sc_ref, n_tiles, thr, k_f, cnt_res, key_axis)
    return thr


def _drop_surplus_ties(sc_ref, n_tiles, thr, k_f, cnt_ge, key_axis):
    t = sc_ref.shape[1 + key_axis]
    surplus = jnp.max(cnt_ge - k_f)

    @pl.when(surplus > 0.0)
    def _():
        n_take = k_f - _count(sc_ref, n_tiles, key_axis, lambda x, j: x > thr)
        r = lax.broadcasted_iota(I32, (t, t), 0)
        c = lax.broadcasted_iota(I32, (t, t), 1)
        tri = jnp.where((r >= c) if key_axis == 0 else (r <= c), 1.0, 0.0).astype(BF16)

        def body(j, before):
            x = sc_ref[j]
            tied = jnp.where(x == thr, 1.0, 0.0)
            if key_axis == 0:
                upto = jnp.dot(tri, tied.astype(BF16), preferred_element_type=F32)
                total = upto[t - 1:t]
            else:
                upto = jnp.dot(tied.astype(BF16), tri, preferred_element_type=F32)
                total = upto[:, t - 1:t]
            rank = tied * (before + upto)
            sc_ref[j] = jnp.where(rank > n_take, NEG_INF, x)
            return before + total

        lax.fori_loop(0, n_tiles, body, jnp.zeros(thr.shape, F32))


def _attend_init(m_ref, l_ref, acc_ref):
    m_ref[...] = jnp.full(m_ref.shape, NEG_INF, F32)
    l_ref[...] = jnp.zeros(l_ref.shape, F32)
    acc_ref[...] = jnp.zeros(acc_ref.shape, F32)


def _softmax_step(s, m_prev, l_prev):
    m_new = jnp.maximum(m_prev, jnp.max(s, axis=1, keepdims=True))
    m_safe = jnp.where(m_new == NEG_INF, 0.0, m_new)
    alpha = jnp.exp2(m_prev - m_safe)
    p = jnp.exp2(s - _tile_lanes(m_safe, s.shape[1]))
    l_new = alpha * l_prev + jnp.sum(p, axis=1, keepdims=True)
    return p, m_new, l_new, alpha


_NT = (((1,), (1,)), ((), ()))


def _two_stage(n_tiles, produce, consume, carry, buf_a, buf_b):
    produce(0, buf_a)

    def pair(i, c):
        j0 = 2 * i
        produce(j0 + 1, buf_b)
        c = consume(j0, buf_a, c)
        produce(jnp.minimum(j0 + 2, n_tiles - 1), buf_a)
        c = consume(j0 + 1, buf_b, c)
        return c

    return lax.fori_loop(0, n_tiles // 2, pair, carry)


def _prompt_attend_body(k_top, bound_ref, qbdt_ref, qiht_ref, kiwi_ref, kb_ref, vt_ref,
                        kib_ref, o_ref, sc_ref, acc_ref, buf_a, buf_b):
    qb = pl.program_id(1)
    n_tiles = 2 * ((qb * Q_BLOCK + Q_BLOCK + 2 * KEY_TILE - 1) // (2 * KEY_TILE))
    qpos = qb * Q_BLOCK + lax.broadcasted_iota(I32, (1, Q_BLOCK), 1)

    w_t = kiwi_ref[...].T[D_IDX:D_IDX + N_IDX_HEADS, :]
    w_pairs = [jnp.concatenate([w_t[2 * p:2 * p + 1], w_t[2 * p + 1:2 * p + 2]], axis=1)
               for p in range(N_IDX_HEADS // 2)]
    krow = lax.broadcasted_iota(I32, (KEY_TILE, Q_BLOCK), 0)

    def score_dots(j, buf):
        start = pl.multiple_of(j * KEY_TILE, KEY_TILE)
        ki_t = kib_ref[pl.ds(start, KEY_TILE), :]
        for p in range(N_IDX_HEADS // 2):
            buf[p] = jnp.dot(ki_t, qiht_ref[p], preferred_element_type=F32)

    def score_sum(j, buf, c):
        acc = jnp.zeros((KEY_TILE, Q_BLOCK), F32)
        for p in range(N_IDX_HEADS // 2):
            d = jnp.maximum(buf[p], 0.0) * w_pairs[p]
            acc = acc + d[:, :Q_BLOCK] + d[:, Q_BLOCK:]
        sc_ref[j] = jnp.where(j * KEY_TILE + krow <= qpos, acc, NEG_INF)
        return c

    _two_stage(n_tiles, score_dots, score_sum, 0, buf_a, buf_b)

    k_row = jnp.minimum(k_top, qpos + 1)
    thr = _select_threshold(sc_ref, n_tiles, k_row, key_axis=0, unroll=2)
    thr2 = jnp.concatenate([thr, thr], axis=1)

    acc_ref[...] = jnp.zeros(acc_ref.shape, F32)
    m0 = tuple(jnp.full((1, 2 * Q_BLOCK), NEG_INF, F32) for _ in range(N_KV_HEADS))
    l0 = tuple(jnp.zeros((1, 2 * Q_BLOCK), F32) for _ in range(N_KV_HEADS))

    ones_rows = jnp.ones((BF16_SUBLANES, KEY_TILE), BF16)

    def logits(j, buf):
        start = pl.multiple_of(j * KEY_TILE, KEY_TILE)
        k_t = kb_ref[pl.ds(start, KEY_TILE), :]
        for g in range(N_KV_HEADS):
            buf[g] = jnp.dot(k_t, qbdt_ref[g], preferred_element_type=F32)

    def weighted_values(v_t, g, p):
        lhs = jnp.concatenate([v_t[g * HEAD_DIM:(g + 1) * HEAD_DIM, :], ones_rows], axis=0)
        pv = jnp.dot(lhs, p, preferred_element_type=F32)
        return pv[:HEAD_DIM], pv[HEAD_DIM:HEAD_DIM + 1]

    def softmax_pv(j, buf, carry):
        ms, ls = carry
        v_t = vt_ref[j]
        sc = sc_ref[j]
        keep = jnp.concatenate([sc, sc], axis=1) >= thr2
        new_m, new_l = [], []
        for g in range(N_KV_HEADS):
            s = jnp.where(keep, buf[g], NEG_INF)
            m_new = jnp.maximum(ms[g], jnp.max(s, axis=0, keepdims=True))
            m_safe = jnp.where(m_new == NEG_INF, 0.0, m_new)
            alpha = jnp.exp2(ms[g] - m_safe)
            pv, psum = weighted_values(v_t, g, jnp.exp2(s - m_safe).astype(BF16))
            new_m.append(m_new)
            new_l.append(alpha * ls[g] + psum)
            acc_ref[g] = acc_ref[g] * alpha + pv
        return tuple(new_m), tuple(new_l)

    def exp_pv(j, buf, ls):
        v_t = vt_ref[j]
        sc = sc_ref[j]
        keep = jnp.concatenate([sc, sc], axis=1) >= thr2
        new_l = []
        for g in range(N_KV_HEADS):
            pv, psum = weighted_values(v_t, g, jnp.exp2(jnp.where(keep, buf[g], NEG_INF)).astype(BF16))
            new_l.append(ls[g] + psum)
            acc_ref[g] = acc_ref[g] + pv
        return tuple(new_l)

    def write_out(ls):
        blocks = []
        for g in range(N_KV_HEADS):
            og = acc_ref[g] / ls[g]
            blocks += [og[:, :Q_BLOCK], og[:, Q_BLOCK:]]
        o_ref[...] = jnp.concatenate(blocks, axis=0).T

    bounded = bound_ref[0] < MAX_UNSHIFTED_LOG2

    @pl.when(bounded)
    def _():
        write_out(_two_stage(n_tiles, logits, exp_pv, l0, buf_a, buf_b))

    @pl.when(jnp.logical_not(bounded))
    def _():
        write_out(_two_stage(n_tiles, logits, softmax_pv, (m0, l0), buf_a, buf_b)[1])


def _prompt_attend(logit_bound, qbdt, qiht, kiwi, kb, vt, kib, k_top):
    bsz, nqb = qbdt.shape[:2]
    seq = kb.shape[1]
    resident = dict(pipeline_mode=pl.Buffered(1))
    return pl.pallas_call(
        functools.partial(_prompt_attend_body, k_top),
        grid=(bsz, nqb),
        in_specs=[
            pl.BlockSpec(memory_space=pltpu.SMEM),
            pl.BlockSpec((None, None, N_KV_HEADS, KV_WIDTH, 2 * Q_BLOCK), lambda b, i: (b, i, 0, 0, 0)),
            pl.BlockSpec((None, None, N_IDX_HEADS // 2, LANES, 2 * Q_BLOCK), lambda b, i: (b, i, 0, 0, 0)),
            pl.BlockSpec((Q_BLOCK, LANES), lambda b, i: (b * nqb + i, 0)),
            pl.BlockSpec((None, seq, KV_WIDTH), lambda b, i: (b, 0, 0), **resident),
            pl.BlockSpec((None, seq // KEY_TILE, KV_WIDTH, KEY_TILE), lambda b, i: (b, 0, 0, 0), **resident),
            pl.BlockSpec((None, seq, LANES), lambda b, i: (b, 0, 0), **resident),
        ],
        out_specs=pl.BlockSpec((Q_BLOCK, ATT_WIDTH), lambda b, i: (b * nqb + i, 0)),
        out_shape=jax.ShapeDtypeStruct((bsz * seq, ATT_WIDTH), F32),
        scratch_shapes=[
            pltpu.VMEM((seq // KEY_TILE, KEY_TILE, Q_BLOCK), F32),
            pltpu.VMEM((N_KV_HEADS, HEAD_DIM, 2 * Q_BLOCK), F32),
            pltpu.VMEM((N_KV_HEADS, KEY_TILE, 2 * Q_BLOCK), F32),
            pltpu.VMEM((N_KV_HEADS, KEY_TILE, 2 * Q_BLOCK), F32),
        ],
        compiler_params=pltpu.CompilerParams(dimension_semantics=("arbitrary", "arbitrary"),
                                             vmem_limit_bytes=VMEM_LIMIT),
        name="prompt_attend",
    )(logit_bound, qbdt, qiht, kiwi, kb, vt, kib)


def _sample_attend_body(k_top, n_pages, t_new,
                        pt_ref, qbd_ref, qih_ref, w_ref, knew_ref, vnew_ref, kinew_ref,
                        cik_hbm, ck_hbm, cv_hbm, o_ref,
                        ibuf, kbuf, vbuf, sem, sc_ref, m_ref, l_ref, acc_ref):
    b = pl.program_id(0)
    nb = pl.num_programs(0)
    slot = b % 2
    cp = SAMPLE_CHUNK_PAGES
    n_chunks = n_pages // cp
    width = cp * PAGE_SIZE

    def page_copies(bb, s, p):
        page = pt_ref[bb, p]
        return (pltpu.make_async_copy(cik_hbm.at[page], ibuf.at[s, p], sem.at[s, 0]),
                pltpu.make_async_copy(ck_hbm.at[page], kbuf.at[s, p], sem.at[s, 1]),
                pltpu.make_async_copy(cv_hbm.at[page], vbuf.at[s, p], sem.at[s, 2]))

    def start_fetch(bb, s):
        def body(p, _):
            for c in page_copies(bb, s, p):
                c.start()
            return 0
        lax.fori_loop(0, n_pages, body, 0)

    def wait_fetch(bb, s):
        def body(p, _):
            for c in page_copies(bb, s, p):
                c.wait()
            return 0
        lax.fori_loop(0, n_pages, body, 0)

    @pl.when(b == 0)
    def _():
        start_fetch(b, slot)

    @pl.when(b + 1 < nb)
    def _():
        start_fetch(b + 1, 1 - slot)

    wait_fetch(b, slot)

    def chunk(buf, c):
        return jnp.concatenate([buf[slot, c * cp + i] for i in range(cp)], axis=1).astype(BF16)

    def new_cols(a):
        padded = jnp.concatenate([a, jnp.zeros((LANES - a.shape[0], a.shape[1]), a.dtype)], axis=0)
        return padded.T.astype(BF16)

    qih = qih_ref[...]
    w = _tile_lanes(w_ref[...], width)

    def head_sum(ki_c):
        d = jnp.dot(qih, ki_c, preferred_element_type=F32)
        d = w[:, :ki_c.shape[1]] * jnp.maximum(d, 0.0)
        acc = d[0:t_new]
        for h in range(1, N_IDX_HEADS):
            acc = acc + d[h * t_new:(h + 1) * t_new]
        return acc

    def score_chunk(c, _):
        sc_ref[c] = head_sum(chunk(ibuf, c))
        return 0

    lax.fori_loop(0, n_chunks, score_chunk, 0)

    tq = lax.broadcasted_iota(I32, (t_new, LANES), 0)
    col = lax.broadcasted_iota(I32, (t_new, LANES), 1)
    sc_new = jnp.where((col < t_new) & (col <= tq), head_sum(new_cols(kinew_ref[...])[:D_IDX]), NEG_INF)
    sc_ref[n_chunks] = jnp.concatenate([sc_new, jnp.full((t_new, width - LANES), NEG_INF, F32)], axis=1)

    qpos = n_pages * PAGE_SIZE + lax.broadcasted_iota(I32, (t_new, 1), 0)
    k_row = jnp.minimum(k_top, qpos + 1)
    thr = _select_threshold(sc_ref, n_chunks + 1, k_row, key_axis=1, unroll=n_chunks + 1)

    _attend_init(m_ref, l_ref, acc_ref)
    qbd = qbd_ref[...]

    def attend(sc, k_c, v_c):
        bias = jnp.where(sc >= thr, 0.0, NEG_INF)
        s = jnp.dot(qbd, k_c, preferred_element_type=F32) + jnp.concatenate([bias] * N_HEADS, axis=0)
        p, m_new, l_new, alpha = _softmax_step(s, m_ref[...], l_ref[...])
        m_ref[...] = m_new
        l_ref[...] = l_new
        acc_ref[...] = (acc_ref[...] * _tile_lanes(alpha, KV_WIDTH)
                        + lax.dot_general(p.astype(BF16), v_c, _NT, preferred_element_type=F32))

    def attend_chunk(c, _):
        attend(sc_ref[c], chunk(kbuf, c), chunk(vbuf, c))
        return 0

    lax.fori_loop(0, n_chunks, attend_chunk, 0)
    attend(sc_ref[n_chunks][:, :LANES], new_cols(knew_ref[...]), new_cols(vnew_ref[...]))

    o_ref[...] = acc_ref[...] / _tile_lanes(l_ref[...], KV_WIDTH)


def _sample_attend(page_table, qbd, qih, w_rows, knew, vnew, kinew, cik_t, ck_t, cv_t, k_top):
    dbsz, n_pages = page_table.shape
    t_new = knew.shape[1]
    rows = N_HEADS * t_new
    past = n_pages * PAGE_SIZE
    assert n_pages % SAMPLE_CHUNK_PAGES == 0
    width = SAMPLE_CHUNK_PAGES * PAGE_SIZE
    n_chunks = n_pages // SAMPLE_CHUNK_PAGES

    def bmap(b, pt):
        return (b, 0, 0)

    grid_spec = pltpu.PrefetchScalarGridSpec(
        num_scalar_prefetch=1,
        grid=(dbsz,),
        in_specs=[
            pl.BlockSpec((None, rows, KV_WIDTH), bmap),
            pl.BlockSpec((None, rows, D_IDX), bmap),
            pl.BlockSpec((None, rows, LANES), bmap),
            pl.BlockSpec((None, t_new, KV_WIDTH), bmap),
            pl.BlockSpec((None, t_new, KV_WIDTH), bmap),
            pl.BlockSpec((None, t_new, LANES), bmap),
            pl.BlockSpec(memory_space=pl.ANY),
            pl.BlockSpec(memory_space=pl.ANY),
            pl.BlockSpec(memory_space=pl.ANY),
        ],
        out_specs=pl.BlockSpec((None, rows, KV_WIDTH), bmap),
        scratch_shapes=[
            pltpu.VMEM((2, n_pages, D_IDX, PAGE_SIZE), F32),
            pltpu.VMEM((2, n_pages, KV_WIDTH, PAGE_SIZE), F32),
            pltpu.VMEM((2, n_pages, KV_WIDTH, PAGE_SIZE), F32),
            pltpu.SemaphoreType.DMA((2, 3)),
            pltpu.VMEM((n_chunks + 1, t_new, width), F32),
            pltpu.VMEM((rows, LANES), F32),
            pltpu.VMEM((rows, LANES), F32),
            pltpu.VMEM((rows, KV_WIDTH), F32),
        ],
    )
    return pl.pallas_call(
        functools.partial(_sample_attend_body, k_top, n_pages, t_new),
        grid_spec=grid_spec,
        out_shape=jax.ShapeDtypeStruct((dbsz, rows, KV_WIDTH), F32),
        compiler_params=pltpu.CompilerParams(dimension_semantics=("arbitrary",),
                                             vmem_limit_bytes=VMEM_LIMIT),
        name="sample_attend",
    )(page_table, qbd, qih, w_rows, knew, vnew, kinew, cik_t, ck_t, cv_t)


def _finish_body(x_ref, o_ref, ga_ref, gm_ref, p_ref, wo_ref, pg_ref, wg_ref, wp_ref, y_ref):
    att = (o_ref[...] * ga_ref[...]).astype(BF16)
    r = (x_ref[...]
         + jnp.dot(att, wo_ref[:ATT_WIDTH, :], preferred_element_type=F32)
         + jnp.dot(gm_ref[...].astype(BF16), wo_ref[ATT_WIDTH:, :], preferred_element_type=F32))
    rn = r * lax.rsqrt(jnp.mean(r * r, axis=-1, keepdims=True) + EPS) * pg_ref[...]
    gate = jax.nn.sigmoid(jnp.dot(rn.astype(BF16), wg_ref[...], preferred_element_type=F32))
    y_ref[...] = r + gate * jnp.dot(p_ref[...].astype(BF16), wp_ref[...], preferred_element_type=F32)


def _finish(x2, o, ga, gm, p2, wo, pg, wg, wp):
    rows, d_model = x2.shape
    tm = min(ROW_TILE, rows)

    def rspec(a):
        return pl.BlockSpec((tm, a.shape[1]), lambda i: (i, 0))

    def cspec(a):
        return pl.BlockSpec(a.shape, lambda i: (0, 0))

    return pl.pallas_call(
        _finish_body,
        grid=(rows // tm,),
        in_specs=[rspec(x2), rspec(o), rspec(ga), rspec(gm), rspec(p2),
                  cspec(wo), cspec(pg), cspec(wg), cspec(wp)],
        out_specs=pl.BlockSpec((tm, d_model), lambda i: (i, 0)),
        out_shape=jax.ShapeDtypeStruct((rows, d_model), F32),
        compiler_params=pltpu.CompilerParams(dimension_semantics=("arbitrary",),
                                             vmem_limit_bytes=VMEM_LIMIT),
        name="finish",
    )(x2, o, ga, gm, p2, wo, pg, wg, wp)


def _rope_tables(pos):
    inv = jnp.power(ROPE_THETA, -jnp.arange(ROPE_HALF, dtype=F32) * 2.0 / ROPE_ROT)
    ang = pos.astype(F32)[:, None] * inv[None, :]
    cos, sin = jnp.cos(ang), jnp.sin(ang)
    n = pos.shape[0]
    one = jnp.ones((n, HEAD_DIM - ROPE_ROT), F32)
    zero8 = jnp.zeros((n, ROPE_HALF), F32)
    zero = jnp.zeros((n, HEAD_DIM - ROPE_ROT), F32)
    cos_h = jnp.concatenate([cos, cos, one], axis=1)
    sa_h = jnp.concatenate([-sin, zero8, zero], axis=1)
    sb_h = jnp.concatenate([zero8, sin, zero], axis=1)
    return tuple(jnp.concatenate([t, t], axis=1) for t in (cos_h, sa_h, sb_h))


def _pair_mix(ws):
    g = ws.shape[0]
    return ws.reshape(g // 2, 2, CHUNK, CHUNK).transpose(0, 2, 1, 3).reshape(g // 2, CHUNK, 2 * CHUNK).astype(BF16)


def _group_diag(q, lead):
    t = q.shape[-2]
    n = len(lead)
    q6 = q.reshape(*lead, t, N_KV_HEADS, 2, HEAD_DIM)
    q6 = jnp.moveaxis(q6, n, n + 2)
    q6 = q6.reshape(*lead, N_KV_HEADS, 2 * t, HEAD_DIM)
    eye = jnp.eye(N_KV_HEADS, dtype=q.dtype)
    qd = q6[..., None, :] * eye[:, None, :, None]
    return qd.reshape(*lead, N_KV_HEADS, 2 * t, KV_WIDTH)


def _heads_to_cols(x):
    b, n, nq, width = x.shape
    x6 = x.reshape(b, n, nq, width // (2 * HEAD_DIM), 2, HEAD_DIM)
    return x6.transpose(0, 1, 3, 5, 4, 2).reshape(b, n, width // (2 * HEAD_DIM), HEAD_DIM, 2 * nq)


def kernel(x_prompt, x_sample, cache_k, cache_v, cache_idx_k, page_table, p_prompt, p_sample,
           norm_in_g, w_in, q_norm_g, k_norm_g, ln_v_g, ln_v_b, w_s, b_s, w_out,
           ple_norm_g, w_ple_gate, w_ple_proj):
    depth = w_in.shape[0]
    assert depth == 1
    bsz, seq, d_model = x_prompt.shape
    dbsz, t_new, _ = x_sample.shape
    n_pages = page_table.shape[1]
    past = n_pages * PAGE_SIZE
    assert seq % KEY_TILE == 0 and CHUNK % t_new == 0 and (dbsz * t_new) % CHUNK == 0

    perm = np.concatenate([np.arange(0, _ORIG_KI), np.arange(_ORIG_GA, _ORIG_END),
                           np.arange(_ORIG_KI, _ORIG_GA)])
    w = jnp.pad(w_in[0][:, perm], ((0, 0), (0, C_END - _ORIG_END))).astype(BF16)
    seg = jnp.asarray(np.kron(np.eye(N_HEADS), np.full((HEAD_DIM, HEAD_DIM), 1.0 / HEAD_DIM)), BF16)
    weights = (norm_in_g[0][None, :], w, jnp.tile(q_norm_g[0], N_HEADS)[None, :],
               jnp.tile(k_norm_g[0], N_KV_HEADS)[None, :], ln_v_g[0][None, :], ln_v_b[0][None, :], seg)
    ws_tril = jnp.where(jnp.tril(jnp.ones((CHUNK, CHUNK), bool))[None], w_s[0], 0.0)
    bias_p = jnp.repeat(b_s[0].T, GM_WIDTH // GM_GROUPS, axis=1)
    reps = CHUNK // t_new
    ws_s = jnp.einsum("ab,gij->gaibj", jnp.eye(reps, dtype=F32),
                      ws_tril[:, :t_new, :t_new]).reshape(GM_GROUPS, CHUNK, CHUNK)
    bias_s = jnp.tile(bias_p[:t_new], (reps, 1))
    wo = w_out[0].astype(BF16)
    wg = w_ple_gate[0].astype(BF16)
    wp = w_ple_proj[0].astype(BF16)
    pg = ple_norm_g[0][None, :]

    xp = x_prompt.reshape(bsz * seq, d_model)
    (q, k, v, kb, qi, kiwi, kib, ga, gm, vt) = _project(
        xp, _rope_tables(jnp.arange(seq)), _pair_mix(ws_tril), bias_p, weights,
        emit_vn=False, emit_vt=True)
    nqb = seq // Q_BLOCK
    k_top_p = min(TOPK_MAX, seq // 4)
    eye = jnp.eye(N_KV_HEADS, dtype=BF16)
    qt = _heads_to_cols(q.reshape(bsz, nqb, Q_BLOCK, ATT_WIDTH))
    qbdt = (qt[:, :, :, None] * eye[:, :, None, None]).reshape(bsz, nqb, N_KV_HEADS, KV_WIDTH, 2 * Q_BLOCK)
    qiht = jnp.pad(_heads_to_cols(qi.reshape(bsz, nqb, Q_BLOCK, IDX_WIDTH)),
                   ((0, 0),) * 3 + ((0, LANES - D_IDX), (0, 0)))
    logit_bound = (HEAD_DIM ** 0.5 * LOG2_E * BOUND_MARGIN
                   * jnp.max(jnp.abs(q_norm_g[0])) * jnp.max(jnp.abs(k_norm_g[0]))).reshape(1)
    o = _prompt_attend(logit_bound, qbdt, qiht, kiwi, kb.reshape(bsz, seq, KV_WIDTH),
                       vt.reshape(bsz, seq // KEY_TILE, KV_WIDTH, KEY_TILE),
                       kib.reshape(bsz, seq, LANES), k_top_p)
    y_prompt = _finish(xp, o, ga, gm, p_prompt[0].reshape(bsz * seq, -1), wo, pg, wg, wp)

    xs = x_sample.reshape(dbsz * t_new, d_model)
    pos_s = past + jnp.arange(min(ROW_TILE, dbsz * t_new)) % t_new
    (q, k_s, v_s, _, qi, kiwi_s, _, ga, gm, vn_s) = _project(
        xs, _rope_tables(pos_s), _pair_mix(ws_s), bias_s, weights, emit_vn=True, emit_vt=False)
    k_top_s = min(TOPK_MAX, (past + t_new) // 4)
    qbd = _group_diag(q.reshape(dbsz, t_new, ATT_WIDTH), (dbsz,)).reshape(dbsz, N_HEADS * t_new, KV_WIDTH)
    qih = qi.reshape(dbsz, t_new, N_IDX_HEADS, D_IDX).transpose(0, 2, 1, 3).reshape(
        dbsz, N_IDX_HEADS * t_new, D_IDX)
    kiwi3 = kiwi_s.reshape(dbsz, t_new, LANES)
    w_rows = kiwi3[:, :, D_IDX:D_IDX + N_IDX_HEADS].transpose(0, 2, 1).reshape(dbsz, N_IDX_HEADS * t_new, 1)
    w_rows = jnp.broadcast_to(w_rows, (dbsz, N_IDX_HEADS * t_new, LANES))
    acc = _sample_attend(page_table, qbd, qih, w_rows,
                         k_s.reshape(dbsz, t_new, KV_WIDTH), v_s.reshape(dbsz, t_new, KV_WIDTH), kiwi3,
                         cache_idx_k[0].transpose(0, 2, 1),
                         cache_k[0].transpose(0, 2, 3, 1).reshape(-1, KV_WIDTH, PAGE_SIZE),
                         cache_v[0].transpose(0, 2, 3, 1).reshape(-1, KV_WIDTH, PAGE_SIZE), k_top_s)
    acc = acc.reshape(dbsz, N_KV_HEADS, 2, t_new, N_KV_HEADS, HEAD_DIM)
    o_s = jnp.stack([acc[:, g, :, :, g, :] for g in range(N_KV_HEADS)], axis=1)
    o_s = o_s.transpose(0, 3, 1, 2, 4).reshape(dbsz * t_new, ATT_WIDTH)
    y_sample = _finish(xs, o_s, ga, gm, p_sample[0].reshape(dbsz * t_new, -1), wo, pg, wg, wp)

    return (y_prompt.reshape(bsz, seq, d_model),
            y_sample.reshape(dbsz, t_new, d_model),
            k.reshape(1, bsz, seq, N_KV_HEADS, HEAD_DIM),
            v.reshape(1, bsz, seq, N_KV_HEADS, HEAD_DIM),
            kiwi[:, :D_IDX].reshape(1, bsz, seq, D_IDX),
            k_s.reshape(1, dbsz, t_new, N_KV_HEADS, HEAD_DIM),
            v_s.reshape(1, dbsz, t_new, N_KV_HEADS, HEAD_DIM),
            kiwi_s[:, :D_IDX].reshape(1, dbsz, t_new, D_IDX),
            vn_s.reshape(1, dbsz, t_new, GM_WIDTH))
```

```python
import functools

import numpy as np
import jax
import jax.numpy as jnp
from jax import lax
from jax.experimental import pallas as pl
from jax.experimental.pallas import tpu as pltpu

F32 = jnp.float32
BF16 = jnp.bfloat16
I32 = jnp.int32

HEAD_DIM = 64
N_HEADS = 8
N_KV_HEADS = 4
N_IDX_HEADS = 8
D_IDX = 64
TOPK_MAX = 256
GM_GROUPS = 8
CHUNK = 128
PAGE_SIZE = 128
ROPE_THETA = 500000.0
ROPE_ROT = HEAD_DIM // 4
ROPE_HALF = ROPE_ROT // 2
EPS = 1e-6

ATT_WIDTH = N_HEADS * HEAD_DIM
KV_WIDTH = N_KV_HEADS * HEAD_DIM
IDX_WIDTH = N_IDX_HEADS * D_IDX
GM_WIDTH = GM_GROUPS * 64
LANES = 128
SUBLANES = 8
BF16_SUBLANES = 16
KEY_TILE = 256
Q_BLOCK = 128
ROW_TILE = 256
SAMPLE_CHUNK_PAGES = 8
VMEM_LIMIT = 56 * 1024 * 1024

C_Q, C_K, C_V, C_QI, C_GA, C_U, C_VM, C_GM, C_KIWI, C_END = (
    0, 512, 768, 1024, 1536, 2048, 2560, 3072, 3584, 3712)
_ORIG_KI, _ORIG_GA, _ORIG_END = 1536, 1608, 3656
LOG2_E = 1.4426950408889634
MAX_UNSHIFTED_LOG2 = 60.0
BOUND_MARGIN = 1.05
NEG_INF = float("-inf")
INT_MIN = -2 ** 31


def _head_rms(xh, seg, g):
    sq = xh * xh
    hi = sq.astype(BF16)
    lo = (sq - hi.astype(F32)).astype(BF16)
    ms = (jnp.dot(hi, seg, preferred_element_type=F32)
          + jnp.dot(lo, seg, preferred_element_type=F32))
    return xh * lax.rsqrt(ms + EPS) * g


def _rope(xh, cos, sa, sb):
    w = xh.shape[-1]
    return xh * cos + pltpu.roll(xh, w - ROPE_HALF, 1) * sa + pltpu.roll(xh, ROPE_HALF, 1) * sb


def _tile_lanes(t, width):
    reps = width // t.shape[-1]
    return t if reps == 1 else jnp.concatenate([t] * reps, axis=1)


def _project_body(emit_vn, emit_vt, x_ref, ng_ref, w_ref, qg_ref, kg_ref, lng_ref, lnb_ref, mixw_ref,
                  bias_ref, cos_ref, sa_ref, sb_ref, seg_ref, *outs):
    q_o, k_o, v_o, kb_o, qi_o, kiwi_o, kib_o, ga_o, gm_o = outs[:9]
    extra = list(outs[9:])
    vn_o = extra.pop(0) if emit_vn else None
    vt_o = extra.pop(0) if emit_vt else None
    x = x_ref[...]
    h = x * lax.rsqrt(jnp.mean(x * x, axis=-1, keepdims=True) + EPS) * ng_ref[...]
    hb = h.astype(BF16)

    def proj(c0, c1):
        return jnp.dot(hb, w_ref[:, c0:c1], preferred_element_type=F32)

    cos, sa, sb = cos_ref[...], sa_ref[...], sb_ref[...]
    seg = seg_ref[...]

    q = _head_rms(proj(C_Q, C_K), seg, qg_ref[...])
    q = _rope(q, _tile_lanes(cos, ATT_WIDTH), _tile_lanes(sa, ATT_WIDTH), _tile_lanes(sb, ATT_WIDTH))
    q_o[...] = (q * (HEAD_DIM ** -0.5 * LOG2_E)).astype(BF16)

    k = _head_rms(proj(C_K, C_V), seg[:KV_WIDTH, :KV_WIDTH], kg_ref[...])
    k = _rope(k, _tile_lanes(cos, KV_WIDTH), _tile_lanes(sa, KV_WIDTH), _tile_lanes(sb, KV_WIDTH))
    k_o[...] = k
    kb_o[...] = k.astype(BF16)

    v = proj(C_V, C_QI)
    v_o[...] = v
    if emit_vt:
        vt_o[...] = v.T.astype(BF16)

    qi = _rope(proj(C_QI, C_GA), _tile_lanes(cos, IDX_WIDTH), _tile_lanes(sa, IDX_WIDTH),
               _tile_lanes(sb, IDX_WIDTH))
    qi_o[...] = qi.astype(BF16)

    lane = lax.broadcasted_iota(I32, cos.shape, 1)
    is_ki = lane < D_IDX
    wi_scale = jnp.where(lane < D_IDX + N_IDX_HEADS, IDX_WIDTH ** -0.5, 1.0)
    kiwi = _rope(proj(C_KIWI, C_END), jnp.where(is_ki, cos, wi_scale),
                 jnp.where(is_ki, sa, 0.0), jnp.where(is_ki, sb, 0.0))
    kiwi_o[...] = kiwi
    kib_o[...] = kiwi.astype(BF16)

    ga_o[...] = jax.nn.silu(proj(C_GA, C_U))

    vmg = jax.nn.gelu(proj(C_VM, C_GM))
    xc = vmg - jnp.mean(vmg, axis=-1, keepdims=True)
    vn = xc * lax.rsqrt(jnp.mean(xc * xc, axis=-1, keepdims=True) + EPS) * lng_ref[...] + lnb_ref[...]
    if emit_vn:
        vn_o[...] = vn

    lane_c = lax.broadcasted_iota(I32, (CHUNK, LANES), 1)
    rows = x.shape[0]
    chunks = []
    for c in range(rows // CHUNK):
        pieces = []
        for p in range(GM_GROUPS // 2):
            t = vn[c * CHUNK:(c + 1) * CHUNK, p * LANES:(p + 1) * LANES]
            rhs = jnp.concatenate([jnp.where(lane_c < 64, t, 0.0), jnp.where(lane_c >= 64, t, 0.0)],
                                  axis=0).astype(BF16)
            pieces.append(jnp.dot(mixw_ref[p], rhs, preferred_element_type=F32))
        chunks.append(jnp.concatenate(pieces, axis=1) + bias_ref[...])
    s = jnp.concatenate(chunks, axis=0)
    gm_o[...] = jax.nn.gelu(proj(C_U, C_VM)) * s * jax.nn.silu(proj(C_GM, C_KIWI))


def _project(x2, pos_tables, mixw, bias, weights, emit_vn, emit_vt):
    rows = x2.shape[0]
    tm = min(ROW_TILE, rows)
    assert rows % tm == 0 and tm % CHUNK == 0
    ng, w, qg, kg, lng, lnb, seg = weights
    cos, sa, sb = pos_tables
    assert cos.shape[0] % tm == 0
    n_pos_blocks = cos.shape[0] // tm

    def row_map(i):
        return (i, 0)

    def pos_map(i):
        return (i % n_pos_blocks, 0)

    def const(i):
        return (0, 0)

    def rspec(width):
        return pl.BlockSpec((tm, width), row_map)

    def cspec(a):
        if a.ndim == 3:
            return pl.BlockSpec(a.shape, lambda i: (0, 0, 0))
        return pl.BlockSpec(a.shape, const)

    out_shapes = [
        jax.ShapeDtypeStruct((rows, ATT_WIDTH), BF16),
        jax.ShapeDtypeStruct((rows, KV_WIDTH), F32),
        jax.ShapeDtypeStruct((rows, KV_WIDTH), F32),
        jax.ShapeDtypeStruct((rows, KV_WIDTH), BF16),
        jax.ShapeDtypeStruct((rows, IDX_WIDTH), BF16),
        jax.ShapeDtypeStruct((rows, LANES), F32),
        jax.ShapeDtypeStruct((rows, LANES), BF16),
        jax.ShapeDtypeStruct((rows, ATT_WIDTH), F32),
        jax.ShapeDtypeStruct((rows, GM_WIDTH), F32),
    ]
    if emit_vn:
        out_shapes.append(jax.ShapeDtypeStruct((rows, GM_WIDTH), F32))
    out_specs = [rspec(s.shape[1]) for s in out_shapes]
    if emit_vt:
        assert tm == KEY_TILE
        out_shapes.append(jax.ShapeDtypeStruct((rows // tm, KV_WIDTH, tm), BF16))
        out_specs.append(pl.BlockSpec((None, KV_WIDTH, tm), lambda i: (i, 0, 0)))
    in_specs = [rspec(x2.shape[1]), cspec(ng), cspec(w), cspec(qg), cspec(kg), cspec(lng), cspec(lnb),
                cspec(mixw), cspec(bias),
                pl.BlockSpec((tm, LANES), pos_map), pl.BlockSpec((tm, LANES), pos_map),
                pl.BlockSpec((tm, LANES), pos_map), cspec(seg)]
    return pl.pallas_call(
        functools.partial(_project_body, emit_vn, emit_vt),
        grid=(rows // tm,),
        in_specs=in_specs,
        out_specs=out_specs,
        out_shape=out_shapes,
        compiler_params=pltpu.CompilerParams(dimension_semantics=("arbitrary",),
                                             vmem_limit_bytes=VMEM_LIMIT),
        name="project",
    )(x2, ng, w, qg, kg, lng, lnb, mixw, bias, cos, sa, sb, seg)


def _key_to_float(u):
    bits = jnp.where(u < 0, u ^ I32(INT_MIN), ~u)
    return lax.bitcast_convert_type(bits, F32)


def _count(sc_ref, n_tiles, key_axis, pred, unroll=1, tail=0):
    a, b = sc_ref.shape[1], sc_ref.shape[2]

    def fold(ind):
        if key_axis == 1:
            return ind
        parts = [ind[i * SUBLANES:(i + 1) * SUBLANES] for i in range(a // SUBLANES)]
        while len(parts) > 1:
            parts = [parts[i] + parts[i + 1] for i in range(0, len(parts), 2)]
        return parts[0]

    def tiles(j0, n, c):
        for u in range(n):
            c = c + fold(jnp.where(pred(sc_ref[j0 + u], j0 + u), 1.0, 0.0))
        return c

    init = jnp.zeros((a, b) if key_axis == 1 else (SUBLANES, b), F32)
    n_steps = n_tiles // unroll
    c = lax.fori_loop(0, n_steps, lambda i, c: tiles(i * unroll, unroll, c), init)
    if tail:
        c = lax.cond(n_tiles - n_steps * unroll == tail, lambda c: tiles(n_tiles - tail, tail, c),
                     lambda c: c, c)
    return jnp.sum(c, axis=key_axis, keepdims=True)


def _select_threshold(sc_ref, n_tiles, k_row, key_axis, unroll=1, tail=0):
    k_f = k_row.astype(F32)

    def bit_step(i, carry):
        res, cnt_res = carry
        cand = res | lax.shift_left(I32(1), I32(31) - i)
        t = _key_to_float(cand)
        cnt = _count(sc_ref, n_tiles, key_axis, lambda x, j: x >= t, unroll, tail)
        ok = cnt >= k_f
        return jnp.where(ok, cand, res), jnp.where(ok, cnt, cnt_res)

    res, cnt_res = lax.fori_loop(0, 32, bit_step,
                                 (jnp.zeros(k_row.shape, I32), jnp.zeros(k_row.shape, F32)))
    thr = _key_to_float(res)
    _drop_surplus_ties(sc_ref, n_tiles, thr, k_f, cnt_res, key_axis)
    return thr


def _drop_surplus_ties(sc_ref, n_tiles, thr, k_f, cnt_ge, key_axis):
    t = sc_ref.shape[1 + key_axis]
    surplus = jnp.max(cnt_ge - k_f)

    @pl.when(surplus > 0.0)
    def _():
        n_take = k_f - _count(sc_ref, n_tiles, key_axis, lambda x, j: x > thr)
        r = lax.broadcasted_iota(I32, (t, t), 0)
        c = lax.broadcasted_iota(I32, (t, t), 1)
        tri = jnp.where((r >= c) if key_axis == 0 else (r <= c), 1.0, 0.0).astype(BF16)

        def body(j, before):
            x = sc_ref[j]
            tied = jnp.where(x == thr, 1.0, 0.0)
            if key_axis == 0:
                upto = jnp.dot(tri, tied.astype(BF16), preferred_element_type=F32)
                total = upto[t - 1:t]
            else:
                upto = jnp.dot(tied.astype(BF16), tri, preferred_element_type=F32)
                total = upto[:, t - 1:t]
            rank = tied * (before + upto)
            sc_ref[j] = jnp.where(rank > n_take, NEG_INF, x)
            return before + total

        lax.fori_loop(0, n_tiles, body, jnp.zeros(thr.shape, F32))


def _attend_init(m_ref, l_ref, acc_ref):
    m_ref[...] = jnp.full(m_ref.shape, NEG_INF, F32)
    l_ref[...] = jnp.zeros(l_ref.shape, F32)
    acc_ref[...] = jnp.zeros(acc_ref.shape, F32)


def _softmax_step(s, m_prev, l_prev):
    m_new = jnp.maximum(m_prev, jnp.max(s, axis=1, keepdims=True))
    m_safe = jnp.where(m_new == NEG_INF, 0.0, m_new)
    alpha = jnp.exp2(m_prev - m_safe)
    p = jnp.exp2(s - _tile_lanes(m_safe, s.shape[1]))
    l_new = alpha * l_prev + jnp.sum(p, axis=1, keepdims=True)
    return p, m_new, l_new, alpha


_NT = (((1,), (1,)), ((), ()))


def _two_stage(n_tiles, produce, consume, carry, buf_a, buf_b):
    produce(0, buf_a)

    def pair(j0, c, last):
        produce(j0 + 1, buf_b)
        c = consume(j0, buf_a, c)
        if not last:
            produce(jnp.minimum(j0 + 2, n_tiles - 1), buf_a)
        return consume(j0 + 1, buf_b, c)

    n_quads = n_tiles // 4
    carry = lax.fori_loop(0, n_quads, lambda i, c: pair(4 * i + 2, pair(4 * i, c, False), False), carry)
    return lax.cond(n_tiles - 4 * n_quads == 2, lambda c: pair(n_tiles - 2, c, True), lambda c: c, carry)


def _prompt_attend_body(k_top, bound_ref, qbdt_ref, qiht_ref, kiwi_ref, kb_ref, vt_ref,
                        kib_ref, o_ref, sc_ref, acc_ref, buf_a, buf_b):
    qb = pl.program_id(1)
    n_tiles = 2 * ((qb * Q_BLOCK + Q_BLOCK + 2 * KEY_TILE - 1) // (2 * KEY_TILE))
    qpos = qb * Q_BLOCK + lax.broadcasted_iota(I32, (1, Q_BLOCK), 1)

    w_t = kiwi_ref[...].T[D_IDX:D_IDX + N_IDX_HEADS, :]
    w_pairs = [jnp.concatenate([w_t[2 * p:2 * p + 1], w_t[2 * p + 1:2 * p + 2]], axis=1)
               for p in range(N_IDX_HEADS // 2)]
    krow = lax.broadcasted_iota(I32, (KEY_TILE, Q_BLOCK), 0)

    def score_dots(j, buf):
        start = pl.multiple_of(j * KEY_TILE, KEY_TILE)
        ki_t = kib_ref[pl.ds(start, KEY_TILE), :]
        for p in range(N_IDX_HEADS // 2):
            buf[p] = jnp.dot(ki_t, qiht_ref[p], preferred_element_type=F32)

    def score_sum(j, buf, c):
        acc = jnp.zeros((KEY_TILE, Q_BLOCK), F32)
        for p in range(N_IDX_HEADS // 2):
            d = jnp.maximum(buf[p], 0.0) * w_pairs[p]
            acc = acc + d[:, :Q_BLOCK] + d[:, Q_BLOCK:]
        sc_ref[j] = jnp.where(j * KEY_TILE + krow <= qpos, acc, NEG_INF)
        return c

    _two_stage(n_tiles, score_dots, score_sum, 0, buf_a, buf_b)

    k_row = jnp.minimum(k_top, qpos + 1)
    thr = _select_threshold(sc_ref, n_tiles, k_row, key_axis=0, unroll=4, tail=2)
    thr2 = jnp.concatenate([thr, thr], axis=1)

    acc_ref[...] = jnp.zeros(acc_ref.shape, F32)
    m0 = tuple(jnp.full((1, 2 * Q_BLOCK), NEG_INF, F32) for _ in range(N_KV_HEADS))
    l0 = tuple(jnp.zeros((1, 2 * Q_BLOCK), F32) for _ in range(N_KV_HEADS))

    ones_rows = jnp.ones((BF16_SUBLANES, KEY_TILE), BF16)

    def logits(j, buf):
        start = pl.multiple_of(j * KEY_TILE, KEY_TILE)
        k_t = kb_ref[pl.ds(start, KEY_TILE), :]
        for g in range(N_KV_HEADS):
            buf[g] = jnp.dot(k_t, qbdt_ref[g], preferred_element_type=F32)

    def weighted_values(v_t, g, p):
        lhs = jnp.concatenate([v_t[g * HEAD_DIM:(g + 1) * HEAD_DIM, :], ones_rows], axis=0)
        pv = jnp.dot(lhs, p, preferred_element_type=F32)
        return pv[:HEAD_DIM], pv[HEAD_DIM:HEAD_DIM + 1]

    def softmax_pv(j, buf, carry):
        ms, ls = carry
        v_t = vt_ref[j]
        sc = sc_ref[j]
        keep = jnp.concatenate([sc, sc], axis=1) >= thr2
        new_m, new_l = [], []
        for g in range(N_KV_HEADS):
            s = jnp.where(keep, buf[g], NEG_INF)
            m_new = jnp.maximum(ms[g], jnp.max(s, axis=0, keepdims=True))
            m_safe = jnp.where(m_new == NEG_INF, 0.0, m_new)
            alpha = jnp.exp2(ms[g] - m_safe)
            pv, psum = weighted_values(v_t, g, jnp.exp2(s - m_safe).astype(BF16))
            new_m.append(m_new)
            new_l.append(alpha * ls[g] + psum)
            acc_ref[g] = acc_ref[g] * alpha + pv
        return tuple(new_m), tuple(new_l)

    def exp_pv(j, buf, ls):
        v_t = vt_ref[j]
        sc = sc_ref[j]
        keep = jnp.concatenate([sc, sc], axis=1) >= thr2
        new_l = []
        for g in range(N_KV_HEADS):
            pv, psum = weighted_values(v_t, g, jnp.exp2(jnp.where(keep, buf[g], NEG_INF)).astype(BF16))
            new_l.append(ls[g] + psum)
            acc_ref[g] = acc_ref[g] + pv
        return tuple(new_l)

    def write_out(ls):
        blocks = []
        for g in range(N_KV_HEADS):
            og = acc_ref[g] / ls[g]
            blocks += [og[:, :Q_BLOCK], og[:, Q_BLOCK:]]
        o_ref[...] = jnp.concatenate(blocks, axis=0).T

    bounded = bound_ref[0] < MAX_UNSHIFTED_LOG2

    @pl.when(bounded)
    def _():
        write_out(_two_stage(n_tiles, logits, exp_pv, l0, buf_a, buf_b))

    @pl.when(jnp.logical_not(bounded))
    def _():
        write_out(_two_stage(n_tiles, logits, softmax_pv, (m0, l0), buf_a, buf_b)[1])


def _prompt_attend(logit_bound, qbdt, qiht, kiwi, kb, vt, kib, k_top):
    bsz, nqb = qbdt.shape[:2]
    seq = kb.shape[1]
    resident = dict(pipeline_mode=pl.Buffered(1))
    return pl.pallas_call(
        functools.partial(_prompt_attend_body, k_top),
        grid=(bsz, nqb),
        in_specs=[
            pl.BlockSpec(memory_space=pltpu.SMEM),
            pl.BlockSpec((None, None, N_KV_HEADS, KV_WIDTH, 2 * Q_BLOCK), lambda b, i: (b, i, 0, 0, 0)),
            pl.BlockSpec((None, None, N_IDX_HEADS // 2, LANES, 2 * Q_BLOCK), lambda b, i: (b, i, 0, 0, 0)),
            pl.BlockSpec((Q_BLOCK, LANES), lambda b, i: (b * nqb + i, 0)),
            pl.BlockSpec((None, seq, KV_WIDTH), lambda b, i: (b, 0, 0), **resident),
            pl.BlockSpec((None, seq // KEY_TILE, KV_WIDTH, KEY_TILE), lambda b, i: (b, 0, 0, 0), **resident),
            pl.BlockSpec((None, seq, LANES), lambda b, i: (b, 0, 0), **resident),
        ],
        out_specs=pl.BlockSpec((Q_BLOCK, ATT_WIDTH), lambda b, i: (b * nqb + i, 0)),
        out_shape=jax.ShapeDtypeStruct((bsz * seq, ATT_WIDTH), F32),
        scratch_shapes=[
            pltpu.VMEM((seq // KEY_TILE, KEY_TILE, Q_BLOCK), F32),
            pltpu.VMEM((N_KV_HEADS, HEAD_DIM, 2 * Q_BLOCK), F32),
            pltpu.VMEM((N_KV_HEADS, KEY_TILE, 2 * Q_BLOCK), F32),
            pltpu.VMEM((N_KV_HEADS, KEY_TILE, 2 * Q_BLOCK), F32),
        ],
        compiler_params=pltpu.CompilerParams(dimension_semantics=("arbitrary", "arbitrary"),
                                             vmem_limit_bytes=VMEM_LIMIT),
        name="prompt_attend",
    )(logit_bound, qbdt, qiht, kiwi, kb, vt, kib)


def _sample_attend_body(k_top, n_pages, t_new,
                        pt_ref, qbd_ref, qih_ref, w_ref, knew_ref, vnew_ref, kinew_ref,
                        cik_hbm, ck_hbm, cv_hbm, o_ref,
                        ibuf, kbuf, vbuf, sem, sc_ref, m_ref, l_ref, acc_ref):
    b = pl.program_id(0)
    nb = pl.num_programs(0)
    slot = b % 2
    cp = SAMPLE_CHUNK_PAGES
    n_chunks = n_pages // cp
    width = cp * PAGE_SIZE

    def page_copies(bb, s, p):
        page = pt_ref[bb, p]
        return (pltpu.make_async_copy(cik_hbm.at[page], ibuf.at[s, p], sem.at[s, 0]),
                pltpu.make_async_copy(ck_hbm.at[page], kbuf.at[s, p], sem.at[s, 1]),
                pltpu.make_async_copy(cv_hbm.at[page], vbuf.at[s, p], sem.at[s, 2]))

    def start_fetch(bb, s):
        def body(p, _):
            for c in page_copies(bb, s, p):
                c.start()
            return 0
        lax.fori_loop(0, n_pages, body, 0)

    def wait_fetch(bb, s):
        def body(p, _):
            for c in page_copies(bb, s, p):
                c.wait()
            return 0
        lax.fori_loop(0, n_pages, body, 0)

    @pl.when(b == 0)
    def _():
        start_fetch(b, slot)

    @pl.when(b + 1 < nb)
    def _():
        start_fetch(b + 1, 1 - slot)

    wait_fetch(b, slot)

    def chunk(buf, c):
        return jnp.concatenate([buf[slot, c * cp + i] for i in range(cp)], axis=1).astype(BF16)

    def new_cols(a):
        padded = jnp.concatenate([a, jnp.zeros((LANES - a.shape[0], a.shape[1]), a.dtype)], axis=0)
        return padded.T.astype(BF16)

    qih = qih_ref[...]
    w = _tile_lanes(w_ref[...], width)

    def head_sum(ki_c):
        d = jnp.dot(qih, ki_c, preferred_element_type=F32)
        d = w[:, :ki_c.shape[1]] * jnp.maximum(d, 0.0)
        acc = d[0:t_new]
        for h in range(1, N_IDX_HEADS):
            acc = acc + d[h * t_new:(h + 1) * t_new]
        return acc

    def score_chunk(c, _):
        sc_ref[c] = head_sum(chunk(ibuf, c))
        return 0

    lax.fori_loop(0, n_chunks, score_chunk, 0)

    tq = lax.broadcasted_iota(I32, (t_new, LANES), 0)
    col = lax.broadcasted_iota(I32, (t_new, LANES), 1)
    sc_new = jnp.where((col < t_new) & (col <= tq), head_sum(new_cols(kinew_ref[...])[:D_IDX]), NEG_INF)
    sc_ref[n_chunks] = jnp.concatenate([sc_new, jnp.full((t_new, width - LANES), NEG_INF, F32)], axis=1)

    qpos = n_pages * PAGE_SIZE + lax.broadcasted_iota(I32, (t_new, 1), 0)
    k_row = jnp.minimum(k_top, qpos + 1)
    thr = _select_threshold(sc_ref, n_chunks + 1, k_row, key_axis=1, unroll=n_chunks + 1)

    _attend_init(m_ref, l_ref, acc_ref)
    qbd = qbd_ref[...]

    def attend(sc, k_c, v_c):
        bias = jnp.where(sc >= thr, 0.0, NEG_INF)
        s = jnp.dot(qbd, k_c, preferred_element_type=F32) + jnp.concatenate([bias] * N_HEADS, axis=0)
        p, m_new, l_new, alpha = _softmax_step(s, m_ref[...], l_ref[...])
        m_ref[...] = m_new
        l_ref[...] = l_new
        acc_ref[...] = (acc_ref[...] * _tile_lanes(alpha, KV_WIDTH)
                        + lax.dot_general(p.astype(BF16), v_c, _NT, preferred_element_type=F32))

    def attend_chunk(c, _):
        attend(sc_ref[c], chunk(kbuf, c), chunk(vbuf, c))
        return 0

    lax.fori_loop(0, n_chunks, attend_chunk, 0)
    attend(sc_ref[n_chunks][:, :LANES], new_cols(knew_ref[...]), new_cols(vnew_ref[...]))

    o_ref[...] = acc_ref[...] / _tile_lanes(l_ref[...], KV_WIDTH)


def _sample_attend(page_table, qbd, qih, w_rows, knew, vnew, kinew, cik_t, ck_t, cv_t, k_top):
    dbsz, n_pages = page_table.shape
    t_new = knew.shape[1]
    rows = N_HEADS * t_new
    past = n_pages * PAGE_SIZE
    assert n_pages % SAMPLE_CHUNK_PAGES == 0
    width = SAMPLE_CHUNK_PAGES * PAGE_SIZE
    n_chunks = n_pages // SAMPLE_CHUNK_PAGES

    def bmap(b, pt):
        return (b, 0, 0)

    grid_spec = pltpu.PrefetchScalarGridSpec(
        num_scalar_prefetch=1,
        grid=(dbsz,),
        in_specs=[
            pl.BlockSpec((None, rows, KV_WIDTH), bmap),
            pl.BlockSpec((None, rows, D_IDX), bmap),
            pl.BlockSpec((None, rows, LANES), bmap),
            pl.BlockSpec((None, t_new, KV_WIDTH), bmap),
            pl.BlockSpec((None, t_new, KV_WIDTH), bmap),
            pl.BlockSpec((None, t_new, LANES), bmap),
            pl.BlockSpec(memory_space=pl.ANY),
            pl.BlockSpec(memory_space=pl.ANY),
            pl.BlockSpec(memory_space=pl.ANY),
        ],
        out_specs=pl.BlockSpec((None, rows, KV_WIDTH), bmap),
        scratch_shapes=[
            pltpu.VMEM((2, n_pages, D_IDX, PAGE_SIZE), F32),
            pltpu.VMEM((2, n_pages, KV_WIDTH, PAGE_SIZE), F32),
            pltpu.VMEM((2, n_pages, KV_WIDTH, PAGE_SIZE), F32),
            pltpu.SemaphoreType.DMA((2, 3)),
            pltpu.VMEM((n_chunks + 1, t_new, width), F32),
            pltpu.VMEM((rows, LANES), F32),
            pltpu.VMEM((rows, LANES), F32),
            pltpu.VMEM((rows, KV_WIDTH), F32),
        ],
    )
    return pl.pallas_call(
        functools.partial(_sample_attend_body, k_top, n_pages, t_new),
        grid_spec=grid_spec,
        out_shape=jax.ShapeDtypeStruct((dbsz, rows, KV_WIDTH), F32),
        compiler_params=pltpu.CompilerParams(dimension_semantics=("arbitrary",),
                                             vmem_limit_bytes=VMEM_LIMIT),
        name="sample_attend",
    )(page_table, qbd, qih, w_rows, knew, vnew, kinew, cik_t, ck_t, cv_t)


def _finish_body(x_ref, o_ref, ga_ref, gm_ref, p_ref, wo_ref, pg_ref, wg_ref, wp_ref, y_ref):
    att = (o_ref[...] * ga_ref[...]).astype(BF16)
    r = (x_ref[...]
         + jnp.dot(att, wo_ref[:ATT_WIDTH, :], preferred_element_type=F32)
         + jnp.dot(gm_ref[...].astype(BF16), wo_ref[ATT_WIDTH:, :], preferred_element_type=F32))
    rn = r * lax.rsqrt(jnp.mean(r * r, axis=-1, keepdims=True) + EPS) * pg_ref[...]
    gate = jax.nn.sigmoid(jnp.dot(rn.astype(BF16), wg_ref[...], preferred_element_type=F32))
    y_ref[...] = r + gate * jnp.dot(p_ref[...].astype(BF16), wp_ref[...], preferred_element_type=F32)


def _finish(x2, o, ga, gm, p2, wo, pg, wg, wp):
    rows, d_model = x2.shape
    tm = min(ROW_TILE, rows)

    def rspec(a):
        return pl.BlockSpec((tm, a.shape[1]), lambda i: (i, 0))

    def cspec(a):
        return pl.BlockSpec(a.shape, lambda i: (0, 0))

    return pl.pallas_call(
        _finish_body,
        grid=(rows // tm,),
        in_specs=[rspec(x2), rspec(o), rspec(ga), rspec(gm), rspec(p2),
                  cspec(wo), cspec(pg), cspec(wg), cspec(wp)],
        out_specs=pl.BlockSpec((tm, d_model), lambda i: (i, 0)),
        out_shape=jax.ShapeDtypeStruct((rows, d_model), F32),
        compiler_params=pltpu.CompilerParams(dimension_semantics=("arbitrary",),
                                             vmem_limit_bytes=VMEM_LIMIT),
        name="finish",
    )(x2, o, ga, gm, p2, wo, pg, wg, wp)


def _rope_tables(pos):
    inv = jnp.power(ROPE_THETA, -jnp.arange(ROPE_HALF, dtype=F32) * 2.0 / ROPE_ROT)
    ang = pos.astype(F32)[:, None] * inv[None, :]
    cos, sin = jnp.cos(ang), jnp.sin(ang)
    n = pos.shape[0]
    one = jnp.ones((n, HEAD_DIM - ROPE_ROT), F32)
    zero8 = jnp.zeros((n, ROPE_HALF), F32)
    zero = jnp.zeros((n, HEAD_DIM - ROPE_ROT), F32)
    cos_h = jnp.concatenate([cos, cos, one], axis=1)
    sa_h = jnp.concatenate([-sin, zero8, zero], axis=1)
    sb_h = jnp.concatenate([zero8, sin, zero], axis=1)
    return tuple(jnp.concatenate([t, t], axis=1) for t in (cos_h, sa_h, sb_h))


def _pair_mix(ws):
    g = ws.shape[0]
    return ws.reshape(g // 2, 2, CHUNK, CHUNK).transpose(0, 2, 1, 3).reshape(g // 2, CHUNK, 2 * CHUNK).astype(BF16)


def _group_diag(q, lead):
    t = q.shape[-2]
    n = len(lead)
    q6 = q.reshape(*lead, t, N_KV_HEADS, 2, HEAD_DIM)
    q6 = jnp.moveaxis(q6, n, n + 2)
    q6 = q6.reshape(*lead, N_KV_HEADS, 2 * t, HEAD_DIM)
    eye = jnp.eye(N_KV_HEADS, dtype=q.dtype)
    qd = q6[..., None, :] * eye[:, None, :, None]
    return qd.reshape(*lead, N_KV_HEADS, 2 * t, KV_WIDTH)


def _heads_to_cols(x):
    b, n, nq, width = x.shape
    x6 = x.reshape(b, n, nq, width // (2 * HEAD_DIM), 2, HEAD_DIM)
    return x6.transpose(0, 1, 3, 5, 4, 2).reshape(b, n, width // (2 * HEAD_DIM), HEAD_DIM, 2 * nq)


def kernel(x_prompt, x_sample, cache_k, cache_v, cache_idx_k, page_table, p_prompt, p_sample,
           norm_in_g, w_in, q_norm_g, k_norm_g, ln_v_g, ln_v_b, w_s, b_s, w_out,
           ple_norm_g, w_ple_gate, w_ple_proj):
    depth = w_in.shape[0]
    assert depth == 1
    bsz, seq, d_model = x_prompt.shape
    dbsz, t_new, _ = x_sample.shape
    n_pages = page_table.shape[1]
    past = n_pages * PAGE_SIZE
    assert seq % KEY_TILE == 0 and CHUNK % t_new == 0 and (dbsz * t_new) % CHUNK == 0

    perm = np.concatenate([np.arange(0, _ORIG_KI), np.arange(_ORIG_GA, _ORIG_END),
                           np.arange(_ORIG_KI, _ORIG_GA)])
    w = jnp.pad(w_in[0][:, perm], ((0, 0), (0, C_END - _ORIG_END))).astype(BF16)
    seg = jnp.asarray(np.kron(np.eye(N_HEADS), np.full((HEAD_DIM, HEAD_DIM), 1.0 / HEAD_DIM)), BF16)
    weights = (norm_in_g[0][None, :], w, jnp.tile(q_norm_g[0], N_HEADS)[None, :],
               jnp.tile(k_norm_g[0], N_KV_HEADS)[None, :], ln_v_g[0][None, :], ln_v_b[0][None, :], seg)
    ws_tril = jnp.where(jnp.tril(jnp.ones((CHUNK, CHUNK), bool))[None], w_s[0], 0.0)
    bias_p = jnp.repeat(b_s[0].T, GM_WIDTH // GM_GROUPS, axis=1)
    reps = CHUNK // t_new
    ws_s = jnp.einsum("ab,gij->gaibj", jnp.eye(reps, dtype=F32),
                      ws_tril[:, :t_new, :t_new]).reshape(GM_GROUPS, CHUNK, CHUNK)
    bias_s = jnp.tile(bias_p[:t_new], (reps, 1))
    wo = w_out[0].astype(BF16)
    wg = w_ple_gate[0].astype(BF16)
    wp = w_ple_proj[0].astype(BF16)
    pg = ple_norm_g[0][None, :]

    xp = x_prompt.reshape(bsz * seq, d_model)
    (q, k, v, kb, qi, kiwi, kib, ga, gm, vt) = _project(
        xp, _rope_tables(jnp.arange(seq)), _pair_mix(ws_tril), bias_p, weights,
        emit_vn=False, emit_vt=True)
    nqb = seq // Q_BLOCK
    k_top_p = min(TOPK_MAX, seq // 4)
    eye = jnp.eye(N_KV_HEADS, dtype=BF16)
    qt = _heads_to_cols(q.reshape(bsz, nqb, Q_BLOCK, ATT_WIDTH))
    qbdt = (qt[:, :, :, None] * eye[:, :, None, None]).reshape(bsz, nqb, N_KV_HEADS, KV_WIDTH, 2 * Q_BLOCK)
    qiht = jnp.pad(_heads_to_cols(qi.reshape(bsz, nqb, Q_BLOCK, IDX_WIDTH)),
                   ((0, 0),) * 3 + ((0, LANES - D_IDX), (0, 0)))
    logit_bound = (HEAD_DIM ** 0.5 * LOG2_E * BOUND_MARGIN
                   * jnp.max(jnp.abs(q_norm_g[0])) * jnp.max(jnp.abs(k_norm_g[0]))).reshape(1)
    o = _prompt_attend(logit_bound, qbdt, qiht, kiwi, kb.reshape(bsz, seq, KV_WIDTH),
                       vt.reshape(bsz, seq // KEY_TILE, KV_WIDTH, KEY_TILE),
                       kib.reshape(bsz, seq, LANES), k_top_p)
    y_prompt = _finish(xp, o, ga, gm, p_prompt[0].reshape(bsz * seq, -1), wo, pg, wg, wp)

    xs = x_sample.reshape(dbsz * t_new, d_model)
    pos_s = past + jnp.arange(min(ROW_TILE, dbsz * t_new)) % t_new
    (q, k_s, v_s, _, qi, kiwi_s, _, ga, gm, vn_s) = _project(
        xs, _rope_tables(pos_s), _pair_mix(ws_s), bias_s, weights, emit_vn=True, emit_vt=False)
    k_top_s = min(TOPK_MAX, (past + t_new) // 4)
    qbd = _group_diag(q.reshape(dbsz, t_new, ATT_WIDTH), (dbsz,)).reshape(dbsz, N_HEADS * t_new, KV_WIDTH)
    qih = qi.reshape(dbsz, t_new, N_IDX_HEADS, D_IDX).transpose(0, 2, 1, 3).reshape(
        dbsz, N_IDX_HEADS * t_new, D_IDX)
    kiwi3 = kiwi_s.reshape(dbsz, t_new, LANES)
    w_rows = kiwi3[:, :, D_IDX:D_IDX + N_IDX_HEADS].transpose(0, 2, 1).reshape(dbsz, N_IDX_HEADS * t_new, 1)
    w_rows = jnp.broadcast_to(w_rows, (dbsz, N_IDX_HEADS * t_new, LANES))
    acc = _sample_attend(page_table, qbd, qih, w_rows,
                         k_s.reshape(dbsz, t_new, KV_WIDTH), v_s.reshape(dbsz, t_new, KV_WIDTH), kiwi3,
                         cache_idx_k[0].transpose(0, 2, 1),
                         cache_k[0].transpose(0, 2, 3, 1).reshape(-1, KV_WIDTH, PAGE_SIZE),
                         cache_v[0].transpose(0, 2, 3, 1).reshape(-1, KV_WIDTH, PAGE_SIZE), k_top_s)
    acc = acc.reshape(dbsz, N_KV_HEADS, 2, t_new, N_KV_HEADS, HEAD_DIM)
    o_s = jnp.stack([acc[:, g, :, :, g, :] for g in range(N_KV_HEADS)], axis=1)
    o_s = o_s.transpose(0, 3, 1, 2, 4).reshape(dbsz * t_new, ATT_WIDTH)
    y_sample = _finish(xs, o_s, ga, gm, p_sample[0].reshape(dbsz * t_new, -1), wo, pg, wg, wp)

    return (y_prompt.reshape(bsz, seq, d_model),
            y_sample.reshape(dbsz, t_new, d_model),
            k.reshape(1, bsz, seq, N_KV_HEADS, HEAD_DIM),
            v.reshape(1, bsz, seq, N_KV_HEADS, HEAD_DIM),
            kiwi[:, :D_IDX].reshape(1, bsz, seq, D_IDX),
            k_s.reshape(1, dbsz, t_new, N_KV_HEADS, HEAD_DIM),
            v_s.reshape(1, dbsz, t_new, N_KV_HEADS, HEAD_DIM),
            kiwi_s[:, :D_IDX].reshape(1, dbsz, t_new, D_IDX),
            vn_s.reshape(1, dbsz, t_new, GM_WIDTH))
```

```python
import functools

import numpy as np
import jax
import jax.numpy as jnp
from jax import lax
from jax.experimental import pallas as pl
from jax.experimental.pallas import tpu as pltpu

F32 = jnp.float32
BF16 = jnp.bfloat16
I32 = jnp.int32

HEAD_DIM = 64
N_HEADS = 8
N_KV_HEADS = 4
N_IDX_HEADS = 8
D_IDX = 64
TOPK_MAX = 256
GM_GROUPS = 8
CHUNK = 128
PAGE_SIZE = 128
ROPE_THETA = 500000.0
ROPE_ROT = HEAD_DIM // 4
ROPE_HALF = ROPE_ROT // 2
EPS = 1e-6

ATT_WIDTH = N_HEADS * HEAD_DIM
KV_WIDTH = N_KV_HEADS * HEAD_DIM
IDX_WIDTH = N_IDX_HEADS * D_IDX
GM_WIDTH = GM_GROUPS * 64
LANES = 128
SUBLANES = 8
BF16_SUBLANES = 16
KEY_TILE = 256
Q_BLOCK = 128
ROW_TILE = 256
SAMPLE_CHUNK_PAGES = 8
VMEM_LIMIT = 56 * 1024 * 1024

C_Q, C_K, C_V, C_QI, C_GA, C_U, C_VM, C_GM, C_KIWI, C_END = (
    0, 512, 768, 1024, 1536, 2048, 2560, 3072, 3584, 3712)
_ORIG_KI, _ORIG_GA, _ORIG_END = 1536, 1608, 3656
LOG2_E = 1.4426950408889634
MAX_UNSHIFTED_LOG2 = 60.0
BOUND_MARGIN = 1.05
BOUNDED_STEPS = 12
FIRST_SETTLE_CHECK = 20
SETTLE_CHECK_EVERY = 4
NEG_INF = float("-inf")
INT_MIN = -2 ** 31


def _head_rms(xh, seg, g):
    sq = xh * xh
    hi = sq.astype(BF16)
    lo = (sq - hi.astype(F32)).astype(BF16)
    ms = (jnp.dot(hi, seg, preferred_element_type=F32)
          + jnp.dot(lo, seg, preferred_element_type=F32))
    return xh * lax.rsqrt(ms + EPS) * g


def _rope(xh, cos, sa, sb):
    w = xh.shape[-1]
    return xh * cos + pltpu.roll(xh, w - ROPE_HALF, 1) * sa + pltpu.roll(xh, ROPE_HALF, 1) * sb


def _tile_lanes(t, width):
    reps = width // t.shape[-1]
    return t if reps == 1 else jnp.concatenate([t] * reps, axis=1)


def _project_body(emit_vn, emit_vt, x_ref, ng_ref, w_ref, qg_ref, kg_ref, lng_ref, lnb_ref, mixw_ref,
                  bias_ref, cos_ref, sa_ref, sb_ref, seg_ref, *outs):
    q_o, k_o, v_o, kb_o, qi_o, kiwi_o, kib_o, ga_o, gm_o = outs[:9]
    extra = list(outs[9:])
    vn_o = extra.pop(0) if emit_vn else None
    vt_o = extra.pop(0) if emit_vt else None
    x = x_ref[...]
    h = x * lax.rsqrt(jnp.mean(x * x, axis=-1, keepdims=True) + EPS) * ng_ref[...]
    hb = h.astype(BF16)

    def proj(c0, c1):
        return jnp.dot(hb, w_ref[:, c0:c1], preferred_element_type=F32)

    cos, sa, sb = cos_ref[...], sa_ref[...], sb_ref[...]
    seg = seg_ref[...]

    q = _head_rms(proj(C_Q, C_K), seg, qg_ref[...])
    q = _rope(q, _tile_lanes(cos, ATT_WIDTH), _tile_lanes(sa, ATT_WIDTH), _tile_lanes(sb, ATT_WIDTH))
    q_o[...] = (q * (HEAD_DIM ** -0.5 * LOG2_E)).astype(BF16)

    k = _head_rms(proj(C_K, C_V), seg[:KV_WIDTH, :KV_WIDTH], kg_ref[...])
    k = _rope(k, _tile_lanes(cos, KV_WIDTH), _tile_lanes(sa, KV_WIDTH), _tile_lanes(sb, KV_WIDTH))
    k_o[...] = k
    kb_o[...] = k.astype(BF16)

    v = proj(C_V, C_QI)
    v_o[...] = v
    if emit_vt:
        vt_o[...] = v.T.astype(BF16)

    qi = _rope(proj(C_QI, C_GA), _tile_lanes(cos, IDX_WIDTH), _tile_lanes(sa, IDX_WIDTH),
               _tile_lanes(sb, IDX_WIDTH))
    qi_o[...] = qi.astype(BF16)

    lane = lax.broadcasted_iota(I32, cos.shape, 1)
    is_ki = lane < D_IDX
    wi_scale = jnp.where(lane < D_IDX + N_IDX_HEADS, IDX_WIDTH ** -0.5, 1.0)
    kiwi = _rope(proj(C_KIWI, C_END), jnp.where(is_ki, cos, wi_scale),
                 jnp.where(is_ki, sa, 0.0), jnp.where(is_ki, sb, 0.0))
    kiwi_o[...] = kiwi
    kib_o[...] = kiwi.astype(BF16)

    ga_o[...] = jax.nn.silu(proj(C_GA, C_U))

    vmg = jax.nn.gelu(proj(C_VM, C_GM))
    xc = vmg - jnp.mean(vmg, axis=-1, keepdims=True)
    vn = xc * lax.rsqrt(jnp.mean(xc * xc, axis=-1, keepdims=True) + EPS) * lng_ref[...] + lnb_ref[...]
    if emit_vn:
        vn_o[...] = vn

    lane_c = lax.broadcasted_iota(I32, (CHUNK, LANES), 1)
    rows = x.shape[0]
    chunks = []
    for c in range(rows // CHUNK):
        pieces = []
        for p in range(GM_GROUPS // 2):
            t = vn[c * CHUNK:(c + 1) * CHUNK, p * LANES:(p + 1) * LANES]
            rhs = jnp.concatenate([jnp.where(lane_c < 64, t, 0.0), jnp.where(lane_c >= 64, t, 0.0)],
                                  axis=0).astype(BF16)
            pieces.append(jnp.dot(mixw_ref[p], rhs, preferred_element_type=F32))
        chunks.append(jnp.concatenate(pieces, axis=1) + bias_ref[...])
    s = jnp.concatenate(chunks, axis=0)
    gm_o[...] = jax.nn.gelu(proj(C_U, C_VM)) * s * jax.nn.silu(proj(C_GM, C_KIWI))


def _project(x2, pos_tables, mixw, bias, weights, emit_vn, emit_vt):
    rows = x2.shape[0]
    tm = min(ROW_TILE, rows)
    assert rows % tm == 0 and tm % CHUNK == 0
    ng, w, qg, kg, lng, lnb, seg = weights
    cos, sa, sb = pos_tables
    assert cos.shape[0] % tm == 0
    n_pos_blocks = cos.shape[0] // tm

    def row_map(i):
        return (i, 0)

    def pos_map(i):
        return (i % n_pos_blocks, 0)

    def const(i):
        return (0, 0)

    def rspec(width):
        return pl.BlockSpec((tm, width), row_map)

    def cspec(a):
        if a.ndim == 3:
            return pl.BlockSpec(a.shape, lambda i: (0, 0, 0))
        return pl.BlockSpec(a.shape, const)

    out_shapes = [
        jax.ShapeDtypeStruct((rows, ATT_WIDTH), BF16),
        jax.ShapeDtypeStruct((rows, KV_WIDTH), F32),
        jax.ShapeDtypeStruct((rows, KV_WIDTH), F32),
        jax.ShapeDtypeStruct((rows, KV_WIDTH), BF16),
        jax.ShapeDtypeStruct((rows, IDX_WIDTH), BF16),
        jax.ShapeDtypeStruct((rows, LANES), F32),
        jax.ShapeDtypeStruct((rows, LANES), BF16),
        jax.ShapeDtypeStruct((rows, ATT_WIDTH), F32),
        jax.ShapeDtypeStruct((rows, GM_WIDTH), F32),
    ]
    if emit_vn:
        out_shapes.append(jax.ShapeDtypeStruct((rows, GM_WIDTH), F32))
    out_specs = [rspec(s.shape[1]) for s in out_shapes]
    if emit_vt:
        assert tm == KEY_TILE
        out_shapes.append(jax.ShapeDtypeStruct((rows // tm, KV_WIDTH, tm), BF16))
        out_specs.append(pl.BlockSpec((None, KV_WIDTH, tm), lambda i: (i, 0, 0)))
    in_specs = [rspec(x2.shape[1]), cspec(ng), cspec(w), cspec(qg), cspec(kg), cspec(lng), cspec(lnb),
                cspec(mixw), cspec(bias),
                pl.BlockSpec((tm, LANES), pos_map), pl.BlockSpec((tm, LANES), pos_map),
                pl.BlockSpec((tm, LANES), pos_map), cspec(seg)]
    return pl.pallas_call(
        functools.partial(_project_body, emit_vn, emit_vt),
        grid=(rows // tm,),
        in_specs=in_specs,
        out_specs=out_specs,
        out_shape=out_shapes,
        compiler_params=pltpu.CompilerParams(dimension_semantics=("arbitrary",),
                                             vmem_limit_bytes=VMEM_LIMIT),
        name="project",
    )(x2, ng, w, qg, kg, lng, lnb, mixw, bias, cos, sa, sb, seg)


def _key_to_float(u):
    bits = jnp.where(u < 0, u ^ I32(INT_MIN), ~u)
    return lax.bitcast_convert_type(bits, F32)


def _count(sc_ref, n_tiles, key_axis, pred, unroll=1, tail=0):
    a, b = sc_ref.shape[1], sc_ref.shape[2]

    def fold(ind):
        if key_axis == 1:
            return ind
        parts = [ind[i * SUBLANES:(i + 1) * SUBLANES] for i in range(a // SUBLANES)]
        while len(parts) > 1:
            parts = [parts[i] + parts[i + 1] for i in range(0, len(parts), 2)]
        return parts[0]

    def tiles(j0, n, c):
        for u in range(n):
            c = c + fold(jnp.where(pred(sc_ref[j0 + u], j0 + u), 1.0, 0.0))
        return c

    init = jnp.zeros((a, b) if key_axis == 1 else (SUBLANES, b), F32)
    n_steps = n_tiles // unroll
    c = lax.fori_loop(0, n_steps, lambda i, c: tiles(i * unroll, unroll, c), init)
    if tail:
        c = lax.cond(n_tiles - n_steps * unroll == tail, lambda c: tiles(n_tiles - tail, tail, c),
                     lambda c: c, c)
    return jnp.sum(c, axis=key_axis, keepdims=True)


def _select_threshold(sc_ref, n_tiles, k_row, key_axis, unroll=1, tail=0, row_max=None):
    k_f = k_row.astype(F32)

    def count_ge(cand):
        t = _key_to_float(cand)
        return _count(sc_ref, n_tiles, key_axis, lambda x, j: x >= t, unroll, tail)

    if row_max is not None:
        mbits = lax.bitcast_convert_type(row_max, I32)
        max_key = jnp.where(mbits < 0, ~mbits, mbits ^ I32(INT_MIN)) ^ I32(INT_MIN)

    def bit_step(i, carry, bounded):
        res, cnt_res = carry
        cand = res | lax.shift_left(I32(1), I32(31) - i)
        if bounded:
            reachable = jnp.max(jnp.where((cand ^ I32(INT_MIN)) <= max_key, 1.0, 0.0)) > 0.0
            cnt = lax.cond(reachable, lambda: count_ge(cand), lambda: jnp.zeros(k_row.shape, F32))
        else:
            cnt = count_ge(cand)
        ok = cnt >= k_f
        return jnp.where(ok, cand, res), jnp.where(ok, cnt, cnt_res)

    state = (jnp.zeros(k_row.shape, I32), jnp.zeros(k_row.shape, F32))
    n_bounded = 0
    if row_max is not None:
        n_bounded = BOUNDED_STEPS
        state = lax.fori_loop(0, n_bounded, functools.partial(bit_step, bounded=True), state)
    plain = functools.partial(bit_step, bounded=False)
    state = lax.fori_loop(n_bounded, FIRST_SETTLE_CHECK, plain, state)
    for lo in range(FIRST_SETTLE_CHECK, 32, SETTLE_CHECK_EVERY):
        settled = jnp.min(jnp.where(state[1] == k_f, 1.0, 0.0)) > 0.0
        state = lax.cond(settled, lambda s: s,
                         lambda s, lo=lo: lax.fori_loop(lo, lo + SETTLE_CHECK_EVERY, plain, s), state)
    res, cnt_res = state
    thr = _key_to_float(res)
    _drop_surplus_ties(sc_ref, n_tiles, thr, k_f, cnt_res, key_axis)
    return thr


def _drop_surplus_ties(sc_ref, n_tiles, thr, k_f, cnt_ge, key_axis):
    t = sc_ref.shape[1 + key_axis]
    surplus = jnp.max(cnt_ge - k_f)

    @pl.when(surplus > 0.0)
    def _():
        n_take = k_f - _count(sc_ref, n_tiles, key_axis, lambda x, j: x > thr)
        r = lax.broadcasted_iota(I32, (t, t), 0)
        c = lax.broadcasted_iota(I32, (t, t), 1)
        tri = jnp.where((r >= c) if key_axis == 0 else (r <= c), 1.0, 0.0).astype(BF16)

        def body(j, before):
            x = sc_ref[j]
            tied = jnp.where(x == thr, 1.0, 0.0)
            if key_axis == 0:
                upto = jnp.dot(tri, tied.astype(BF16), preferred_element_type=F32)
                total = upto[t - 1:t]
            else:
                upto = jnp.dot(tied.astype(BF16), tri, preferred_element_type=F32)
                total = upto[:, t - 1:t]
            rank = tied * (before + upto)
            sc_ref[j] = jnp.where(rank > n_take, NEG_INF, x)
            return before + total

        lax.fori_loop(0, n_tiles, body, jnp.zeros(thr.shape, F32))


def _attend_init(m_ref, l_ref, acc_ref):
    m_ref[...] = jnp.full(m_ref.shape, NEG_INF, F32)
    l_ref[...] = jnp.zeros(l_ref.shape, F32)
    acc_ref[...] = jnp.zeros(acc_ref.shape, F32)


def _softmax_step(s, m_prev, l_prev):
    m_new = jnp.maximum(m_prev, jnp.max(s, axis=1, keepdims=True))
    m_safe = jnp.where(m_new == NEG_INF, 0.0, m_new)
    alpha = jnp.exp2(m_prev - m_safe)
    p = jnp.exp2(s - _tile_lanes(m_safe, s.shape[1]))
    l_new = alpha * l_prev + jnp.sum(p, axis=1, keepdims=True)
    return p, m_new, l_new, alpha


_NT = (((1,), (1,)), ((), ()))


def _two_stage(n_tiles, produce, consume, carry, buf_a, buf_b):
    produce(0, buf_a)

    def pair(j0, c, last):
        produce(j0 + 1, buf_b)
        c = consume(j0, buf_a, c)
        if not last:
            produce(jnp.minimum(j0 + 2, n_tiles - 1), buf_a)
        return consume(j0 + 1, buf_b, c)

    n_quads = n_tiles // 4
    carry = lax.fori_loop(0, n_quads, lambda i, c: pair(4 * i + 2, pair(4 * i, c, False), False), carry)
    return lax.cond(n_tiles - 4 * n_quads == 2, lambda c: pair(n_tiles - 2, c, True), lambda c: c, carry)


def _prompt_attend_body(k_top, bound_ref, qbdt_ref, qiht_ref, kiwi_ref, kb_ref, vt_ref,
                        kib_ref, o_ref, sc_ref, acc_ref, buf_a, buf_b):
    qb = pl.program_id(1)
    n_tiles = 2 * ((qb * Q_BLOCK + Q_BLOCK + 2 * KEY_TILE - 1) // (2 * KEY_TILE))
    qpos = qb * Q_BLOCK + lax.broadcasted_iota(I32, (1, Q_BLOCK), 1)

    w_t = kiwi_ref[...].T[D_IDX:D_IDX + N_IDX_HEADS, :]
    w_pairs = [jnp.concatenate([w_t[2 * p:2 * p + 1], w_t[2 * p + 1:2 * p + 2]], axis=1)
               for p in range(N_IDX_HEADS // 2)]
    krow = lax.broadcasted_iota(I32, (KEY_TILE, Q_BLOCK), 0)

    def score_dots(j, buf):
        start = pl.multiple_of(j * KEY_TILE, KEY_TILE)
        ki_t = kib_ref[pl.ds(start, KEY_TILE), :]
        for p in range(N_IDX_HEADS // 2):
            buf[p] = jnp.dot(ki_t, qiht_ref[p], preferred_element_type=F32)

    def score_sum(j, buf, c):
        acc = jnp.zeros((KEY_TILE, Q_BLOCK), F32)
        for p in range(N_IDX_HEADS // 2):
            d = jnp.maximum(buf[p], 0.0) * w_pairs[p]
            acc = acc + d[:, :Q_BLOCK] + d[:, Q_BLOCK:]
        x = jnp.where(j * KEY_TILE + krow <= qpos, acc, NEG_INF)
        sc_ref[j] = x
        parts = [x[i * SUBLANES:(i + 1) * SUBLANES] for i in range(KEY_TILE // SUBLANES)]
        while len(parts) > 1:
            parts = [jnp.maximum(parts[i], parts[i + 1]) for i in range(0, len(parts), 2)]
        return jnp.maximum(c, parts[0])

    top = _two_stage(n_tiles, score_dots, score_sum, jnp.full((SUBLANES, Q_BLOCK), NEG_INF, F32),
                     buf_a, buf_b)
    row_max = jnp.max(top, axis=0, keepdims=True)

    k_row = jnp.minimum(k_top, qpos + 1)
    thr = _select_threshold(sc_ref, n_tiles, k_row, key_axis=0, unroll=4, tail=2, row_max=row_max)
    thr2 = jnp.concatenate([thr, thr], axis=1)

    acc_ref[...] = jnp.zeros(acc_ref.shape, F32)
    m0 = tuple(jnp.full((1, 2 * Q_BLOCK), NEG_INF, F32) for _ in range(N_KV_HEADS))
    l0 = tuple(jnp.zeros((1, 2 * Q_BLOCK), F32) for _ in range(N_KV_HEADS))

    ones_rows = jnp.ones((BF16_SUBLANES, KEY_TILE), BF16)

    def logits(j, buf):
        start = pl.multiple_of(j * KEY_TILE, KEY_TILE)
        k_t = kb_ref[pl.ds(start, KEY_TILE), :]
        for g in range(N_KV_HEADS):
            buf[g] = jnp.dot(k_t, qbdt_ref[g], preferred_element_type=F32)

    def weighted_values(v_t, g, p):
        lhs = jnp.concatenate([v_t[g * HEAD_DIM:(g + 1) * HEAD_DIM, :], ones_rows], axis=0)
        pv = jnp.dot(lhs, p, preferred_element_type=F32)
        return pv[:HEAD_DIM], pv[HEAD_DIM:HEAD_DIM + 1]

    def softmax_pv(j, buf, carry):
        ms, ls = carry
        v_t = vt_ref[j]
        sc = sc_ref[j]
        keep = jnp.concatenate([sc, sc], axis=1) >= thr2
        new_m, new_l = [], []
        for g in range(N_KV_HEADS):
            s = jnp.where(keep, buf[g], NEG_INF)
            m_new = jnp.maximum(ms[g], jnp.max(s, axis=0, keepdims=True))
            m_safe = jnp.where(m_new == NEG_INF, 0.0, m_new)
            alpha = jnp.exp2(ms[g] - m_safe)
            pv, psum = weighted_values(v_t, g, jnp.exp2(s - m_safe).astype(BF16))
            new_m.append(m_new)
            new_l.append(alpha * ls[g] + psum)
            acc_ref[g] = acc_ref[g] * alpha + pv
        return tuple(new_m), tuple(new_l)

    def exp_pv(j, buf, ls):
        v_t = vt_ref[j]
        sc = sc_ref[j]
        keep = jnp.concatenate([sc, sc], axis=1) >= thr2
        new_l = []
        for g in range(N_KV_HEADS):
            pv, psum = weighted_values(v_t, g, jnp.exp2(jnp.where(keep, buf[g], NEG_INF)).astype(BF16))
            new_l.append(ls[g] + psum)
            acc_ref[g] = acc_ref[g] + pv
        return tuple(new_l)

    def write_out(ls):
        blocks = []
        for g in range(N_KV_HEADS):
            og = acc_ref[g] / ls[g]
            blocks += [og[:, :Q_BLOCK], og[:, Q_BLOCK:]]
        o_ref[...] = jnp.concatenate(blocks, axis=0).T

    bounded = bound_ref[0] < MAX_UNSHIFTED_LOG2

    @pl.when(bounded)
    def _():
        write_out(_two_stage(n_tiles, logits, exp_pv, l0, buf_a, buf_b))

    @pl.when(jnp.logical_not(bounded))
    def _():
        write_out(_two_stage(n_tiles, logits, softmax_pv, (m0, l0), buf_a, buf_b)[1])


def _prompt_attend(logit_bound, qbdt, qiht, kiwi, kb, vt, kib, k_top):
    bsz, nqb = qbdt.shape[:2]
    seq = kb.shape[1]
    resident = dict(pipeline_mode=pl.Buffered(1))
    return pl.pallas_call(
        functools.partial(_prompt_attend_body, k_top),
        grid=(bsz, nqb),
        in_specs=[
            pl.BlockSpec(memory_space=pltpu.SMEM),
            pl.BlockSpec((None, None, N_KV_HEADS, KV_WIDTH, 2 * Q_BLOCK), lambda b, i: (b, i, 0, 0, 0)),
            pl.BlockSpec((None, None, N_IDX_HEADS // 2, LANES, 2 * Q_BLOCK), lambda b, i: (b, i, 0, 0, 0)),
            pl.BlockSpec((Q_BLOCK, LANES), lambda b, i: (b * nqb + i, 0)),
            pl.BlockSpec((None, seq, KV_WIDTH), lambda b, i: (b, 0, 0), **resident),
            pl.BlockSpec((None, seq // KEY_TILE, KV_WIDTH, KEY_TILE), lambda b, i: (b, 0, 0, 0), **resident),
            pl.BlockSpec((None, seq, LANES), lambda b, i: (b, 0, 0), **resident),
        ],
        out_specs=pl.BlockSpec((Q_BLOCK, ATT_WIDTH), lambda b, i: (b * nqb + i, 0)),
        out_shape=jax.ShapeDtypeStruct((bsz * seq, ATT_WIDTH), F32),
        scratch_shapes=[
            pltpu.VMEM((seq // KEY_TILE, KEY_TILE, Q_BLOCK), F32),
            pltpu.VMEM((N_KV_HEADS, HEAD_DIM, 2 * Q_BLOCK), F32),
            pltpu.VMEM((N_KV_HEADS, KEY_TILE, 2 * Q_BLOCK), F32),
            pltpu.VMEM((N_KV_HEADS, KEY_TILE, 2 * Q_BLOCK), F32),
        ],
        compiler_params=pltpu.CompilerParams(dimension_semantics=("arbitrary", "arbitrary"),
                                             vmem_limit_bytes=VMEM_LIMIT),
        name="prompt_attend",
    )(logit_bound, qbdt, qiht, kiwi, kb, vt, kib)


def _sample_attend_body(k_top, n_pages, t_new,
                        pt_ref, qbd_ref, qih_ref, w_ref, knew_ref, vnew_ref, kinew_ref,
                        cik_hbm, ck_hbm, cv_hbm, o_ref,
                        ibuf, kbuf, vbuf, sem, sc_ref, m_ref, l_ref, acc_ref):
    b = pl.program_id(0)
    nb = pl.num_programs(0)
    slot = b % 2
    cp = SAMPLE_CHUNK_PAGES
    n_chunks = n_pages // cp
    width = cp * PAGE_SIZE

    def page_copies(bb, s, p):
        page = pt_ref[bb, p]
        return (pltpu.make_async_copy(cik_hbm.at[page], ibuf.at[s, p], sem.at[s, 0]),
                pltpu.make_async_copy(ck_hbm.at[page], kbuf.at[s, p], sem.at[s, 1]),
                pltpu.make_async_copy(cv_hbm.at[page], vbuf.at[s, p], sem.at[s, 2]))

    def start_fetch(bb, s):
        def body(p, _):
            for c in page_copies(bb, s, p):
                c.start()
            return 0
        lax.fori_loop(0, n_pages, body, 0)

    def wait_fetch(bb, s):
        def body(p, _):
            for c in page_copies(bb, s, p):
                c.wait()
            return 0
        lax.fori_loop(0, n_pages, body, 0)

    @pl.when(b == 0)
    def _():
        start_fetch(b, slot)

    @pl.when(b + 1 < nb)
    def _():
        start_fetch(b + 1, 1 - slot)

    wait_fetch(b, slot)

    def chunk(buf, c):
        return jnp.concatenate([buf[slot, c * cp + i] for i in range(cp)], axis=1).astype(BF16)

    def new_cols(a):
        padded = jnp.concatenate([a, jnp.zeros((LANES - a.shape[0], a.shape[1]), a.dtype)], axis=0)
        return padded.T.astype(BF16)

    qih = qih_ref[...]
    w = _tile_lanes(w_ref[...], width)

    def head_sum(ki_c):
        d = jnp.dot(qih, ki_c, preferred_element_type=F32)
        d = w[:, :ki_c.shape[1]] * jnp.maximum(d, 0.0)
        acc = d[0:t_new]
        for h in range(1, N_IDX_HEADS):
            acc = acc + d[h * t_new:(h + 1) * t_new]
        return acc

    def score_chunk(c, _):
        sc_ref[c] = head_sum(chunk(ibuf, c))
        return 0

    lax.fori_loop(0, n_chunks, score_chunk, 0)

    tq = lax.broadcasted_iota(I32, (t_new, LANES), 0)
    col = lax.broadcasted_iota(I32, (t_new, LANES), 1)
    sc_new = jnp.where((col < t_new) & (col <= tq), head_sum(new_cols(kinew_ref[...])[:D_IDX]), NEG_INF)
    sc_ref[n_chunks] = jnp.concatenate([sc_new, jnp.full((t_new, width - LANES), NEG_INF, F32)], axis=1)

    qpos = n_pages * PAGE_SIZE + lax.broadcasted_iota(I32, (t_new, 1), 0)
    k_row = jnp.minimum(k_top, qpos + 1)
    thr = _select_threshold(sc_ref, n_chunks + 1, k_row, key_axis=1, unroll=n_chunks + 1)

    _attend_init(m_ref, l_ref, acc_ref)
    qbd = qbd_ref[...]

    def attend(sc, k_c, v_c):
        bias = jnp.where(sc >= thr, 0.0, NEG_INF)
        s = jnp.dot(qbd, k_c, preferred_element_type=F32) + jnp.concatenate([bias] * N_HEADS, axis=0)
        p, m_new, l_new, alpha = _softmax_step(s, m_ref[...], l_ref[...])
        m_ref[...] = m_new
        l_ref[...] = l_new
        acc_ref[...] = (acc_ref[...] * _tile_lanes(alpha, KV_WIDTH)
                        + lax.dot_general(p.astype(BF16), v_c, _NT, preferred_element_type=F32))

    def attend_chunk(c, _):
        attend(sc_ref[c], chunk(kbuf, c), chunk(vbuf, c))
        return 0

    lax.fori_loop(0, n_chunks, attend_chunk, 0)
    attend(sc_ref[n_chunks][:, :LANES], new_cols(knew_ref[...]), new_cols(vnew_ref[...]))

    o_ref[...] = acc_ref[...] / _tile_lanes(l_ref[...], KV_WIDTH)


def _sample_attend(page_table, qbd, qih, w_rows, knew, vnew, kinew, cik_t, ck_t, cv_t, k_top):
    dbsz, n_pages = page_table.shape
    t_new = knew.shape[1]
    rows = N_HEADS * t_new
    past = n_pages * PAGE_SIZE
    assert n_pages % SAMPLE_CHUNK_PAGES == 0
    width = SAMPLE_CHUNK_PAGES * PAGE_SIZE
    n_chunks = n_pages // SAMPLE_CHUNK_PAGES

    def bmap(b, pt):
        return (b, 0, 0)

    grid_spec = pltpu.PrefetchScalarGridSpec(
        num_scalar_prefetch=1,
        grid=(dbsz,),
        in_specs=[
            pl.BlockSpec((None, rows, KV_WIDTH), bmap),
            pl.BlockSpec((None, rows, D_IDX), bmap),
            pl.BlockSpec((None, rows, LANES), bmap),
            pl.BlockSpec((None, t_new, KV_WIDTH), bmap),
            pl.BlockSpec((None, t_new, KV_WIDTH), bmap),
            pl.BlockSpec((None, t_new, LANES), bmap),
            pl.BlockSpec(memory_space=pl.ANY),
            pl.BlockSpec(memory_space=pl.ANY),
            pl.BlockSpec(memory_space=pl.ANY),
        ],
        out_specs=pl.BlockSpec((None, rows, KV_WIDTH), bmap),
        scratch_shapes=[
            pltpu.VMEM((2, n_pages, D_IDX, PAGE_SIZE), F32),
            pltpu.VMEM((2, n_pages, KV_WIDTH, PAGE_SIZE), F32),
            pltpu.VMEM((2, n_pages, KV_WIDTH, PAGE_SIZE), F32),
            pltpu.SemaphoreType.DMA((2, 3)),
            pltpu.VMEM((n_chunks + 1, t_new, width), F32),
            pltpu.VMEM((rows, LANES), F32),
            pltpu.VMEM((rows, LANES), F32),
            pltpu.VMEM((rows, KV_WIDTH), F32),
        ],
    )
    return pl.pallas_call(
        functools.partial(_sample_attend_body, k_top, n_pages, t_new),
        grid_spec=grid_spec,
        out_shape=jax.ShapeDtypeStruct((dbsz, rows, KV_WIDTH), F32),
        compiler_params=pltpu.CompilerParams(dimension_semantics=("arbitrary",),
                                             vmem_limit_bytes=VMEM_LIMIT),
        name="sample_attend",
    )(page_table, qbd, qih, w_rows, knew, vnew, kinew, cik_t, ck_t, cv_t)


def _finish_body(x_ref, o_ref, ga_ref, gm_ref, p_ref, wo_ref, pg_ref, wg_ref, wp_ref, y_ref):
    att = (o_ref[...] * ga_ref[...]).astype(BF16)
    r = (x_ref[...]
         + jnp.dot(att, wo_ref[:ATT_WIDTH, :], preferred_element_type=F32)
         + jnp.dot(gm_ref[...].astype(BF16), wo_ref[ATT_WIDTH:, :], preferred_element_type=F32))
    rn = r * lax.rsqrt(jnp.mean(r * r, axis=-1, keepdims=True) + EPS) * pg_ref[...]
    gate = jax.nn.sigmoid(jnp.dot(rn.astype(BF16), wg_ref[...], preferred_element_type=F32))
    y_ref[...] = r + gate * jnp.dot(p_ref[...].astype(BF16), wp_ref[...], preferred_element_type=F32)


def _finish(x2, o, ga, gm, p2, wo, pg, wg, wp):
    rows, d_model = x2.shape
    tm = min(ROW_TILE, rows)

    def rspec(a):
        return pl.BlockSpec((tm, a.shape[1]), lambda i: (i, 0))

    def cspec(a):
        return pl.BlockSpec(a.shape, lambda i: (0, 0))

    return pl.pallas_call(
        _finish_body,
        grid=(rows // tm,),
        in_specs=[rspec(x2), rspec(o), rspec(ga), rspec(gm), rspec(p2),
                  cspec(wo), cspec(pg), cspec(wg), cspec(wp)],
        out_specs=pl.BlockSpec((tm, d_model), lambda i: (i, 0)),
        out_shape=jax.ShapeDtypeStruct((rows, d_model), F32),
        compiler_params=pltpu.CompilerParams(dimension_semantics=("arbitrary",),
                                             vmem_limit_bytes=VMEM_LIMIT),
        name="finish",
    )(x2, o, ga, gm, p2, wo, pg, wg, wp)


def _rope_tables(pos):
    inv = jnp.power(ROPE_THETA, -jnp.arange(ROPE_HALF, dtype=F32) * 2.0 / ROPE_ROT)
    ang = pos.astype(F32)[:, None] * inv[None, :]
    cos, sin = jnp.cos(ang), jnp.sin(ang)
    n = pos.shape[0]
    one = jnp.ones((n, HEAD_DIM - ROPE_ROT), F32)
    zero8 = jnp.zeros((n, ROPE_HALF), F32)
    zero = jnp.zeros((n, HEAD_DIM - ROPE_ROT), F32)
    cos_h = jnp.concatenate([cos, cos, one], axis=1)
    sa_h = jnp.concatenate([-sin, zero8, zero], axis=1)
    sb_h = jnp.concatenate([zero8, sin, zero], axis=1)
    return tuple(jnp.concatenate([t, t], axis=1) for t in (cos_h, sa_h, sb_h))


def _pair_mix(ws):
    g = ws.shape[0]
    return ws.reshape(g // 2, 2, CHUNK, CHUNK).transpose(0, 2, 1, 3).reshape(g // 2, CHUNK, 2 * CHUNK).astype(BF16)


def _group_diag(q, lead):
    t = q.shape[-2]
    n = len(lead)
    q6 = q.reshape(*lead, t, N_KV_HEADS, 2, HEAD_DIM)
    q6 = jnp.moveaxis(q6, n, n + 2)
    q6 = q6.reshape(*lead, N_KV_HEADS, 2 * t, HEAD_DIM)
    eye = jnp.eye(N_KV_HEADS, dtype=q.dtype)
    qd = q6[..., None, :] * eye[:, None, :, None]
    return qd.reshape(*lead, N_KV_HEADS, 2 * t, KV_WIDTH)


def _heads_to_cols(x):
    b, n, nq, width = x.shape
    x6 = x.reshape(b, n, nq, width // (2 * HEAD_DIM), 2, HEAD_DIM)
    return x6.transpose(0, 1, 3, 5, 4, 2).reshape(b, n, width // (2 * HEAD_DIM), HEAD_DIM, 2 * nq)


def kernel(x_prompt, x_sample, cache_k, cache_v, cache_idx_k, page_table, p_prompt, p_sample,
           norm_in_g, w_in, q_norm_g, k_norm_g, ln_v_g, ln_v_b, w_s, b_s, w_out,
           ple_norm_g, w_ple_gate, w_ple_proj):
    depth = w_in.shape[0]
    assert depth == 1
    bsz, seq, d_model = x_prompt.shape
    dbsz, t_new, _ = x_sample.shape
    n_pages = page_table.shape[1]
    past = n_pages * PAGE_SIZE
    assert seq % KEY_TILE == 0 and CHUNK % t_new == 0 and (dbsz * t_new) % CHUNK == 0

    perm = np.concatenate([np.arange(0, _ORIG_KI), np.arange(_ORIG_GA, _ORIG_END),
                           np.arange(_ORIG_KI, _ORIG_GA)])
    w = jnp.pad(w_in[0][:, perm], ((0, 0), (0, C_END - _ORIG_END))).astype(BF16)
    seg = jnp.asarray(np.kron(np.eye(N_HEADS), np.full((HEAD_DIM, HEAD_DIM), 1.0 / HEAD_DIM)), BF16)
    weights = (norm_in_g[0][None, :], w, jnp.tile(q_norm_g[0], N_HEADS)[None, :],
               jnp.tile(k_norm_g[0], N_KV_HEADS)[None, :], ln_v_g[0][None, :], ln_v_b[0][None, :], seg)
    ws_tril = jnp.where(jnp.tril(jnp.ones((CHUNK, CHUNK), bool))[None], w_s[0], 0.0)
    bias_p = jnp.repeat(b_s[0].T, GM_WIDTH // GM_GROUPS, axis=1)
    reps = CHUNK // t_new
    ws_s = jnp.einsum("ab,gij->gaibj", jnp.eye(reps, dtype=F32),
                      ws_tril[:, :t_new, :t_new]).reshape(GM_GROUPS, CHUNK, CHUNK)
    bias_s = jnp.tile(bias_p[:t_new], (reps, 1))
    wo = w_out[0].astype(BF16)
    wg = w_ple_gate[0].astype(BF16)
    wp = w_ple_proj[0].astype(BF16)
    pg = ple_norm_g[0][None, :]

    xp = x_prompt.reshape(bsz * seq, d_model)
    (q, k, v, kb, qi, kiwi, kib, ga, gm, vt) = _project(
        xp, _rope_tables(jnp.arange(seq)), _pair_mix(ws_tril), bias_p, weights,
        emit_vn=False, emit_vt=True)
    nqb = seq // Q_BLOCK
    k_top_p = min(TOPK_MAX, seq // 4)
    eye = jnp.eye(N_KV_HEADS, dtype=BF16)
    qt = _heads_to_cols(q.reshape(bsz, nqb, Q_BLOCK, ATT_WIDTH))
    qbdt = (qt[:, :, :, None] * eye[:, :, None, None]).reshape(bsz, nqb, N_KV_HEADS, KV_WIDTH, 2 * Q_BLOCK)
    qiht = jnp.pad(_heads_to_cols(qi.reshape(bsz, nqb, Q_BLOCK, IDX_WIDTH)),
                   ((0, 0),) * 3 + ((0, LANES - D_IDX), (0, 0)))
    logit_bound = (HEAD_DIM ** 0.5 * LOG2_E * BOUND_MARGIN
                   * jnp.max(jnp.abs(q_norm_g[0])) * jnp.max(jnp.abs(k_norm_g[0]))).reshape(1)
    o = _prompt_attend(logit_bound, qbdt, qiht, kiwi, kb.reshape(bsz, seq, KV_WIDTH),
                       vt.reshape(bsz, seq // KEY_TILE, KV_WIDTH, KEY_TILE),
                       kib.reshape(bsz, seq, LANES), k_top_p)
    y_prompt = _finish(xp, o, ga, gm, p_prompt[0].reshape(bsz * seq, -1), wo, pg, wg, wp)

    xs = x_sample.reshape(dbsz * t_new, d_model)
    pos_s = past + jnp.arange(min(ROW_TILE, dbsz * t_new)) % t_new
    (q, k_s, v_s, _, qi, kiwi_s, _, ga, gm, vn_s) = _project(
        xs, _rope_tables(pos_s), _pair_mix(ws_s), bias_s, weights, emit_vn=True, emit_vt=False)
    k_top_s = min(TOPK_MAX, (past + t_new) // 4)
    qbd = _group_diag(q.reshape(dbsz, t_new, ATT_WIDTH), (dbsz,)).reshape(dbsz, N_HEADS * t_new, KV_WIDTH)
    qih = qi.reshape(dbsz, t_new, N_IDX_HEADS, D_IDX).transpose(0, 2, 1, 3).reshape(
        dbsz, N_IDX_HEADS * t_new, D_IDX)
    kiwi3 = kiwi_s.reshape(dbsz, t_new, LANES)
    w_rows = kiwi3[:, :, D_IDX:D_IDX + N_IDX_HEADS].transpose(0, 2, 1).reshape(dbsz, N_IDX_HEADS * t_new, 1)
    w_rows = jnp.broadcast_to(w_rows, (dbsz, N_IDX_HEADS * t_new, LANES))
    acc = _sample_attend(page_table, qbd, qih, w_rows,
                         k_s.reshape(dbsz, t_new, KV_WIDTH), v_s.reshape(dbsz, t_new, KV_WIDTH), kiwi3,
                         cache_idx_k[0].transpose(0, 2, 1),
                         cache_k[0].transpose(0, 2, 3, 1).reshape(-1, KV_WIDTH, PAGE_SIZE),
                         cache_v[0].transpose(0, 2, 3, 1).reshape(-1, KV_WIDTH, PAGE_SIZE), k_top_s)
    acc = acc.reshape(dbsz, N_KV_HEADS, 2, t_new, N_KV_HEADS, HEAD_DIM)
    o_s = jnp.stack([acc[:, g, :, :, g, :] for g in range(N_KV_HEADS)], axis=1)
    o_s = o_s.transpose(0, 3, 1, 2, 4).reshape(dbsz * t_new, ATT_WIDTH)
    y_sample = _finish(xs, o_s, ga, gm, p_sample[0].reshape(dbsz * t_new, -1), wo, pg, wg, wp)

    return (y_prompt.reshape(bsz, seq, d_model),
            y_sample.reshape(dbsz, t_new, d_model),
            k.reshape(1, bsz, seq, N_KV_HEADS, HEAD_DIM),
            v.reshape(1, bsz, seq, N_KV_HEADS, HEAD_DIM),
            kiwi[:, :D_IDX].reshape(1, bsz, seq, D_IDX),
            k_s.reshape(1, dbsz, t_new, N_KV_HEADS, HEAD_DIM),
            v_s.reshape(1, dbsz, t_new, N_KV_HEADS, HEAD_DIM),
            kiwi_s[:, :D_IDX].reshape(1, dbsz, t_new, D_IDX),
            vn_s.reshape(1, dbsz, t_new, GM_WIDTH))
```

```python
import functools

import numpy as np
import jax
import jax.numpy as jnp
from jax import lax
from jax.experimental import pallas as pl
from jax.experimental.pallas import tpu as pltpu

F32 = jnp.float32
BF16 = jnp.bfloat16
I32 = jnp.int32

HEAD_DIM = 64
N_HEADS = 8
N_KV_HEADS = 4
N_IDX_HEADS = 8
D_IDX = 64
TOPK_MAX = 256
GM_GROUPS = 8
CHUNK = 128
PAGE_SIZE = 128
ROPE_THETA = 500000.0
ROPE_ROT = HEAD_DIM // 4
ROPE_HALF = ROPE_ROT // 2
EPS = 1e-6

ATT_WIDTH = N_HEADS * HEAD_DIM
KV_WIDTH = N_KV_HEADS * HEAD_DIM
IDX_WIDTH = N_IDX_HEADS * D_IDX
GM_WIDTH = GM_GROUPS * 64
LANES = 128
SUBLANES = 8
BF16_SUBLANES = 16
KEY_TILE = 256
Q_BLOCK = 128
ROW_TILE = 256
SAMPLE_CHUNK_PAGES = 8
VMEM_LIMIT = 56 * 1024 * 1024

C_Q, C_K, C_V, C_QI, C_GA, C_U, C_VM, C_GM, C_KIWI, C_END = (
    0, 512, 768, 1024, 1536, 2048, 2560, 3072, 3584, 3712)
_ORIG_KI, _ORIG_GA, _ORIG_END = 1536, 1608, 3656
LOG2_E = 1.4426950408889634
MAX_UNSHIFTED_LOG2 = 60.0
BOUND_MARGIN = 1.05
FIRST_SETTLE_CHECK = 20
SETTLE_CHECK_EVERY = 4
NEG_INF = float("-inf")
INT_MIN = -2 ** 31


def _head_rms(xh, seg, g):
    sq = xh * xh
    hi = sq.astype(BF16)
    lo = (sq - hi.astype(F32)).astype(BF16)
    ms = (jnp.dot(hi, seg, preferred_element_type=F32)
          + jnp.dot(lo, seg, preferred_element_type=F32))
    return xh * lax.rsqrt(ms + EPS) * g


def _rope(xh, cos, sa, sb):
    w = xh.shape[-1]
    return xh * cos + pltpu.roll(xh, w - ROPE_HALF, 1) * sa + pltpu.roll(xh, ROPE_HALF, 1) * sb


def _tile_lanes(t, width):
    reps = width // t.shape[-1]
    return t if reps == 1 else jnp.concatenate([t] * reps, axis=1)


def _project_body(emit_vn, emit_vt, x_ref, ng_ref, w_ref, qg_ref, kg_ref, lng_ref, lnb_ref, mixw_ref,
                  bias_ref, cos_ref, sa_ref, sb_ref, seg_ref, *outs):
    q_o, k_o, v_o, kb_o, qi_o, kiwi_o, kib_o, ga_o, gm_o = outs[:9]
    extra = list(outs[9:])
    vn_o = extra.pop(0) if emit_vn else None
    vt_o = extra.pop(0) if emit_vt else None
    x = x_ref[...]
    h = x * lax.rsqrt(jnp.mean(x * x, axis=-1, keepdims=True) + EPS) * ng_ref[...]
    hb = h.astype(BF16)

    def proj(c0, c1):
        return jnp.dot(hb, w_ref[:, c0:c1], preferred_element_type=F32)

    cos, sa, sb = cos_ref[...], sa_ref[...], sb_ref[...]
    seg = seg_ref[...]

    q = _head_rms(proj(C_Q, C_K), seg, qg_ref[...])
    q = _rope(q, _tile_lanes(cos, ATT_WIDTH), _tile_lanes(sa, ATT_WIDTH), _tile_lanes(sb, ATT_WIDTH))
    q_o[...] = (q * (HEAD_DIM ** -0.5 * LOG2_E)).astype(BF16)

    k = _head_rms(proj(C_K, C_V), seg[:KV_WIDTH, :KV_WIDTH], kg_ref[...])
    k = _rope(k, _tile_lanes(cos, KV_WIDTH), _tile_lanes(sa, KV_WIDTH), _tile_lanes(sb, KV_WIDTH))
    k_o[...] = k
    kb_o[...] = k.astype(BF16)

    v = proj(C_V, C_QI)
    v_o[...] = v
    if emit_vt:
        vt_o[...] = v.T.astype(BF16)

    qi = _rope(proj(C_QI, C_GA), _tile_lanes(cos, IDX_WIDTH), _tile_lanes(sa, IDX_WIDTH),
               _tile_lanes(sb, IDX_WIDTH))
    qi_o[...] = qi.astype(BF16)

    lane = lax.broadcasted_iota(I32, cos.shape, 1)
    is_ki = lane < D_IDX
    wi_scale = jnp.where(lane < D_IDX + N_IDX_HEADS, IDX_WIDTH ** -0.5, 1.0)
    kiwi = _rope(proj(C_KIWI, C_END), jnp.where(is_ki, cos, wi_scale),
                 jnp.where(is_ki, sa, 0.0), jnp.where(is_ki, sb, 0.0))
    kiwi_o[...] = kiwi
    kib_o[...] = kiwi.astype(BF16)

    ga_o[...] = jax.nn.silu(proj(C_GA, C_U))

    vmg = jax.nn.gelu(proj(C_VM, C_GM))
    xc = vmg - jnp.mean(vmg, axis=-1, keepdims=True)
    vn = xc * lax.rsqrt(jnp.mean(xc * xc, axis=-1, keepdims=True) + EPS) * lng_ref[...] + lnb_ref[...]
    if emit_vn:
        vn_o[...] = vn

    lane_c = lax.broadcasted_iota(I32, (CHUNK, LANES), 1)
    rows = x.shape[0]
    chunks = []
    for c in range(rows // CHUNK):
        pieces = []
        for p in range(GM_GROUPS // 2):
            t = vn[c * CHUNK:(c + 1) * CHUNK, p * LANES:(p + 1) * LANES]
            rhs = jnp.concatenate([jnp.where(lane_c < 64, t, 0.0), jnp.where(lane_c >= 64, t, 0.0)],
                                  axis=0).astype(BF16)
            pieces.append(jnp.dot(mixw_ref[p], rhs, preferred_element_type=F32))
        chunks.append(jnp.concatenate(pieces, axis=1) + bias_ref[...])
    s = jnp.concatenate(chunks, axis=0)
    gm_o[...] = jax.nn.gelu(proj(C_U, C_VM)) * s * jax.nn.silu(proj(C_GM, C_KIWI))


def _project(x2, pos_tables, mixw, bias, weights, emit_vn, emit_vt):
    rows = x2.shape[0]
    tm = min(ROW_TILE, rows)
    assert rows % tm == 0 and tm % CHUNK == 0
    ng, w, qg, kg, lng, lnb, seg = weights
    cos, sa, sb = pos_tables
    assert cos.shape[0] % tm == 0
    n_pos_blocks = cos.shape[0] // tm

    def row_map(i):
        return (i, 0)

    def pos_map(i):
        return (i % n_pos_blocks, 0)

    def const(i):
        return (0, 0)

    def rspec(width):
        return pl.BlockSpec((tm, width), row_map)

    def cspec(a):
        if a.ndim == 3:
            return pl.BlockSpec(a.shape, lambda i: (0, 0, 0))
        return pl.BlockSpec(a.shape, const)

    out_shapes = [
        jax.ShapeDtypeStruct((rows, ATT_WIDTH), BF16),
        jax.ShapeDtypeStruct((rows, KV_WIDTH), F32),
        jax.ShapeDtypeStruct((rows, KV_WIDTH), F32),
        jax.ShapeDtypeStruct((rows, KV_WIDTH), BF16),
        jax.ShapeDtypeStruct((rows, IDX_WIDTH), BF16),
        jax.ShapeDtypeStruct((rows, LANES), F32),
        jax.ShapeDtypeStruct((rows, LANES), BF16),
        jax.ShapeDtypeStruct((rows, ATT_WIDTH), F32),
        jax.ShapeDtypeStruct((rows, GM_WIDTH), F32),
    ]
    if emit_vn:
        out_shapes.append(jax.ShapeDtypeStruct((rows, GM_WIDTH), F32))
    out_specs = [rspec(s.shape[1]) for s in out_shapes]
    if emit_vt:
        assert tm == KEY_TILE
        out_shapes.append(jax.ShapeDtypeStruct((rows // tm, KV_WIDTH, tm), BF16))
        out_specs.append(pl.BlockSpec((None, KV_WIDTH, tm), lambda i: (i, 0, 0)))
    in_specs = [rspec(x2.shape[1]), cspec(ng), cspec(w), cspec(qg), cspec(kg), cspec(lng), cspec(lnb),
                cspec(mixw), cspec(bias),
                pl.BlockSpec((tm, LANES), pos_map), pl.BlockSpec((tm, LANES), pos_map),
                pl.BlockSpec((tm, LANES), pos_map), cspec(seg)]
    return pl.pallas_call(
        functools.partial(_project_body, emit_vn, emit_vt),
        grid=(rows // tm,),
        in_specs=in_specs,
        out_specs=out_specs,
        out_shape=out_shapes,
        compiler_params=pltpu.CompilerParams(dimension_semantics=("arbitrary",),
                                             vmem_limit_bytes=VMEM_LIMIT),
        name="project",
    )(x2, ng, w, qg, kg, lng, lnb, mixw, bias, cos, sa, sb, seg)


def _key_to_float(u):
    bits = jnp.where(u < 0, u ^ I32(INT_MIN), ~u)
    return lax.bitcast_convert_type(bits, F32)


def _count(sc_ref, n_tiles, key_axis, pred, unroll=1, tail=0):
    a, b = sc_ref.shape[1], sc_ref.shape[2]

    def fold(ind):
        if key_axis == 1:
            return ind
        parts = [ind[i * SUBLANES:(i + 1) * SUBLANES] for i in range(a // SUBLANES)]
        while len(parts) > 1:
            parts = [parts[i] + parts[i + 1] for i in range(0, len(parts), 2)]
        return parts[0]

    def tiles(j0, n, c):
        for u in range(n):
            c = c + fold(jnp.where(pred(sc_ref[j0 + u], j0 + u), 1.0, 0.0))
        return c

    init = jnp.zeros((a, b) if key_axis == 1 else (SUBLANES, b), F32)
    n_steps = n_tiles // unroll
    c = lax.fori_loop(0, n_steps, lambda i, c: tiles(i * unroll, unroll, c), init)
    if tail:
        c = lax.cond(n_tiles - n_steps * unroll == tail, lambda c: tiles(n_tiles - tail, tail, c),
                     lambda c: c, c)
    return jnp.sum(c, axis=key_axis, keepdims=True)


def _select_threshold(sc_ref, n_tiles, k_row, key_axis, unroll=1, tail=0):
    k_f = k_row.astype(F32)

    def bit_step(i, carry):
        res, cnt_res = carry
        cand = res | lax.shift_left(I32(1), I32(31) - i)
        t = _key_to_float(cand)
        cnt = _count(sc_ref, n_tiles, key_axis, lambda x, j: x >= t, unroll, tail)
        ok = cnt >= k_f
        return jnp.where(ok, cand, res), jnp.where(ok, cnt, cnt_res)

    state = (jnp.zeros(k_row.shape, I32), jnp.zeros(k_row.shape, F32))
    state = lax.fori_loop(0, FIRST_SETTLE_CHECK, bit_step, state)
    for lo in range(FIRST_SETTLE_CHECK, 32, SETTLE_CHECK_EVERY):
        settled = jnp.min(jnp.where(state[1] == k_f, 1.0, 0.0)) > 0.0
        state = lax.cond(settled, lambda s: s,
                         lambda s, lo=lo: lax.fori_loop(lo, lo + SETTLE_CHECK_EVERY, bit_step, s), state)
    res, cnt_res = state
    thr = _key_to_float(res)
    _drop_surplus_ties(sc_ref, n_tiles, thr, k_f, cnt_res, key_axis)
    return thr


def _drop_surplus_ties(sc_ref, n_tiles, thr, k_f, cnt_ge, key_axis):
    t = sc_ref.shape[1 + key_axis]
    surplus = jnp.max(cnt_ge - k_f)

    @pl.when(surplus > 0.0)
    def _():
        n_take = k_f - _count(sc_ref, n_tiles, key_axis, lambda x, j: x > thr)
        r = lax.broadcasted_iota(I32, (t, t), 0)
        c = lax.broadcasted_iota(I32, (t, t), 1)
        tri = jnp.where((r >= c) if key_axis == 0 else (r <= c), 1.0, 0.0).astype(BF16)

        def body(j, before):
            x = sc_ref[j]
            tied = jnp.where(x == thr, 1.0, 0.0)
            if key_axis == 0:
                upto = jnp.dot(tri, tied.astype(BF16), preferred_element_type=F32)
                total = upto[t - 1:t]
            else:
                upto = jnp.dot(tied.astype(BF16), tri, preferred_element_type=F32)
                total = upto[:, t - 1:t]
            rank = tied * (before + upto)
            sc_ref[j] = jnp.where(rank > n_take, NEG_INF, x)
            return before + total

        lax.fori_loop(0, n_tiles, body, jnp.zeros(thr.shape, F32))


def _attend_init(m_ref, l_ref, acc_ref):
    m_ref[...] = jnp.full(m_ref.shape, NEG_INF, F32)
    l_ref[...] = jnp.zeros(l_ref.shape, F32)
    acc_ref[...] = jnp.zeros(acc_ref.shape, F32)


def _softmax_step(s, m_prev, l_prev):
    m_new = jnp.maximum(m_prev, jnp.max(s, axis=1, keepdims=True))
    m_safe = jnp.where(m_new == NEG_INF, 0.0, m_new)
    alpha = jnp.exp2(m_prev - m_safe)
    p = jnp.exp2(s - _tile_lanes(m_safe, s.shape[1]))
    l_new = alpha * l_prev + jnp.sum(p, axis=1, keepdims=True)
    return p, m_new, l_new, alpha


_NT = (((1,), (1,)), ((), ()))


def _two_stage(n_tiles, produce, consume, carry, buf_a, buf_b):
    produce(0, buf_a)

    def pair(j0, c, last):
        produce(j0 + 1, buf_b)
        c = consume(j0, buf_a, c)
        if not last:
            produce(jnp.minimum(j0 + 2, n_tiles - 1), buf_a)
        return consume(j0 + 1, buf_b, c)

    n_quads = n_tiles // 4
    carry = lax.fori_loop(0, n_quads, lambda i, c: pair(4 * i + 2, pair(4 * i, c, False), False), carry)
    return lax.cond(n_tiles - 4 * n_quads == 2, lambda c: pair(n_tiles - 2, c, True), lambda c: c, carry)


def _prompt_attend_body(k_top, bound_ref, qt_ref, qit_ref, kiwi_ref, kb_ref, vt_ref,
                        kib_ref, o_ref, sc_ref, acc_ref, buf_a, buf_b, qbdt_ref, qiht_ref):
    qb = pl.program_id(1)

    @pl.when((pl.program_id(0) == 0) & (qb == 0))
    def _():
        qbdt_ref[...] = jnp.zeros(qbdt_ref.shape, BF16)
        qiht_ref[...] = jnp.zeros(qiht_ref.shape, BF16)

    for g in range(N_KV_HEADS):
        qbdt_ref[g, g * HEAD_DIM:(g + 1) * HEAD_DIM, :] = qt_ref[g]
        qiht_ref[g, :D_IDX, :] = qit_ref[g]
    n_tiles = 2 * ((qb * Q_BLOCK + Q_BLOCK + 2 * KEY_TILE - 1) // (2 * KEY_TILE))
    qpos = qb * Q_BLOCK + lax.broadcasted_iota(I32, (1, Q_BLOCK), 1)

    w_t = kiwi_ref[...].T[D_IDX:D_IDX + N_IDX_HEADS, :]
    w_pairs = [jnp.concatenate([w_t[2 * p:2 * p + 1], w_t[2 * p + 1:2 * p + 2]], axis=1)
               for p in range(N_IDX_HEADS // 2)]
    krow = lax.broadcasted_iota(I32, (KEY_TILE, Q_BLOCK), 0)

    def score_dots(j, buf):
        start = pl.multiple_of(j * KEY_TILE, KEY_TILE)
        ki_t = kib_ref[pl.ds(start, KEY_TILE), :]
        for p in range(N_IDX_HEADS // 2):
            buf[p] = jnp.dot(ki_t, qiht_ref[p], preferred_element_type=F32)

    def score_sum(j, buf, c):
        acc = jnp.zeros((KEY_TILE, Q_BLOCK), F32)
        for p in range(N_IDX_HEADS // 2):
            d = jnp.maximum(buf[p], 0.0) * w_pairs[p]
            acc = acc + d[:, :Q_BLOCK] + d[:, Q_BLOCK:]
        sc_ref[j] = jnp.where(j * KEY_TILE + krow <= qpos, acc, NEG_INF)
        return c

    _two_stage(n_tiles, score_dots, score_sum, 0, buf_a, buf_b)

    k_row = jnp.minimum(k_top, qpos + 1)
    thr = _select_threshold(sc_ref, n_tiles, k_row, key_axis=0, unroll=4, tail=2)
    thr2 = jnp.concatenate([thr, thr], axis=1)

    acc_ref[...] = jnp.zeros(acc_ref.shape, F32)
    m0 = tuple(jnp.full((1, 2 * Q_BLOCK), NEG_INF, F32) for _ in range(N_KV_HEADS))
    l0 = tuple(jnp.zeros((1, 2 * Q_BLOCK), F32) for _ in range(N_KV_HEADS))

    ones_rows = jnp.ones((BF16_SUBLANES, KEY_TILE), BF16)

    def logits(j, buf):
        start = pl.multiple_of(j * KEY_TILE, KEY_TILE)
        k_t = kb_ref[pl.ds(start, KEY_TILE), :]
        for g in range(N_KV_HEADS):
            buf[g] = jnp.dot(k_t, qbdt_ref[g], preferred_element_type=F32)

    def weighted_values(v_t, g, p):
        lhs = jnp.concatenate([v_t[g * HEAD_DIM:(g + 1) * HEAD_DIM, :], ones_rows], axis=0)
        pv = jnp.dot(lhs, p, preferred_element_type=F32)
        return pv[:HEAD_DIM], pv[HEAD_DIM:HEAD_DIM + 1]

    def softmax_pv(j, buf, carry):
        ms, ls = carry
        v_t = vt_ref[j]
        sc = sc_ref[j]
        keep = jnp.concatenate([sc, sc], axis=1) >= thr2
        new_m, new_l = [], []
        for g in range(N_KV_HEADS):
            s = jnp.where(keep, buf[g], NEG_INF)
            m_new = jnp.maximum(ms[g], jnp.max(s, axis=0, keepdims=True))
            m_safe = jnp.where(m_new == NEG_INF, 0.0, m_new)
            alpha = jnp.exp2(ms[g] - m_safe)
            pv, psum = weighted_values(v_t, g, jnp.exp2(s - m_safe).astype(BF16))
            new_m.append(m_new)
            new_l.append(alpha * ls[g] + psum)
            acc_ref[g] = acc_ref[g] * alpha + pv
        return tuple(new_m), tuple(new_l)

    def exp_pv(j, buf, ls):
        v_t = vt_ref[j]
        sc = sc_ref[j]
        keep = jnp.concatenate([sc, sc], axis=1) >= thr2
        new_l = []
        for g in range(N_KV_HEADS):
            pv, psum = weighted_values(v_t, g, jnp.exp2(jnp.where(keep, buf[g], NEG_INF)).astype(BF16))
            new_l.append(ls[g] + psum)
            acc_ref[g] = acc_ref[g] + pv
        return tuple(new_l)

    def write_out(ls):
        blocks = []
        for g in range(N_KV_HEADS):
            og = acc_ref[g] / ls[g]
            blocks += [og[:, :Q_BLOCK], og[:, Q_BLOCK:]]
        o_ref[...] = jnp.concatenate(blocks, axis=0).T

    bounded = bound_ref[0] < MAX_UNSHIFTED_LOG2

    @pl.when(bounded)
    def _():
        write_out(_two_stage(n_tiles, logits, exp_pv, l0, buf_a, buf_b))

    @pl.when(jnp.logical_not(bounded))
    def _():
        write_out(_two_stage(n_tiles, logits, softmax_pv, (m0, l0), buf_a, buf_b)[1])


def _prompt_attend(logit_bound, qt, qit, kiwi, kb, vt, kib, k_top):
    bsz, nqb = qt.shape[:2]
    seq = kb.shape[1]
    assert N_IDX_HEADS // 2 == N_KV_HEADS
    resident = dict(pipeline_mode=pl.Buffered(1))
    return pl.pallas_call(
        functools.partial(_prompt_attend_body, k_top),
        grid=(bsz, nqb),
        in_specs=[
            pl.BlockSpec(memory_space=pltpu.SMEM),
            pl.BlockSpec((None, None, N_KV_HEADS, HEAD_DIM, 2 * Q_BLOCK), lambda b, i: (b, i, 0, 0, 0)),
            pl.BlockSpec((None, None, N_IDX_HEADS // 2, D_IDX, 2 * Q_BLOCK), lambda b, i: (b, i, 0, 0, 0)),
            pl.BlockSpec((Q_BLOCK, LANES), lambda b, i: (b * nqb + i, 0)),
            pl.BlockSpec((None, seq, KV_WIDTH), lambda b, i: (b, 0, 0), **resident),
            pl.BlockSpec((None, seq // KEY_TILE, KV_WIDTH, KEY_TILE), lambda b, i: (b, 0, 0, 0), **resident),
            pl.BlockSpec((None, seq, LANES), lambda b, i: (b, 0, 0), **resident),
        ],
        out_specs=pl.BlockSpec((Q_BLOCK, ATT_WIDTH), lambda b, i: (b * nqb + i, 0)),
        out_shape=jax.ShapeDtypeStruct((bsz * seq, ATT_WIDTH), F32),
        scratch_shapes=[
            pltpu.VMEM((seq // KEY_TILE, KEY_TILE, Q_BLOCK), F32),
            pltpu.VMEM((N_KV_HEADS, HEAD_DIM, 2 * Q_BLOCK), F32),
            pltpu.VMEM((N_KV_HEADS, KEY_TILE, 2 * Q_BLOCK), F32),
            pltpu.VMEM((N_KV_HEADS, KEY_TILE, 2 * Q_BLOCK), F32),
            pltpu.VMEM((N_KV_HEADS, KV_WIDTH, 2 * Q_BLOCK), BF16),
            pltpu.VMEM((N_IDX_HEADS // 2, LANES, 2 * Q_BLOCK), BF16),
        ],
        compiler_params=pltpu.CompilerParams(dimension_semantics=("arbitrary", "arbitrary"),
                                             vmem_limit_bytes=VMEM_LIMIT),
        name="prompt_attend",
    )(logit_bound, qt, qit, kiwi, kb, vt, kib)


def _sample_attend_body(k_top, n_pages, t_new,
                        pt_ref, qbd_ref, qih_ref, w_ref, knew_ref, vnew_ref, kinew_ref,
                        cik_hbm, ck_hbm, cv_hbm, o_ref,
                        ibuf, kbuf, vbuf, sem, sc_ref, m_ref, l_ref, acc_ref):
    b = pl.program_id(0)
    nb = pl.num_programs(0)
    slot = b % 2
    cp = SAMPLE_CHUNK_PAGES
    n_chunks = n_pages // cp
    width = cp * PAGE_SIZE

    def page_copies(bb, s, p):
        page = pt_ref[bb, p]
        return (pltpu.make_async_copy(cik_hbm.at[page], ibuf.at[s, p], sem.at[s, 0]),
                pltpu.make_async_copy(ck_hbm.at[page], kbuf.at[s, p], sem.at[s, 1]),
                pltpu.make_async_copy(cv_hbm.at[page], vbuf.at[s, p], sem.at[s, 2]))

    def start_fetch(bb, s):
        def body(p, _):
            for c in page_copies(bb, s, p):
                c.start()
            return 0
        lax.fori_loop(0, n_pages, body, 0)

    def wait_fetch(bb, s):
        def body(p, _):
            for c in page_copies(bb, s, p):
                c.wait()
            return 0
        lax.fori_loop(0, n_pages, body, 0)

    @pl.when(b == 0)
    def _():
        start_fetch(b, slot)

    @pl.when(b + 1 < nb)
    def _():
        start_fetch(b + 1, 1 - slot)

    wait_fetch(b, slot)

    def chunk(buf, c):
        return jnp.concatenate([buf[slot, c * cp + i] for i in range(cp)], axis=1).astype(BF16)

    def new_cols(a):
        padded = jnp.concatenate([a, jnp.zeros((LANES - a.shape[0], a.shape[1]), a.dtype)], axis=0)
        return padded.T.astype(BF16)

    qih = qih_ref[...]
    w = _tile_lanes(w_ref[...], width)

    def head_sum(ki_c):
        d = jnp.dot(qih, ki_c, preferred_element_type=F32)
        d = w[:, :ki_c.shape[1]] * jnp.maximum(d, 0.0)
        acc = d[0:t_new]
        for h in range(1, N_IDX_HEADS):
            acc = acc + d[h * t_new:(h + 1) * t_new]
        return acc

    def score_chunk(c, _):
        sc_ref[c] = head_sum(chunk(ibuf, c))
        return 0

    lax.fori_loop(0, n_chunks, score_chunk, 0)

    tq = lax.broadcasted_iota(I32, (t_new, LANES), 0)
    col = lax.broadcasted_iota(I32, (t_new, LANES), 1)
    sc_new = jnp.where((col < t_new) & (col <= tq), head_sum(new_cols(kinew_ref[...])[:D_IDX]), NEG_INF)
    sc_ref[n_chunks] = jnp.concatenate([sc_new, jnp.full((t_new, width - LANES), NEG_INF, F32)], axis=1)

    qpos = n_pages * PAGE_SIZE + lax.broadcasted_iota(I32, (t_new, 1), 0)
    k_row = jnp.minimum(k_top, qpos + 1)
    thr = _select_threshold(sc_ref, n_chunks + 1, k_row, key_axis=1, unroll=n_chunks + 1)

    _attend_init(m_ref, l_ref, acc_ref)
    qbd = qbd_ref[...]

    def attend(sc, k_c, v_c):
        bias = jnp.where(sc >= thr, 0.0, NEG_INF)
        s = jnp.dot(qbd, k_c, preferred_element_type=F32) + jnp.concatenate([bias] * N_HEADS, axis=0)
        p, m_new, l_new, alpha = _softmax_step(s, m_ref[...], l_ref[...])
        m_ref[...] = m_new
        l_ref[...] = l_new
        acc_ref[...] = (acc_ref[...] * _tile_lanes(alpha, KV_WIDTH)
                        + lax.dot_general(p.astype(BF16), v_c, _NT, preferred_element_type=F32))

    def attend_chunk(c, _):
        attend(sc_ref[c], chunk(kbuf, c), chunk(vbuf, c))
        return 0

    lax.fori_loop(0, n_chunks, attend_chunk, 0)
    attend(sc_ref[n_chunks][:, :LANES], new_cols(knew_ref[...]), new_cols(vnew_ref[...]))

    o_ref[...] = acc_ref[...] / _tile_lanes(l_ref[...], KV_WIDTH)


def _sample_attend(page_table, qbd, qih, w_rows, knew, vnew, kinew, cik_t, ck_t, cv_t, k_top):
    dbsz, n_pages = page_table.shape
    t_new = knew.shape[1]
    rows = N_HEADS * t_new
    past = n_pages * PAGE_SIZE
    assert n_pages % SAMPLE_CHUNK_PAGES == 0
    width = SAMPLE_CHUNK_PAGES * PAGE_SIZE
    n_chunks = n_pages // SAMPLE_CHUNK_PAGES

    def bmap(b, pt):
        return (b, 0, 0)

    grid_spec = pltpu.PrefetchScalarGridSpec(
        num_scalar_prefetch=1,
        grid=(dbsz,),
        in_specs=[
            pl.BlockSpec((None, rows, KV_WIDTH), bmap),
            pl.BlockSpec((None, rows, D_IDX), bmap),
            pl.BlockSpec((None, rows, LANES), bmap),
            pl.BlockSpec((None, t_new, KV_WIDTH), bmap),
            pl.BlockSpec((None, t_new, KV_WIDTH), bmap),
            pl.BlockSpec((None, t_new, LANES), bmap),
            pl.BlockSpec(memory_space=pl.ANY),
            pl.BlockSpec(memory_space=pl.ANY),
            pl.BlockSpec(memory_space=pl.ANY),
        ],
        out_specs=pl.BlockSpec((None, rows, KV_WIDTH), bmap),
        scratch_shapes=[
            pltpu.VMEM((2, n_pages, D_IDX, PAGE_SIZE), F32),
            pltpu.VMEM((2, n_pages, KV_WIDTH, PAGE_SIZE), F32),
            pltpu.VMEM((2, n_pages, KV_WIDTH, PAGE_SIZE), F32),
            pltpu.SemaphoreType.DMA((2, 3)),
            pltpu.VMEM((n_chunks + 1, t_new, width), F32),
            pltpu.VMEM((rows, LANES), F32),
            pltpu.VMEM((rows, LANES), F32),
            pltpu.VMEM((rows, KV_WIDTH), F32),
        ],
    )
    return pl.pallas_call(
        functools.partial(_sample_attend_body, k_top, n_pages, t_new),
        grid_spec=grid_spec,
        out_shape=jax.ShapeDtypeStruct((dbsz, rows, KV_WIDTH), F32),
        compiler_params=pltpu.CompilerParams(dimension_semantics=("arbitrary",),
                                             vmem_limit_bytes=VMEM_LIMIT),
        name="sample_attend",
    )(page_table, qbd, qih, w_rows, knew, vnew, kinew, cik_t, ck_t, cv_t)


def _finish_body(x_ref, o_ref, ga_ref, gm_ref, p_ref, wo_ref, pg_ref, wg_ref, wp_ref, y_ref):
    att = (o_ref[...] * ga_ref[...]).astype(BF16)
    r = (x_ref[...]
         + jnp.dot(att, wo_ref[:ATT_WIDTH, :], preferred_element_type=F32)
         + jnp.dot(gm_ref[...].astype(BF16), wo_ref[ATT_WIDTH:, :], preferred_element_type=F32))
    rn = r * lax.rsqrt(jnp.mean(r * r, axis=-1, keepdims=True) + EPS) * pg_ref[...]
    gate = jax.nn.sigmoid(jnp.dot(rn.astype(BF16), wg_ref[...], preferred_element_type=F32))
    y_ref[...] = r + gate * jnp.dot(p_ref[...].astype(BF16), wp_ref[...], preferred_element_type=F32)


def _finish(x2, o, ga, gm, p2, wo, pg, wg, wp):
    rows, d_model = x2.shape
    tm = min(ROW_TILE, rows)

    def rspec(a):
        return pl.BlockSpec((tm, a.shape[1]), lambda i: (i, 0))

    def cspec(a):
        return pl.BlockSpec(a.shape, lambda i: (0, 0))

    return pl.pallas_call(
        _finish_body,
        grid=(rows // tm,),
        in_specs=[rspec(x2), rspec(o), rspec(ga), rspec(gm), rspec(p2),
                  cspec(wo), cspec(pg), cspec(wg), cspec(wp)],
        out_specs=pl.BlockSpec((tm, d_model), lambda i: (i, 0)),
        out_shape=jax.ShapeDtypeStruct((rows, d_model), F32),
        compiler_params=pltpu.CompilerParams(dimension_semantics=("arbitrary",),
                                             vmem_limit_bytes=VMEM_LIMIT),
        name="finish",
    )(x2, o, ga, gm, p2, wo, pg, wg, wp)


def _rope_tables(pos):
    inv = jnp.power(ROPE_THETA, -jnp.arange(ROPE_HALF, dtype=F32) * 2.0 / ROPE_ROT)
    ang = pos.astype(F32)[:, None] * inv[None, :]
    cos, sin = jnp.cos(ang), jnp.sin(ang)
    n = pos.shape[0]
    one = jnp.ones((n, HEAD_DIM - ROPE_ROT), F32)
    zero8 = jnp.zeros((n, ROPE_HALF), F32)
    zero = jnp.zeros((n, HEAD_DIM - ROPE_ROT), F32)
    cos_h = jnp.concatenate([cos, cos, one], axis=1)
    sa_h = jnp.concatenate([-sin, zero8, zero], axis=1)
    sb_h = jnp.concatenate([zero8, sin, zero], axis=1)
    return tuple(jnp.concatenate([t, t], axis=1) for t in (cos_h, sa_h, sb_h))


def _pair_mix(ws):
    g = ws.shape[0]
    return ws.reshape(g // 2, 2, CHUNK, CHUNK).transpose(0, 2, 1, 3).reshape(g // 2, CHUNK, 2 * CHUNK).astype(BF16)


def _group_diag(q, lead):
    t = q.shape[-2]
    n = len(lead)
    q6 = q.reshape(*lead, t, N_KV_HEADS, 2, HEAD_DIM)
    q6 = jnp.moveaxis(q6, n, n + 2)
    q6 = q6.reshape(*lead, N_KV_HEADS, 2 * t, HEAD_DIM)
    eye = jnp.eye(N_KV_HEADS, dtype=q.dtype)
    qd = q6[..., None, :] * eye[:, None, :, None]
    return qd.reshape(*lead, N_KV_HEADS, 2 * t, KV_WIDTH)


def _heads_to_cols(x):
    b, n, nq, width = x.shape
    x6 = x.reshape(b, n, nq, width // (2 * HEAD_DIM), 2, HEAD_DIM)
    return x6.transpose(0, 1, 3, 5, 4, 2).reshape(b, n, width // (2 * HEAD_DIM), HEAD_DIM, 2 * nq)


def kernel(x_prompt, x_sample, cache_k, cache_v, cache_idx_k, page_table, p_prompt, p_sample,
           norm_in_g, w_in, q_norm_g, k_norm_g, ln_v_g, ln_v_b, w_s, b_s, w_out,
           ple_norm_g, w_ple_gate, w_ple_proj):
    depth = w_in.shape[0]
    assert depth == 1
    bsz, seq, d_model = x_prompt.shape
    dbsz, t_new, _ = x_sample.shape
    n_pages = page_table.shape[1]
    past = n_pages * PAGE_SIZE
    assert seq % KEY_TILE == 0 and CHUNK % t_new == 0 and (dbsz * t_new) % CHUNK == 0

    perm = np.concatenate([np.arange(0, _ORIG_KI), np.arange(_ORIG_GA, _ORIG_END),
                           np.arange(_ORIG_KI, _ORIG_GA)])
    w = jnp.pad(w_in[0][:, perm], ((0, 0), (0, C_END - _ORIG_END))).astype(BF16)
    seg = jnp.asarray(np.kron(np.eye(N_HEADS), np.full((HEAD_DIM, HEAD_DIM), 1.0 / HEAD_DIM)), BF16)
    weights = (norm_in_g[0][None, :], w, jnp.tile(q_norm_g[0], N_HEADS)[None, :],
               jnp.tile(k_norm_g[0], N_KV_HEADS)[None, :], ln_v_g[0][None, :], ln_v_b[0][None, :], seg)
    ws_tril = jnp.where(jnp.tril(jnp.ones((CHUNK, CHUNK), bool))[None], w_s[0], 0.0)
    bias_p = jnp.repeat(b_s[0].T, GM_WIDTH // GM_GROUPS, axis=1)
    reps = CHUNK // t_new
    ws_s = jnp.einsum("ab,gij->gaibj", jnp.eye(reps, dtype=F32),
                      ws_tril[:, :t_new, :t_new]).reshape(GM_GROUPS, CHUNK, CHUNK)
    bias_s = jnp.tile(bias_p[:t_new], (reps, 1))
    wo = w_out[0].astype(BF16)
    wg = w_ple_gate[0].astype(BF16)
    wp = w_ple_proj[0].astype(BF16)
    pg = ple_norm_g[0][None, :]

    xp = x_prompt.reshape(bsz * seq, d_model)
    (q, k, v, kb, qi, kiwi, kib, ga, gm, vt) = _project(
        xp, _rope_tables(jnp.arange(seq)), _pair_mix(ws_tril), bias_p, weights,
        emit_vn=False, emit_vt=True)
    nqb = seq // Q_BLOCK
    k_top_p = min(TOPK_MAX, seq // 4)
    qt = _heads_to_cols(q.reshape(bsz, nqb, Q_BLOCK, ATT_WIDTH))
    qit = _heads_to_cols(qi.reshape(bsz, nqb, Q_BLOCK, IDX_WIDTH))
    logit_bound = (HEAD_DIM ** 0.5 * LOG2_E * BOUND_MARGIN
                   * jnp.max(jnp.abs(q_norm_g[0])) * jnp.max(jnp.abs(k_norm_g[0]))).reshape(1)
    o = _prompt_attend(logit_bound, qt, qit, kiwi, kb.reshape(bsz, seq, KV_WIDTH),
                       vt.reshape(bsz, seq // KEY_TILE, KV_WIDTH, KEY_TILE),
                       kib.reshape(bsz, seq, LANES), k_top_p)
    y_prompt = _finish(xp, o, ga, gm, p_prompt[0].reshape(bsz * seq, -1), wo, pg, wg, wp)

    xs = x_sample.reshape(dbsz * t_new, d_model)
    pos_s = past + jnp.arange(min(ROW_TILE, dbsz * t_new)) % t_new
    (q, k_s, v_s, _, qi, kiwi_s, _, ga, gm, vn_s) = _project(
        xs, _rope_tables(pos_s), _pair_mix(ws_s), bias_s, weights, emit_vn=True, emit_vt=False)
    k_top_s = min(TOPK_MAX, (past + t_new) // 4)
    qbd = _group_diag(q.reshape(dbsz, t_new, ATT_WIDTH), (dbsz,)).reshape(dbsz, N_HEADS * t_new, KV_WIDTH)
    qih = qi.reshape(dbsz, t_new, N_IDX_HEADS, D_IDX).transpose(0, 2, 1, 3).reshape(
        dbsz, N_IDX_HEADS * t_new, D_IDX)
    kiwi3 = kiwi_s.reshape(dbsz, t_new, LANES)
    w_rows = kiwi3[:, :, D_IDX:D_IDX + N_IDX_HEADS].transpose(0, 2, 1).reshape(dbsz, N_IDX_HEADS * t_new, 1)
    w_rows = jnp.broadcast_to(w_rows, (dbsz, N_IDX_HEADS * t_new, LANES))
    acc = _sample_attend(page_table, qbd, qih, w_rows,
                         k_s.reshape(dbsz, t_new, KV_WIDTH), v_s.reshape(dbsz, t_new, KV_WIDTH), kiwi3,
                         cache_idx_k[0].transpose(0, 2, 1),
                         cache_k[0].transpose(0, 2, 3, 1).reshape(-1, KV_WIDTH, PAGE_SIZE),
                         cache_v[0].transpose(0, 2, 3, 1).reshape(-1, KV_WIDTH, PAGE_SIZE), k_top_s)
    acc = acc.reshape(dbsz, N_KV_HEADS, 2, t_new, N_KV_HEADS, HEAD_DIM)
    o_s = jnp.stack([acc[:, g, :, :, g, :] for g in range(N_KV_HEADS)], axis=1)
    o_s = o_s.transpose(0, 3, 1, 2, 4).reshape(dbsz * t_new, ATT_WIDTH)
    y_sample = _finish(xs, o_s, ga, gm, p_sample[0].reshape(dbsz * t_new, -1), wo, pg, wg, wp)

    return (y_prompt.reshape(bsz, seq, d_model),
            y_sample.reshape(dbsz, t_new, d_model),
            k.reshape(1, bsz, seq, N_KV_HEADS, HEAD_DIM),
            v.reshape(1, bsz, seq, N_KV_HEADS, HEAD_DIM),
            kiwi[:, :D_IDX].reshape(1, bsz, seq, D_IDX),
            k_s.reshape(1, dbsz, t_new, N_KV_HEADS, HEAD_DIM),
            v_s.reshape(1, dbsz, t_new, N_KV_HEADS, HEAD_DIM),
            kiwi_s[:, :D_IDX].reshape(1, dbsz, t_new, D_IDX),
            vn_s.reshape(1, dbsz, t_new, GM_WIDTH))
```

```python
import functools

import numpy as np
import jax
import jax.numpy as jnp
from jax import lax
from jax.experimental import pallas as pl
from jax.experimental.pallas import tpu as pltpu

F32 = jnp.float32
BF16 = jnp.bfloat16
I32 = jnp.int32

HEAD_DIM = 64
N_HEADS = 8
N_KV_HEADS = 4
N_IDX_HEADS = 8
D_IDX = 64
TOPK_MAX = 256
GM_GROUPS = 8
CHUNK = 128
PAGE_SIZE = 128
ROPE_THETA = 500000.0
ROPE_ROT = HEAD_DIM // 4
ROPE_HALF = ROPE_ROT // 2
EPS = 1e-6

ATT_WIDTH = N_HEADS * HEAD_DIM
KV_WIDTH = N_KV_HEADS * HEAD_DIM
IDX_WIDTH = N_IDX_HEADS * D_IDX
GM_WIDTH = GM_GROUPS * 64
LANES = 128
SUBLANES = 8
BF16_SUBLANES = 16
KEY_TILE = 256
Q_BLOCK = 128
ROW_TILE = 256
SAMPLE_CHUNK_PAGES = 8
VMEM_LIMIT = 56 * 1024 * 1024

C_Q, C_K, C_V, C_QI, C_GA, C_U, C_VM, C_GM, C_KIWI, C_END = (
    0, 512, 768, 1024, 1536, 2048, 2560, 3072, 3584, 3712)
_ORIG_KI, _ORIG_GA, _ORIG_END = 1536, 1608, 3656
LOG2_E = 1.4426950408889634
MAX_UNSHIFTED_LOG2 = 60.0
BOUND_MARGIN = 1.05
FIRST_SETTLE_CHECK = 20
SETTLE_CHECK_EVERY = 4
NEG_INF = float("-inf")
INT_MIN = -2 ** 31


def _head_rms(xh, seg, g):
    sq = xh * xh
    hi = sq.astype(BF16)
    lo = (sq - hi.astype(F32)).astype(BF16)
    ms = (jnp.dot(hi, seg, preferred_element_type=F32)
          + jnp.dot(lo, seg, preferred_element_type=F32))
    return xh * lax.rsqrt(ms + EPS) * g


def _rope(xh, cos, sa, sb):
    w = xh.shape[-1]
    return xh * cos + pltpu.roll(xh, w - ROPE_HALF, 1) * sa + pltpu.roll(xh, ROPE_HALF, 1) * sb


def _tile_lanes(t, width):
    reps = width // t.shape[-1]
    return t if reps == 1 else jnp.concatenate([t] * reps, axis=1)


def _project_body(emit_vn, emit_vt, x_ref, ng_ref, w_ref, qg_ref, kg_ref, lng_ref, lnb_ref, mixw_ref,
                  bias_ref, cos_ref, sa_ref, sb_ref, seg_ref, *outs):
    q_o, k_o, v_o, kb_o, qi_o, kiwi_o, kib_o, ga_o, gm_o = outs[:9]
    extra = list(outs[9:])
    vn_o = extra.pop(0) if emit_vn else None
    vt_o, qt_o, qit_o = extra if emit_vt else (None, None, None)

    def store_cols(dst, xq):
        for blk in range(xq.shape[0] // Q_BLOCK):
            xt = xq[blk * Q_BLOCK:(blk + 1) * Q_BLOCK].T.astype(BF16)
            for p in range(xq.shape[1] // (2 * HEAD_DIM)):
                lo = 2 * p * HEAD_DIM
                dst[blk, p] = jnp.concatenate([xt[lo:lo + HEAD_DIM], xt[lo + HEAD_DIM:lo + 2 * HEAD_DIM]],
                                              axis=1)
    x = x_ref[...]
    h = x * lax.rsqrt(jnp.mean(x * x, axis=-1, keepdims=True) + EPS) * ng_ref[...]
    hb = h.astype(BF16)

    def proj(c0, c1):
        return jnp.dot(hb, w_ref[:, c0:c1], preferred_element_type=F32)

    cos, sa, sb = cos_ref[...], sa_ref[...], sb_ref[...]
    seg = seg_ref[...]

    q = _head_rms(proj(C_Q, C_K), seg, qg_ref[...])
    q = _rope(q, _tile_lanes(cos, ATT_WIDTH), _tile_lanes(sa, ATT_WIDTH), _tile_lanes(sb, ATT_WIDTH))
    q = q * (HEAD_DIM ** -0.5 * LOG2_E)
    q_o[...] = q.astype(BF16)
    if emit_vt:
        store_cols(qt_o, q)

    k = _head_rms(proj(C_K, C_V), seg[:KV_WIDTH, :KV_WIDTH], kg_ref[...])
    k = _rope(k, _tile_lanes(cos, KV_WIDTH), _tile_lanes(sa, KV_WIDTH), _tile_lanes(sb, KV_WIDTH))
    k_o[...] = k
    kb_o[...] = k.astype(BF16)

    v = proj(C_V, C_QI)
    v_o[...] = v
    if emit_vt:
        vt_o[...] = v.T.astype(BF16)

    qi = _rope(proj(C_QI, C_GA), _tile_lanes(cos, IDX_WIDTH), _tile_lanes(sa, IDX_WIDTH),
               _tile_lanes(sb, IDX_WIDTH))
    qi_o[...] = qi.astype(BF16)
    if emit_vt:
        store_cols(qit_o, qi)

    lane = lax.broadcasted_iota(I32, cos.shape, 1)
    is_ki = lane < D_IDX
    wi_scale = jnp.where(lane < D_IDX + N_IDX_HEADS, IDX_WIDTH ** -0.5, 1.0)
    kiwi = _rope(proj(C_KIWI, C_END), jnp.where(is_ki, cos, wi_scale),
                 jnp.where(is_ki, sa, 0.0), jnp.where(is_ki, sb, 0.0))
    kiwi_o[...] = kiwi
    kib_o[...] = kiwi.astype(BF16)

    ga_o[...] = jax.nn.silu(proj(C_GA, C_U))

    vmg = jax.nn.gelu(proj(C_VM, C_GM))
    xc = vmg - jnp.mean(vmg, axis=-1, keepdims=True)
    vn = xc * lax.rsqrt(jnp.mean(xc * xc, axis=-1, keepdims=True) + EPS) * lng_ref[...] + lnb_ref[...]
    if emit_vn:
        vn_o[...] = vn

    lane_c = lax.broadcasted_iota(I32, (CHUNK, LANES), 1)
    rows = x.shape[0]
    chunks = []
    for c in range(rows // CHUNK):
        pieces = []
        for p in range(GM_GROUPS // 2):
            t = vn[c * CHUNK:(c + 1) * CHUNK, p * LANES:(p + 1) * LANES]
            rhs = jnp.concatenate([jnp.where(lane_c < 64, t, 0.0), jnp.where(lane_c >= 64, t, 0.0)],
                                  axis=0).astype(BF16)
            pieces.append(jnp.dot(mixw_ref[p], rhs, preferred_element_type=F32))
        chunks.append(jnp.concatenate(pieces, axis=1) + bias_ref[...])
    s = jnp.concatenate(chunks, axis=0)
    gm_o[...] = jax.nn.gelu(proj(C_U, C_VM)) * s * jax.nn.silu(proj(C_GM, C_KIWI))


def _project(x2, pos_tables, mixw, bias, weights, emit_vn, emit_vt):
    rows = x2.shape[0]
    tm = min(ROW_TILE, rows)
    assert rows % tm == 0 and tm % CHUNK == 0
    ng, w, qg, kg, lng, lnb, seg = weights
    cos, sa, sb = pos_tables
    assert cos.shape[0] % tm == 0
    n_pos_blocks = cos.shape[0] // tm

    def row_map(i):
        return (i, 0)

    def pos_map(i):
        return (i % n_pos_blocks, 0)

    def const(i):
        return (0, 0)

    def rspec(width):
        return pl.BlockSpec((tm, width), row_map)

    def cspec(a):
        if a.ndim == 3:
            return pl.BlockSpec(a.shape, lambda i: (0, 0, 0))
        return pl.BlockSpec(a.shape, const)

    out_shapes = [
        jax.ShapeDtypeStruct((rows, ATT_WIDTH), BF16),
        jax.ShapeDtypeStruct((rows, KV_WIDTH), F32),
        jax.ShapeDtypeStruct((rows, KV_WIDTH), F32),
        jax.ShapeDtypeStruct((rows, KV_WIDTH), BF16),
        jax.ShapeDtypeStruct((rows, IDX_WIDTH), BF16),
        jax.ShapeDtypeStruct((rows, LANES), F32),
        jax.ShapeDtypeStruct((rows, LANES), BF16),
        jax.ShapeDtypeStruct((rows, ATT_WIDTH), F32),
        jax.ShapeDtypeStruct((rows, GM_WIDTH), F32),
    ]
    if emit_vn:
        out_shapes.append(jax.ShapeDtypeStruct((rows, GM_WIDTH), F32))
    out_specs = [rspec(s.shape[1]) for s in out_shapes]
    if emit_vt:
        assert tm == KEY_TILE
        out_shapes.append(jax.ShapeDtypeStruct((rows // tm, KV_WIDTH, tm), BF16))
        out_specs.append(pl.BlockSpec((None, KV_WIDTH, tm), lambda i: (i, 0, 0)))
        for width in (ATT_WIDTH, IDX_WIDTH):
            pairs = width // (2 * HEAD_DIM)
            out_shapes.append(jax.ShapeDtypeStruct((rows // Q_BLOCK, pairs, HEAD_DIM, 2 * Q_BLOCK), BF16))
            out_specs.append(pl.BlockSpec((tm // Q_BLOCK, pairs, HEAD_DIM, 2 * Q_BLOCK),
                                          lambda i: (i, 0, 0, 0)))
    in_specs = [rspec(x2.shape[1]), cspec(ng), cspec(w), cspec(qg), cspec(kg), cspec(lng), cspec(lnb),
                cspec(mixw), cspec(bias),
                pl.BlockSpec((tm, LANES), pos_map), pl.BlockSpec((tm, LANES), pos_map),
                pl.BlockSpec((tm, LANES), pos_map), cspec(seg)]
    return pl.pallas_call(
        functools.partial(_project_body, emit_vn, emit_vt),
        grid=(rows // tm,),
        in_specs=in_specs,
        out_specs=out_specs,
        out_shape=out_shapes,
        compiler_params=pltpu.CompilerParams(dimension_semantics=("arbitrary",),
                                             vmem_limit_bytes=VMEM_LIMIT),
        name="project",
    )(x2, ng, w, qg, kg, lng, lnb, mixw, bias, cos, sa, sb, seg)


def _key_to_float(u):
    bits = jnp.where(u < 0, u ^ I32(INT_MIN), ~u)
    return lax.bitcast_convert_type(bits, F32)


def _count(sc_ref, n_tiles, key_axis, pred, unroll=1, tail=0):
    a, b = sc_ref.shape[1], sc_ref.shape[2]

    def fold(ind):
        if key_axis == 1:
            return ind
        parts = [ind[i * SUBLANES:(i + 1) * SUBLANES] for i in range(a // SUBLANES)]
        while len(parts) > 1:
            parts = [parts[i] + parts[i + 1] for i in range(0, len(parts), 2)]
        return parts[0]

    def tiles(j0, n, c):
        for u in range(n):
            c = c + fold(jnp.where(pred(sc_ref[j0 + u], j0 + u), 1.0, 0.0))
        return c

    init = jnp.zeros((a, b) if key_axis == 1 else (SUBLANES, b), F32)
    n_steps = n_tiles // unroll
    c = lax.fori_loop(0, n_steps, lambda i, c: tiles(i * unroll, unroll, c), init)
    if tail:
        c = lax.cond(n_tiles - n_steps * unroll == tail, lambda c: tiles(n_tiles - tail, tail, c),
                     lambda c: c, c)
    return jnp.sum(c, axis=key_axis, keepdims=True)


def _select_threshold(sc_ref, n_tiles, k_row, key_axis, unroll=1, tail=0):
    k_f = k_row.astype(F32)

    def bit_step(i, carry):
        res, cnt_res = carry
        cand = res | lax.shift_left(I32(1), I32(31) - i)
        t = _key_to_float(cand)
        cnt = _count(sc_ref, n_tiles, key_axis, lambda x, j: x >= t, unroll, tail)
        ok = cnt >= k_f
        return jnp.where(ok, cand, res), jnp.where(ok, cnt, cnt_res)

    state = (jnp.zeros(k_row.shape, I32), jnp.zeros(k_row.shape, F32))
    state = lax.fori_loop(0, FIRST_SETTLE_CHECK, bit_step, state)
    for lo in range(FIRST_SETTLE_CHECK, 32, SETTLE_CHECK_EVERY):
        settled = jnp.min(jnp.where(state[1] == k_f, 1.0, 0.0)) > 0.0
        state = lax.cond(settled, lambda s: s,
                         lambda s, lo=lo: lax.fori_loop(lo, lo + SETTLE_CHECK_EVERY, bit_step, s), state)
    res, cnt_res = state
    thr = _key_to_float(res)
    _drop_surplus_ties(sc_ref, n_tiles, thr, k_f, cnt_res, key_axis)
    return thr


def _drop_surplus_ties(sc_ref, n_tiles, thr, k_f, cnt_ge, key_axis):
    t = sc_ref.shape[1 + key_axis]
    surplus = jnp.max(cnt_ge - k_f)

    @pl.when(surplus > 0.0)
    def _():
        n_take = k_f - _count(sc_ref, n_tiles, key_axis, lambda x, j: x > thr)
        r = lax.broadcasted_iota(I32, (t, t), 0)
        c = lax.broadcasted_iota(I32, (t, t), 1)
        tri = jnp.where((r >= c) if key_axis == 0 else (r <= c), 1.0, 0.0).astype(BF16)

        def body(j, before):
            x = sc_ref[j]
            tied = jnp.where(x == thr, 1.0, 0.0)
            if key_axis == 0:
                upto = jnp.dot(tri, tied.astype(BF16), preferred_element_type=F32)
                total = upto[t - 1:t]
            else:
                upto = jnp.dot(tied.astype(BF16), tri, preferred_element_type=F32)
                total = upto[:, t - 1:t]
            rank = tied * (before + upto)
            sc_ref[j] = jnp.where(rank > n_take, NEG_INF, x)
            return before + total

        lax.fori_loop(0, n_tiles, body, jnp.zeros(thr.shape, F32))


def _attend_init(m_ref, l_ref, acc_ref):
    m_ref[...] = jnp.full(m_ref.shape, NEG_INF, F32)
    l_ref[...] = jnp.zeros(l_ref.shape, F32)
    acc_ref[...] = jnp.zeros(acc_ref.shape, F32)


def _softmax_step(s, m_prev, l_prev):
    m_new = jnp.maximum(m_prev, jnp.max(s, axis=1, keepdims=True))
    m_safe = jnp.where(m_new == NEG_INF, 0.0, m_new)
    alpha = jnp.exp2(m_prev - m_safe)
    p = jnp.exp2(s - _tile_lanes(m_safe, s.shape[1]))
    l_new = alpha * l_prev + jnp.sum(p, axis=1, keepdims=True)
    return p, m_new, l_new, alpha


_NT = (((1,), (1,)), ((), ()))


def _two_stage(n_tiles, produce, consume, carry, buf_a, buf_b):
    produce(0, buf_a)

    def pair(j0, c, last):
        produce(j0 + 1, buf_b)
        c = consume(j0, buf_a, c)
        if not last:
            produce(jnp.minimum(j0 + 2, n_tiles - 1), buf_a)
        return consume(j0 + 1, buf_b, c)

    n_quads = n_tiles // 4
    carry = lax.fori_loop(0, n_quads, lambda i, c: pair(4 * i + 2, pair(4 * i, c, False), False), carry)
    return lax.cond(n_tiles - 4 * n_quads == 2, lambda c: pair(n_tiles - 2, c, True), lambda c: c, carry)


def _prompt_attend_body(k_top, bound_ref, qt_ref, qit_ref, kiwi_ref, kb_ref, vt_ref,
                        kib_ref, o_ref, sc_ref, acc_ref, buf_a, buf_b, qbdt_ref, qiht_ref):
    qb = pl.program_id(1)

    @pl.when((pl.program_id(0) == 0) & (qb == 0))
    def _():
        qbdt_ref[...] = jnp.zeros(qbdt_ref.shape, BF16)
        qiht_ref[...] = jnp.zeros(qiht_ref.shape, BF16)

    for g in range(N_KV_HEADS):
        qbdt_ref[g, g * HEAD_DIM:(g + 1) * HEAD_DIM, :] = qt_ref[g]
        qiht_ref[g, :D_IDX, :] = qit_ref[g]
    n_tiles = 2 * ((qb * Q_BLOCK + Q_BLOCK + 2 * KEY_TILE - 1) // (2 * KEY_TILE))
    qpos = qb * Q_BLOCK + lax.broadcasted_iota(I32, (1, Q_BLOCK), 1)

    w_t = kiwi_ref[...].T[D_IDX:D_IDX + N_IDX_HEADS, :]
    w_pairs = [jnp.concatenate([w_t[2 * p:2 * p + 1], w_t[2 * p + 1:2 * p + 2]], axis=1)
               for p in range(N_IDX_HEADS // 2)]
    krow = lax.broadcasted_iota(I32, (KEY_TILE, Q_BLOCK), 0)

    def score_dots(j, buf):
        start = pl.multiple_of(j * KEY_TILE, KEY_TILE)
        ki_t = kib_ref[pl.ds(start, KEY_TILE), :]
        for p in range(N_IDX_HEADS // 2):
            buf[p] = jnp.dot(ki_t, qiht_ref[p], preferred_element_type=F32)

    def score_sum(j, buf, c):
        acc = jnp.zeros((KEY_TILE, Q_BLOCK), F32)
        for p in range(N_IDX_HEADS // 2):
            d = jnp.maximum(buf[p], 0.0) * w_pairs[p]
            acc = acc + d[:, :Q_BLOCK] + d[:, Q_BLOCK:]
        sc_ref[j] = jnp.where(j * KEY_TILE + krow <= qpos, acc, NEG_INF)
        return c

    _two_stage(n_tiles, score_dots, score_sum, 0, buf_a, buf_b)

    k_row = jnp.minimum(k_top, qpos + 1)
    thr = _select_threshold(sc_ref, n_tiles, k_row, key_axis=0, unroll=4, tail=2)
    thr2 = jnp.concatenate([thr, thr], axis=1)

    acc_ref[...] = jnp.zeros(acc_ref.shape, F32)
    m0 = tuple(jnp.full((1, 2 * Q_BLOCK), NEG_INF, F32) for _ in range(N_KV_HEADS))
    l0 = tuple(jnp.zeros((1, 2 * Q_BLOCK), F32) for _ in range(N_KV_HEADS))

    ones_rows = jnp.ones((BF16_SUBLANES, KEY_TILE), BF16)

    def logits(j, buf):
        start = pl.multiple_of(j * KEY_TILE, KEY_TILE)
        k_t = kb_ref[pl.ds(start, KEY_TILE), :]
        for g in range(N_KV_HEADS):
            buf[g] = jnp.dot(k_t, qbdt_ref[g], preferred_element_type=F32)

    def weighted_values(v_t, g, p):
        lhs = jnp.concatenate([v_t[g * HEAD_DIM:(g + 1) * HEAD_DIM, :], ones_rows], axis=0)
        pv = jnp.dot(lhs, p, preferred_element_type=F32)
        return pv[:HEAD_DIM], pv[HEAD_DIM:HEAD_DIM + 1]

    def softmax_pv(j, buf, carry):
        ms, ls = carry
        v_t = vt_ref[j]
        sc = sc_ref[j]
        keep = jnp.concatenate([sc, sc], axis=1) >= thr2
        new_m, new_l = [], []
        for g in range(N_KV_HEADS):
            s = jnp.where(keep, buf[g], NEG_INF)
            m_new = jnp.maximum(ms[g], jnp.max(s, axis=0, keepdims=True))
            m_safe = jnp.where(m_new == NEG_INF, 0.0, m_new)
            alpha = jnp.exp2(ms[g] - m_safe)
            pv, psum = weighted_values(v_t, g, jnp.exp2(s - m_safe).astype(BF16))
            new_m.append(m_new)
            new_l.append(alpha * ls[g] + psum)
            acc_ref[g] = acc_ref[g] * alpha + pv
        return tuple(new_m), tuple(new_l)

    def exp_pv(j, buf, ls):
        v_t = vt_ref[j]
        sc = sc_ref[j]
        keep = jnp.concatenate([sc, sc], axis=1) >= thr2
        new_l = []
        for g in range(N_KV_HEADS):
            pv, psum = weighted_values(v_t, g, jnp.exp2(jnp.where(keep, buf[g], NEG_INF)).astype(BF16))
            new_l.append(ls[g] + psum)
            acc_ref[g] = acc_ref[g] + pv
        return tuple(new_l)

    def write_out(ls):
        blocks = []
        for g in range(N_KV_HEADS):
            og = acc_ref[g] / ls[g]
            blocks += [og[:, :Q_BLOCK], og[:, Q_BLOCK:]]
        o_ref[...] = jnp.concatenate(blocks, axis=0).T

    bounded = bound_ref[0] < MAX_UNSHIFTED_LOG2

    @pl.when(bounded)
    def _():
        write_out(_two_stage(n_tiles, logits, exp_pv, l0, buf_a, buf_b))

    @pl.when(jnp.logical_not(bounded))
    def _():
        write_out(_two_stage(n_tiles, logits, softmax_pv, (m0, l0), buf_a, buf_b)[1])


def _prompt_attend(logit_bound, qt, qit, kiwi, kb, vt, kib, k_top):
    bsz, nqb = qt.shape[:2]
    seq = kb.shape[1]
    assert N_IDX_HEADS // 2 == N_KV_HEADS
    resident = dict(pipeline_mode=pl.Buffered(1))
    return pl.pallas_call(
        functools.partial(_prompt_attend_body, k_top),
        grid=(bsz, nqb),
        in_specs=[
            pl.BlockSpec(memory_space=pltpu.SMEM),
            pl.BlockSpec((None, None, N_KV_HEADS, HEAD_DIM, 2 * Q_BLOCK), lambda b, i: (b, i, 0, 0, 0)),
            pl.BlockSpec((None, None, N_IDX_HEADS // 2, D_IDX, 2 * Q_BLOCK), lambda b, i: (b, i, 0, 0, 0)),
            pl.BlockSpec((Q_BLOCK, LANES), lambda b, i: (b * nqb + i, 0)),
            pl.BlockSpec((None, seq, KV_WIDTH), lambda b, i: (b, 0, 0), **resident),
            pl.BlockSpec((None, seq // KEY_TILE, KV_WIDTH, KEY_TILE), lambda b, i: (b, 0, 0, 0), **resident),
            pl.BlockSpec((None, seq, LANES), lambda b, i: (b, 0, 0), **resident),
        ],
        out_specs=pl.BlockSpec((Q_BLOCK, ATT_WIDTH), lambda b, i: (b * nqb + i, 0)),
        out_shape=jax.ShapeDtypeStruct((bsz * seq, ATT_WIDTH), F32),
        scratch_shapes=[
            pltpu.VMEM((seq // KEY_TILE, KEY_TILE, Q_BLOCK), F32),
            pltpu.VMEM((N_KV_HEADS, HEAD_DIM, 2 * Q_BLOCK), F32),
            pltpu.VMEM((N_KV_HEADS, KEY_TILE, 2 * Q_BLOCK), F32),
            pltpu.VMEM((N_KV_HEADS, KEY_TILE, 2 * Q_BLOCK), F32),
            pltpu.VMEM((N_KV_HEADS, KV_WIDTH, 2 * Q_BLOCK), BF16),
            pltpu.VMEM((N_IDX_HEADS // 2, LANES, 2 * Q_BLOCK), BF16),
        ],
        compiler_params=pltpu.CompilerParams(dimension_semantics=("arbitrary", "arbitrary"),
                                             vmem_limit_bytes=VMEM_LIMIT),
        name="prompt_attend",
    )(logit_bound, qt, qit, kiwi, kb, vt, kib)


def _sample_attend_body(k_top, n_pages, t_new,
                        pt_ref, qbd_ref, qih_ref, w_ref, knew_ref, vnew_ref, kinew_ref,
                        cik_hbm, ck_hbm, cv_hbm, o_ref,
                        ibuf, kbuf, vbuf, sem, sc_ref, m_ref, l_ref, acc_ref):
    b = pl.program_id(0)
    nb = pl.num_programs(0)
    slot = b % 2
    cp = SAMPLE_CHUNK_PAGES
    n_chunks = n_pages // cp
    width = cp * PAGE_SIZE

    def page_copies(bb, s, p):
        page = pt_ref[bb, p]
        return (pltpu.make_async_copy(cik_hbm.at[page], ibuf.at[s, p], sem.at[s, 0]),
                pltpu.make_async_copy(ck_hbm.at[page], kbuf.at[s, p], sem.at[s, 1]),
                pltpu.make_async_copy(cv_hbm.at[page], vbuf.at[s, p], sem.at[s, 2]))

    def start_fetch(bb, s):
        def body(p, _):
            for c in page_copies(bb, s, p):
                c.start()
            return 0
        lax.fori_loop(0, n_pages, body, 0)

    def wait_fetch(bb, s):
        def body(p, _):
            for c in page_copies(bb, s, p):
                c.wait()
            return 0
        lax.fori_loop(0, n_pages, body, 0)

    @pl.when(b == 0)
    def _():
        start_fetch(b, slot)

    @pl.when(b + 1 < nb)
    def _():
        start_fetch(b + 1, 1 - slot)

    wait_fetch(b, slot)

    def chunk(buf, c):
        return jnp.concatenate([buf[slot, c * cp + i] for i in range(cp)], axis=1).astype(BF16)

    def new_cols(a):
        padded = jnp.concatenate([a, jnp.zeros((LANES - a.shape[0], a.shape[1]), a.dtype)], axis=0)
        return padded.T.astype(BF16)

    qih = qih_ref[...]
    w = _tile_lanes(w_ref[...], width)

    def head_sum(ki_c):
        d = jnp.dot(qih, ki_c, preferred_element_type=F32)
        d = w[:, :ki_c.shape[1]] * jnp.maximum(d, 0.0)
        acc = d[0:t_new]
        for h in range(1, N_IDX_HEADS):
            acc = acc + d[h * t_new:(h + 1) * t_new]
        return acc

    def score_chunk(c, _):
        sc_ref[c] = head_sum(chunk(ibuf, c))
        return 0

    lax.fori_loop(0, n_chunks, score_chunk, 0)

    tq = lax.broadcasted_iota(I32, (t_new, LANES), 0)
    col = lax.broadcasted_iota(I32, (t_new, LANES), 1)
    sc_new = jnp.where((col < t_new) & (col <= tq), head_sum(new_cols(kinew_ref[...])[:D_IDX]), NEG_INF)
    sc_ref[n_chunks] = jnp.concatenate([sc_new, jnp.full((t_new, width - LANES), NEG_INF, F32)], axis=1)

    qpos = n_pages * PAGE_SIZE + lax.broadcasted_iota(I32, (t_new, 1), 0)
    k_row = jnp.minimum(k_top, qpos + 1)
    thr = _select_threshold(sc_ref, n_chunks + 1, k_row, key_axis=1, unroll=n_chunks + 1)

    _attend_init(m_ref, l_ref, acc_ref)
    qbd = qbd_ref[...]

    def attend(sc, k_c, v_c):
        bias = jnp.where(sc >= thr, 0.0, NEG_INF)
        s = jnp.dot(qbd, k_c, preferred_element_type=F32) + jnp.concatenate([bias] * N_HEADS, axis=0)
        p, m_new, l_new, alpha = _softmax_step(s, m_ref[...], l_ref[...])
        m_ref[...] = m_new
        l_ref[...] = l_new
        acc_ref[...] = (acc_ref[...] * _tile_lanes(alpha, KV_WIDTH)
                        + lax.dot_general(p.astype(BF16), v_c, _NT, preferred_element_type=F32))

    def attend_chunk(c, _):
        attend(sc_ref[c], chunk(kbuf, c), chunk(vbuf, c))
        return 0

    lax.fori_loop(0, n_chunks, attend_chunk, 0)
    attend(sc_ref[n_chunks][:, :LANES], new_cols(knew_ref[...]), new_cols(vnew_ref[...]))

    o_ref[...] = acc_ref[...] / _tile_lanes(l_ref[...], KV_WIDTH)


def _sample_attend(page_table, qbd, qih, w_rows, knew, vnew, kinew, cik_t, ck_t, cv_t, k_top):
    dbsz, n_pages = page_table.shape
    t_new = knew.shape[1]
    rows = N_HEADS * t_new
    past = n_pages * PAGE_SIZE
    assert n_pages % SAMPLE_CHUNK_PAGES == 0
    width = SAMPLE_CHUNK_PAGES * PAGE_SIZE
    n_chunks = n_pages // SAMPLE_CHUNK_PAGES

    def bmap(b, pt):
        return (b, 0, 0)

    grid_spec = pltpu.PrefetchScalarGridSpec(
        num_scalar_prefetch=1,
        grid=(dbsz,),
        in_specs=[
            pl.BlockSpec((None, rows, KV_WIDTH), bmap),
            pl.BlockSpec((None, rows, D_IDX), bmap),
            pl.BlockSpec((None, rows, LANES), bmap),
            pl.BlockSpec((None, t_new, KV_WIDTH), bmap),
            pl.BlockSpec((None, t_new, KV_WIDTH), bmap),
            pl.BlockSpec((None, t_new, LANES), bmap),
            pl.BlockSpec(memory_space=pl.ANY),
            pl.BlockSpec(memory_space=pl.ANY),
            pl.BlockSpec(memory_space=pl.ANY),
        ],
        out_specs=pl.BlockSpec((None, rows, KV_WIDTH), bmap),
        scratch_shapes=[
            pltpu.VMEM((2, n_pages, D_IDX, PAGE_SIZE), F32),
            pltpu.VMEM((2, n_pages, KV_WIDTH, PAGE_SIZE), F32),
            pltpu.VMEM((2, n_pages, KV_WIDTH, PAGE_SIZE), F32),
            pltpu.SemaphoreType.DMA((2, 3)),
            pltpu.VMEM((n_chunks + 1, t_new, width), F32),
            pltpu.VMEM((rows, LANES), F32),
            pltpu.VMEM((rows, LANES), F32),
            pltpu.VMEM((rows, KV_WIDTH), F32),
        ],
    )
    return pl.pallas_call(
        functools.partial(_sample_attend_body, k_top, n_pages, t_new),
        grid_spec=grid_spec,
        out_shape=jax.ShapeDtypeStruct((dbsz, rows, KV_WIDTH), F32),
        compiler_params=pltpu.CompilerParams(dimension_semantics=("arbitrary",),
                                             vmem_limit_bytes=VMEM_LIMIT),
        name="sample_attend",
    )(page_table, qbd, qih, w_rows, knew, vnew, kinew, cik_t, ck_t, cv_t)


def _finish_body(x_ref, o_ref, ga_ref, gm_ref, p_ref, wo_ref, pg_ref, wg_ref, wp_ref, y_ref):
    att = (o_ref[...] * ga_ref[...]).astype(BF16)
    r = (x_ref[...]
         + jnp.dot(att, wo_ref[:ATT_WIDTH, :], preferred_element_type=F32)
         + jnp.dot(gm_ref[...].astype(BF16), wo_ref[ATT_WIDTH:, :], preferred_element_type=F32))
    rn = r * lax.rsqrt(jnp.mean(r * r, axis=-1, keepdims=True) + EPS) * pg_ref[...]
    gate = jax.nn.sigmoid(jnp.dot(rn.astype(BF16), wg_ref[...], preferred_element_type=F32))
    y_ref[...] = r + gate * jnp.dot(p_ref[...].astype(BF16), wp_ref[...], preferred_element_type=F32)


def _finish(x2, o, ga, gm, p2, wo, pg, wg, wp):
    rows, d_model = x2.shape
    tm = min(ROW_TILE, rows)

    def rspec(a):
        return pl.BlockSpec((tm, a.shape[1]), lambda i: (i, 0))

    def cspec(a):
        return pl.BlockSpec(a.shape, lambda i: (0, 0))

    return pl.pallas_call(
        _finish_body,
        grid=(rows // tm,),
        in_specs=[rspec(x2), rspec(o), rspec(ga), rspec(gm), rspec(p2),
                  cspec(wo), cspec(pg), cspec(wg), cspec(wp)],
        out_specs=pl.BlockSpec((tm, d_model), lambda i: (i, 0)),
        out_shape=jax.ShapeDtypeStruct((rows, d_model), F32),
        compiler_params=pltpu.CompilerParams(dimension_semantics=("arbitrary",),
                                             vmem_limit_bytes=VMEM_LIMIT),
        name="finish",
    )(x2, o, ga, gm, p2, wo, pg, wg, wp)


def _rope_tables(pos):
    inv = jnp.power(ROPE_THETA, -jnp.arange(ROPE_HALF, dtype=F32) * 2.0 / ROPE_ROT)
    ang = pos.astype(F32)[:, None] * inv[None, :]
    cos, sin = jnp.cos(ang), jnp.sin(ang)
    n = pos.shape[0]
    one = jnp.ones((n, HEAD_DIM - ROPE_ROT), F32)
    zero8 = jnp.zeros((n, ROPE_HALF), F32)
    zero = jnp.zeros((n, HEAD_DIM - ROPE_ROT), F32)
    cos_h = jnp.concatenate([cos, cos, one], axis=1)
    sa_h = jnp.concatenate([-sin, zero8, zero], axis=1)
    sb_h = jnp.concatenate([zero8, sin, zero], axis=1)
    return tuple(jnp.concatenate([t, t], axis=1) for t in (cos_h, sa_h, sb_h))


def _pair_mix(ws):
    g = ws.shape[0]
    return ws.reshape(g // 2, 2, CHUNK, CHUNK).transpose(0, 2, 1, 3).reshape(g // 2, CHUNK, 2 * CHUNK).astype(BF16)


def _group_diag(q, lead):
    t = q.shape[-2]
    n = len(lead)
    q6 = q.reshape(*lead, t, N_KV_HEADS, 2, HEAD_DIM)
    q6 = jnp.moveaxis(q6, n, n + 2)
    q6 = q6.reshape(*lead, N_KV_HEADS, 2 * t, HEAD_DIM)
    eye = jnp.eye(N_KV_HEADS, dtype=q.dtype)
    qd = q6[..., None, :] * eye[:, None, :, None]
    return qd.reshape(*lead, N_KV_HEADS, 2 * t, KV_WIDTH)


def kernel(x_prompt, x_sample, cache_k, cache_v, cache_idx_k, page_table, p_prompt, p_sample,
           norm_in_g, w_in, q_norm_g, k_norm_g, ln_v_g, ln_v_b, w_s, b_s, w_out,
           ple_norm_g, w_ple_gate, w_ple_proj):
    depth = w_in.shape[0]
    assert depth == 1
    bsz, seq, d_model = x_prompt.shape
    dbsz, t_new, _ = x_sample.shape
    n_pages = page_table.shape[1]
    past = n_pages * PAGE_SIZE
    assert seq % KEY_TILE == 0 and CHUNK % t_new == 0 and (dbsz * t_new) % CHUNK == 0

    perm = np.concatenate([np.arange(0, _ORIG_KI), np.arange(_ORIG_GA, _ORIG_END),
                           np.arange(_ORIG_KI, _ORIG_GA)])
    w = jnp.pad(w_in[0][:, perm], ((0, 0), (0, C_END - _ORIG_END))).astype(BF16)
    seg = jnp.asarray(np.kron(np.eye(N_HEADS), np.full((HEAD_DIM, HEAD_DIM), 1.0 / HEAD_DIM)), BF16)
    weights = (norm_in_g[0][None, :], w, jnp.tile(q_norm_g[0], N_HEADS)[None, :],
               jnp.tile(k_norm_g[0], N_KV_HEADS)[None, :], ln_v_g[0][None, :], ln_v_b[0][None, :], seg)
    ws_tril = jnp.where(jnp.tril(jnp.ones((CHUNK, CHUNK), bool))[None], w_s[0], 0.0)
    bias_p = jnp.repeat(b_s[0].T, GM_WIDTH // GM_GROUPS, axis=1)
    reps = CHUNK // t_new
    ws_s = jnp.einsum("ab,gij->gaibj", jnp.eye(reps, dtype=F32),
                      ws_tril[:, :t_new, :t_new]).reshape(GM_GROUPS, CHUNK, CHUNK)
    bias_s = jnp.tile(bias_p[:t_new], (reps, 1))
    wo = w_out[0].astype(BF16)
    wg = w_ple_gate[0].astype(BF16)
    wp = w_ple_proj[0].astype(BF16)
    pg = ple_norm_g[0][None, :]

    xp = x_prompt.reshape(bsz * seq, d_model)
    (_, k, v, kb, _, kiwi, kib, ga, gm, vt, qt, qit) = _project(
        xp, _rope_tables(jnp.arange(seq)), _pair_mix(ws_tril), bias_p, weights,
        emit_vn=False, emit_vt=True)
    nqb = seq // Q_BLOCK
    k_top_p = min(TOPK_MAX, seq // 4)
    qt = qt.reshape(bsz, nqb, N_KV_HEADS, HEAD_DIM, 2 * Q_BLOCK)
    qit = qit.reshape(bsz, nqb, N_IDX_HEADS // 2, D_IDX, 2 * Q_BLOCK)
    logit_bound = (HEAD_DIM ** 0.5 * LOG2_E * BOUND_MARGIN
                   * jnp.max(jnp.abs(q_norm_g[0])) * jnp.max(jnp.abs(k_norm_g[0]))).reshape(1)
    o = _prompt_attend(logit_bound, qt, qit, kiwi, kb.reshape(bsz, seq, KV_WIDTH),
                       vt.reshape(bsz, seq // KEY_TILE, KV_WIDTH, KEY_TILE),
                       kib.reshape(bsz, seq, LANES), k_top_p)
    y_prompt = _finish(xp, o, ga, gm, p_prompt[0].reshape(bsz * seq, -1), wo, pg, wg, wp)

    xs = x_sample.reshape(dbsz * t_new, d_model)
    pos_s = past + jnp.arange(min(ROW_TILE, dbsz * t_new)) % t_new
    (q, k_s, v_s, _, qi, kiwi_s, _, ga, gm, vn_s) = _project(
        xs, _rope_tables(pos_s), _pair_mix(ws_s), bias_s, weights, emit_vn=True, emit_vt=False)
    k_top_s = min(TOPK_MAX, (past + t_new) // 4)
    qbd = _group_diag(q.reshape(dbsz, t_new, ATT_WIDTH), (dbsz,)).reshape(dbsz, N_HEADS * t_new, KV_WIDTH)
    qih = qi.reshape(dbsz, t_new, N_IDX_HEADS, D_IDX).transpose(0, 2, 1, 3).reshape(
        dbsz, N_IDX_HEADS * t_new, D_IDX)
    kiwi3 = kiwi_s.reshape(dbsz, t_new, LANES)
    w_rows = kiwi3[:, :, D_IDX:D_IDX + N_IDX_HEADS].transpose(0, 2, 1).reshape(dbsz, N_IDX_HEADS * t_new, 1)
    w_rows = jnp.broadcast_to(w_rows, (dbsz, N_IDX_HEADS * t_new, LANES))
    acc = _sample_attend(page_table, qbd, qih, w_rows,
                         k_s.reshape(dbsz, t_new, KV_WIDTH), v_s.reshape(dbsz, t_new, KV_WIDTH), kiwi3,
                         cache_idx_k[0].transpose(0, 2, 1),
                         cache_k[0].transpose(0, 2, 3, 1).reshape(-1, KV_WIDTH, PAGE_SIZE),
                         cache_v[0].transpose(0, 2, 3, 1).reshape(-1, KV_WIDTH, PAGE_SIZE), k_top_s)
    acc = acc.reshape(dbsz, N_KV_HEADS, 2, t_new, N_KV_HEADS, HEAD_DIM)
    o_s = jnp.stack([acc[:, g, :, :, g, :] for g in range(N_KV_HEADS)], axis=1)
    o_s = o_s.transpose(0, 3, 1, 2, 4).reshape(dbsz * t_new, ATT_WIDTH)
    y_sample = _finish(xs, o_s, ga, gm, p_sample[0].reshape(dbsz * t_new, -1), wo, pg, wg, wp)

    return (y_prompt.reshape(bsz, seq, d_model),
            y_sample.reshape(dbsz, t_new, d_model),
            k.reshape(1, bsz, seq, N_KV_HEADS, HEAD_DIM),
            v.reshape(1, bsz, seq, N_KV_HEADS, HEAD_DIM),
            kiwi[:, :D_IDX].reshape(1, bsz, seq, D_IDX),
            k_s.reshape(1, dbsz, t_new, N_KV_HEADS, HEAD_DIM),
            v_s.reshape(1, dbsz, t_new, N_KV_HEADS, HEAD_DIM),
            kiwi_s[:, :D_IDX].reshape(1, dbsz, t_new, D_IDX),
            vn_s.reshape(1, dbsz, t_new, GM_WIDTH))
```

```python
import functools

import numpy as np
import jax
import jax.numpy as jnp
from jax import lax
from jax.experimental import pallas as pl
from jax.experimental.pallas import tpu as pltpu

F32 = jnp.float32
BF16 = jnp.bfloat16
I32 = jnp.int32

HEAD_DIM = 64
N_HEADS = 8
N_KV_HEADS = 4
N_IDX_HEADS = 8
D_IDX = 64
TOPK_MAX = 256
GM_GROUPS = 8
CHUNK = 128
PAGE_SIZE = 128
ROPE_THETA = 500000.0
ROPE_ROT = HEAD_DIM // 4
ROPE_HALF = ROPE_ROT // 2
EPS = 1e-6

ATT_WIDTH = N_HEADS * HEAD_DIM
KV_WIDTH = N_KV_HEADS * HEAD_DIM
IDX_WIDTH = N_IDX_HEADS * D_IDX
GM_WIDTH = GM_GROUPS * 64
LANES = 128
SUBLANES = 8
BF16_SUBLANES = 16
KEY_TILE = 256
Q_BLOCK = 128
ROW_TILE = 256
SAMPLE_CHUNK_PAGES = 8
VMEM_LIMIT = 56 * 1024 * 1024

C_Q, C_K, C_V, C_QI, C_GA, C_U, C_VM, C_GM, C_KIWI, C_END = (
    0, 512, 768, 1024, 1536, 2048, 2560, 3072, 3584, 3712)
_ORIG_KI, _ORIG_GA, _ORIG_END = 1536, 1608, 3656
LOG2_E = 1.4426950408889634
MAX_UNSHIFTED_LOG2 = 60.0
BOUND_MARGIN = 1.05
FIRST_SETTLE_CHECK = 20
SETTLE_CHECK_EVERY = 4
NEG_INF = float("-inf")
INT_MIN = -2 ** 31


def _head_rms(xh, seg, g):
    sq = xh * xh
    hi = sq.astype(BF16)
    lo = (sq - hi.astype(F32)).astype(BF16)
    ms = (jnp.dot(hi, seg, preferred_element_type=F32)
          + jnp.dot(lo, seg, preferred_element_type=F32))
    return xh * lax.rsqrt(ms + EPS) * g


def _rope(xh, cos, sa, sb):
    w = xh.shape[-1]
    return xh * cos + pltpu.roll(xh, w - ROPE_HALF, 1) * sa + pltpu.roll(xh, ROPE_HALF, 1) * sb


def _tile_lanes(t, width):
    reps = width // t.shape[-1]
    return t if reps == 1 else jnp.concatenate([t] * reps, axis=1)


def _project_body(emit_vn, emit_vt, x_ref, ng_ref, w_ref, qg_ref, kg_ref, lng_ref, lnb_ref, mixw_ref,
                  bias_ref, cos_ref, sa_ref, sb_ref, seg_ref, *outs):
    q_o, k_o, v_o, kb_o, qi_o, kiwi_o, kib_o, ga_o, gm_o = outs[:9]
    extra = list(outs[9:])
    vn_o = extra.pop(0) if emit_vn else None
    vt_o, qt_o, qit_o = extra if emit_vt else (None, None, None)

    def store_cols(dst, xq):
        for blk in range(xq.shape[0] // Q_BLOCK):
            xt = xq[blk * Q_BLOCK:(blk + 1) * Q_BLOCK].T.astype(BF16)
            for p in range(xq.shape[1] // (2 * HEAD_DIM)):
                lo = 2 * p * HEAD_DIM
                dst[blk, p] = jnp.concatenate([xt[lo:lo + HEAD_DIM], xt[lo + HEAD_DIM:lo + 2 * HEAD_DIM]],
                                              axis=1)
    x = x_ref[...]
    h = x * lax.rsqrt(jnp.mean(x * x, axis=-1, keepdims=True) + EPS) * ng_ref[...]
    hb = h.astype(BF16)

    def proj(c0, c1):
        return jnp.dot(hb, w_ref[:, c0:c1], preferred_element_type=F32)

    cos, sa, sb = cos_ref[...], sa_ref[...], sb_ref[...]
    seg = seg_ref[...]

    q = _head_rms(proj(C_Q, C_K), seg, qg_ref[...])
    q = _rope(q, _tile_lanes(cos, ATT_WIDTH), _tile_lanes(sa, ATT_WIDTH), _tile_lanes(sb, ATT_WIDTH))
    q = q * (HEAD_DIM ** -0.5 * LOG2_E)
    q_o[...] = q.astype(BF16)
    if emit_vt:
        store_cols(qt_o, q)

    k = _head_rms(proj(C_K, C_V), seg[:KV_WIDTH, :KV_WIDTH], kg_ref[...])
    k = _rope(k, _tile_lanes(cos, KV_WIDTH), _tile_lanes(sa, KV_WIDTH), _tile_lanes(sb, KV_WIDTH))
    k_o[...] = k
    kb_o[...] = k.astype(BF16)

    v = proj(C_V, C_QI)
    v_o[...] = v
    if emit_vt:
        vt_o[...] = v.T.astype(BF16)

    qi = _rope(proj(C_QI, C_GA), _tile_lanes(cos, IDX_WIDTH), _tile_lanes(sa, IDX_WIDTH),
               _tile_lanes(sb, IDX_WIDTH))
    qi_o[...] = qi.astype(BF16)
    if emit_vt:
        store_cols(qit_o, qi)

    lane = lax.broadcasted_iota(I32, cos.shape, 1)
    is_ki = lane < D_IDX
    wi_scale = jnp.where(lane < D_IDX + N_IDX_HEADS, IDX_WIDTH ** -0.5, 1.0)
    kiwi = _rope(proj(C_KIWI, C_END), jnp.where(is_ki, cos, wi_scale),
                 jnp.where(is_ki, sa, 0.0), jnp.where(is_ki, sb, 0.0))
    kiwi_o[...] = kiwi
    kib_o[...] = kiwi.astype(BF16)

    ga_o[...] = jax.nn.silu(proj(C_GA, C_U))

    vmg = jax.nn.gelu(proj(C_VM, C_GM))
    xc = vmg - jnp.mean(vmg, axis=-1, keepdims=True)
    vn = xc * lax.rsqrt(jnp.mean(xc * xc, axis=-1, keepdims=True) + EPS) * lng_ref[...] + lnb_ref[...]
    if emit_vn:
        vn_o[...] = vn

    lane_c = lax.broadcasted_iota(I32, (CHUNK, LANES), 1)
    rows = x.shape[0]
    chunks = []
    for c in range(rows // CHUNK):
        pieces = []
        for p in range(GM_GROUPS // 2):
            t = vn[c * CHUNK:(c + 1) * CHUNK, p * LANES:(p + 1) * LANES]
            rhs = jnp.concatenate([jnp.where(lane_c < 64, t, 0.0), jnp.where(lane_c >= 64, t, 0.0)],
                                  axis=0).astype(BF16)
            pieces.append(jnp.dot(mixw_ref[p], rhs, preferred_element_type=F32))
        chunks.append(jnp.concatenate(pieces, axis=1) + bias_ref[...])
    s = jnp.concatenate(chunks, axis=0)
    gm_o[...] = jax.nn.gelu(proj(C_U, C_VM)) * s * jax.nn.silu(proj(C_GM, C_KIWI))


def _project(x2, pos_tables, mixw, bias, weights, emit_vn, emit_vt):
    rows = x2.shape[0]
    tm = min(ROW_TILE, rows)
    assert rows % tm == 0 and tm % CHUNK == 0
    ng, w, qg, kg, lng, lnb, seg = weights
    cos, sa, sb = pos_tables
    assert cos.shape[0] % tm == 0
    n_pos_blocks = cos.shape[0] // tm

    def row_map(i):
        return (i, 0)

    def pos_map(i):
        return (i % n_pos_blocks, 0)

    def const(i):
        return (0, 0)

    def rspec(width):
        return pl.BlockSpec((tm, width), row_map)

    def cspec(a):
        if a.ndim == 3:
            return pl.BlockSpec(a.shape, lambda i: (0, 0, 0))
        return pl.BlockSpec(a.shape, const)

    out_shapes = [
        jax.ShapeDtypeStruct((rows, ATT_WIDTH), BF16),
        jax.ShapeDtypeStruct((rows, KV_WIDTH), F32),
        jax.ShapeDtypeStruct((rows, KV_WIDTH), F32),
        jax.ShapeDtypeStruct((rows, KV_WIDTH), BF16),
        jax.ShapeDtypeStruct((rows, IDX_WIDTH), BF16),
        jax.ShapeDtypeStruct((rows, LANES), F32),
        jax.ShapeDtypeStruct((rows, LANES), BF16),
        jax.ShapeDtypeStruct((rows, ATT_WIDTH), F32),
        jax.ShapeDtypeStruct((rows, GM_WIDTH), F32),
    ]
    if emit_vn:
        out_shapes.append(jax.ShapeDtypeStruct((rows, GM_WIDTH), F32))
    out_specs = [rspec(s.shape[1]) for s in out_shapes]
    if emit_vt:
        assert tm == KEY_TILE
        out_shapes.append(jax.ShapeDtypeStruct((rows // tm, KV_WIDTH, tm), BF16))
        out_specs.append(pl.BlockSpec((None, KV_WIDTH, tm), lambda i: (i, 0, 0)))
        for width in (ATT_WIDTH, IDX_WIDTH):
            pairs = width // (2 * HEAD_DIM)
            out_shapes.append(jax.ShapeDtypeStruct((rows // Q_BLOCK, pairs, HEAD_DIM, 2 * Q_BLOCK), BF16))
            out_specs.append(pl.BlockSpec((tm // Q_BLOCK, pairs, HEAD_DIM, 2 * Q_BLOCK),
                                          lambda i: (i, 0, 0, 0)))
    in_specs = [rspec(x2.shape[1]), cspec(ng), cspec(w), cspec(qg), cspec(kg), cspec(lng), cspec(lnb),
                cspec(mixw), cspec(bias),
                pl.BlockSpec((tm, LANES), pos_map), pl.BlockSpec((tm, LANES), pos_map),
                pl.BlockSpec((tm, LANES), pos_map), cspec(seg)]
    return pl.pallas_call(
        functools.partial(_project_body, emit_vn, emit_vt),
        grid=(rows // tm,),
        in_specs=in_specs,
        out_specs=out_specs,
        out_shape=out_shapes,
        compiler_params=pltpu.CompilerParams(dimension_semantics=("arbitrary",),
                                             vmem_limit_bytes=VMEM_LIMIT),
        name="project",
    )(x2, ng, w, qg, kg, lng, lnb, mixw, bias, cos, sa, sb, seg)


def _key_to_float(u):
    bits = jnp.where(u < 0, u ^ I32(INT_MIN), ~u)
    return lax.bitcast_convert_type(bits, F32)


def _count(sc_ref, n_tiles, key_axis, pred, unroll=1, tail=0):
    a, b = sc_ref.shape[1], sc_ref.shape[2]

    def fold(ind):
        if key_axis == 1:
            return ind
        parts = [ind[i * SUBLANES:(i + 1) * SUBLANES] for i in range(a // SUBLANES)]
        while len(parts) > 1:
            parts = [parts[i] + parts[i + 1] for i in range(0, len(parts), 2)]
        return parts[0]

    def tiles(j0, n, c):
        for u in range(n):
            c = c + fold(jnp.where(pred(sc_ref[j0 + u], j0 + u), 1.0, 0.0))
        return c

    init = jnp.zeros((a, b) if key_axis == 1 else (SUBLANES, b), F32)
    n_steps = n_tiles // unroll
    c = lax.fori_loop(0, n_steps, lambda i, c: tiles(i * unroll, unroll, c), init)
    if tail:
        c = lax.cond(n_tiles - n_steps * unroll == tail, lambda c: tiles(n_tiles - tail, tail, c),
                     lambda c: c, c)
    return jnp.sum(c, axis=key_axis, keepdims=True)


def _select_threshold(sc_ref, n_tiles, k_row, key_axis, unroll=1, tail=0):
    k_f = k_row.astype(F32)

    def bit_step(i, carry):
        res, cnt_res = carry
        cand = res | lax.shift_left(I32(1), I32(31) - i)
        t = _key_to_float(cand)
        cnt = _count(sc_ref, n_tiles, key_axis, lambda x, j: x >= t, unroll, tail)
        ok = cnt >= k_f
        return jnp.where(ok, cand, res), jnp.where(ok, cnt, cnt_res)

    state = (jnp.zeros(k_row.shape, I32), jnp.zeros(k_row.shape, F32))
    state = lax.fori_loop(0, FIRST_SETTLE_CHECK, bit_step, state)
    for lo in range(FIRST_SETTLE_CHECK, 32, SETTLE_CHECK_EVERY):
        settled = jnp.min(jnp.where(state[1] == k_f, 1.0, 0.0)) > 0.0
        state = lax.cond(settled, lambda s: s,
                         lambda s, lo=lo: lax.fori_loop(lo, lo + SETTLE_CHECK_EVERY, bit_step, s), state)
    res, cnt_res = state
    thr = _key_to_float(res)
    _drop_surplus_ties(sc_ref, n_tiles, thr, k_f, cnt_res, key_axis)
    return thr


def _drop_surplus_ties(sc_ref, n_tiles, thr, k_f, cnt_ge, key_axis):
    t = sc_ref.shape[1 + key_axis]
    surplus = jnp.max(cnt_ge - k_f)

    @pl.when(surplus > 0.0)
    def _():
        n_take = k_f - _count(sc_ref, n_tiles, key_axis, lambda x, j: x > thr)
        r = lax.broadcasted_iota(I32, (t, t), 0)
        c = lax.broadcasted_iota(I32, (t, t), 1)
        tri = jnp.where((r >= c) if key_axis == 0 else (r <= c), 1.0, 0.0).astype(BF16)

        def body(j, before):
            x = sc_ref[j]
            tied = jnp.where(x == thr, 1.0, 0.0)
            if key_axis == 0:
                upto = jnp.dot(tri, tied.astype(BF16), preferred_element_type=F32)
                total = upto[t - 1:t]
            else:
                upto = jnp.dot(tied.astype(BF16), tri, preferred_element_type=F32)
                total = upto[:, t - 1:t]
            rank = tied * (before + upto)
            sc_ref[j] = jnp.where(rank > n_take, NEG_INF, x)
            return before + total

        lax.fori_loop(0, n_tiles, body, jnp.zeros(thr.shape, F32))


def _attend_init(m_ref, l_ref, acc_ref):
    m_ref[...] = jnp.full(m_ref.shape, NEG_INF, F32)
    l_ref[...] = jnp.zeros(l_ref.shape, F32)
    acc_ref[...] = jnp.zeros(acc_ref.shape, F32)


def _softmax_step(s, m_prev, l_prev):
    m_new = jnp.maximum(m_prev, jnp.max(s, axis=1, keepdims=True))
    m_safe = jnp.where(m_new == NEG_INF, 0.0, m_new)
    alpha = jnp.exp2(m_prev - m_safe)
    p = jnp.exp2(s - _tile_lanes(m_safe, s.shape[1]))
    l_new = alpha * l_prev + jnp.sum(p, axis=1, keepdims=True)
    return p, m_new, l_new, alpha


_NT = (((1,), (1,)), ((), ()))


def _two_stage(n_tiles, produce, consume, carry, buf_a, buf_b):
    produce(0, buf_a)

    def pair(j0, c, last):
        produce(j0 + 1, buf_b)
        c = consume(j0, buf_a, c)
        if not last:
            produce(jnp.minimum(j0 + 2, n_tiles - 1), buf_a)
        return consume(j0 + 1, buf_b, c)

    def pairs(j0, n, c):
        for q in range(n):
            c = pair(j0 + 2 * q, c, False)
        return c

    n_steps = n_tiles // 8
    carry = lax.fori_loop(0, n_steps, lambda i, c: pairs(8 * i, 4, c), carry)
    left = n_tiles - 8 * n_steps
    carry = lax.cond(left >= 4, lambda c: pairs(8 * n_steps, 2, c), lambda c: c, carry)
    return lax.cond(left % 4 == 2, lambda c: pair(n_tiles - 2, c, True), lambda c: c, carry)


def _prompt_attend_body(k_top, bound_ref, qt_ref, qit_ref, kiwi_ref, kb_ref, vt_ref,
                        kib_ref, o_ref, sc_ref, acc_ref, buf_a, buf_b, qbdt_ref, qiht_ref):
    qb = pl.program_id(1)

    @pl.when((pl.program_id(0) == 0) & (qb == 0))
    def _():
        qbdt_ref[...] = jnp.zeros(qbdt_ref.shape, BF16)
        qiht_ref[...] = jnp.zeros(qiht_ref.shape, BF16)

    for g in range(N_KV_HEADS):
        qbdt_ref[g, g * HEAD_DIM:(g + 1) * HEAD_DIM, :] = qt_ref[g]
        qiht_ref[g, :D_IDX, :] = qit_ref[g]
    n_tiles = 2 * ((qb * Q_BLOCK + Q_BLOCK + 2 * KEY_TILE - 1) // (2 * KEY_TILE))
    qpos = qb * Q_BLOCK + lax.broadcasted_iota(I32, (1, Q_BLOCK), 1)

    w_t = kiwi_ref[...].T[D_IDX:D_IDX + N_IDX_HEADS, :]
    w_pairs = [jnp.concatenate([w_t[2 * p:2 * p + 1], w_t[2 * p + 1:2 * p + 2]], axis=1)
               for p in range(N_IDX_HEADS // 2)]
    krow = lax.broadcasted_iota(I32, (KEY_TILE, Q_BLOCK), 0)

    def score_dots(j, buf):
        start = pl.multiple_of(j * KEY_TILE, KEY_TILE)
        ki_t = kib_ref[pl.ds(start, KEY_TILE), :]
        for p in range(N_IDX_HEADS // 2):
            buf[p] = jnp.dot(ki_t, qiht_ref[p], preferred_element_type=F32)

    def score_sum(j, buf, c):
        acc = jnp.zeros((KEY_TILE, Q_BLOCK), F32)
        for p in range(N_IDX_HEADS // 2):
            d = jnp.maximum(buf[p], 0.0) * w_pairs[p]
            acc = acc + d[:, :Q_BLOCK] + d[:, Q_BLOCK:]
        sc_ref[j] = jnp.where(j * KEY_TILE + krow <= qpos, acc, NEG_INF)
        return c

    _two_stage(n_tiles, score_dots, score_sum, 0, buf_a, buf_b)

    k_row = jnp.minimum(k_top, qpos + 1)
    thr = _select_threshold(sc_ref, n_tiles, k_row, key_axis=0, unroll=4, tail=2)
    thr2 = jnp.concatenate([thr, thr], axis=1)

    acc_ref[...] = jnp.zeros(acc_ref.shape, F32)
    m0 = tuple(jnp.full((1, 2 * Q_BLOCK), NEG_INF, F32) for _ in range(N_KV_HEADS))
    l0 = tuple(jnp.zeros((1, 2 * Q_BLOCK), F32) for _ in range(N_KV_HEADS))

    ones_rows = jnp.ones((BF16_SUBLANES, KEY_TILE), BF16)

    def logits(j, buf):
        start = pl.multiple_of(j * KEY_TILE, KEY_TILE)
        k_t = kb_ref[pl.ds(start, KEY_TILE), :]
        for g in range(N_KV_HEADS):
            buf[g] = jnp.dot(k_t, qbdt_ref[g], preferred_element_type=F32)

    def weighted_values(v_t, g, p):
        lhs = jnp.concatenate([v_t[g * HEAD_DIM:(g + 1) * HEAD_DIM, :], ones_rows], axis=0)
        pv = jnp.dot(lhs, p, preferred_element_type=F32)
        return pv[:HEAD_DIM], pv[HEAD_DIM:HEAD_DIM + 1]

    def softmax_pv(j, buf, carry):
        ms, ls = carry
        v_t = vt_ref[j]
        sc = sc_ref[j]
        keep = jnp.concatenate([sc, sc], axis=1) >= thr2
        new_m, new_l = [], []
        for g in range(N_KV_HEADS):
            s = jnp.where(keep, buf[g], NEG_INF)
            m_new = jnp.maximum(ms[g], jnp.max(s, axis=0, keepdims=True))
            m_safe = jnp.where(m_new == NEG_INF, 0.0, m_new)
            alpha = jnp.exp2(ms[g] - m_safe)
            pv, psum = weighted_values(v_t, g, jnp.exp2(s - m_safe).astype(BF16))
            new_m.append(m_new)
            new_l.append(alpha * ls[g] + psum)
            acc_ref[g] = acc_ref[g] * alpha + pv
        return tuple(new_m), tuple(new_l)

    def exp_pv(j, buf, ls):
        v_t = vt_ref[j]
        sc = sc_ref[j]
        keep = jnp.concatenate([sc, sc], axis=1) >= thr2
        new_l = []
        for g in range(N_KV_HEADS):
            pv, psum = weighted_values(v_t, g, jnp.exp2(jnp.where(keep, buf[g], NEG_INF)).astype(BF16))
            new_l.append(ls[g] + psum)
            acc_ref[g] = acc_ref[g] + pv
        return tuple(new_l)

    def write_out(ls):
        blocks = []
        for g in range(N_KV_HEADS):
            og = acc_ref[g] / ls[g]
            blocks += [og[:, :Q_BLOCK], og[:, Q_BLOCK:]]
        o_ref[...] = jnp.concatenate(blocks, axis=0).T

    bounded = bound_ref[0] < MAX_UNSHIFTED_LOG2

    @pl.when(bounded)
    def _():
        write_out(_two_stage(n_tiles, logits, exp_pv, l0, buf_a, buf_b))

    @pl.when(jnp.logical_not(bounded))
    def _():
        write_out(_two_stage(n_tiles, logits, softmax_pv, (m0, l0), buf_a, buf_b)[1])


def _prompt_attend(logit_bound, qt, qit, kiwi, kb, vt, kib, k_top):
    bsz, nqb = qt.shape[:2]
    seq = kb.shape[1]
    assert N_IDX_HEADS // 2 == N_KV_HEADS
    resident = dict(pipeline_mode=pl.Buffered(1))
    return pl.pallas_call(
        functools.partial(_prompt_attend_body, k_top),
        grid=(bsz, nqb),
        in_specs=[
            pl.BlockSpec(memory_space=pltpu.SMEM),
            pl.BlockSpec((None, None, N_KV_HEADS, HEAD_DIM, 2 * Q_BLOCK), lambda b, i: (b, i, 0, 0, 0)),
            pl.BlockSpec((None, None, N_IDX_HEADS // 2, D_IDX, 2 * Q_BLOCK), lambda b, i: (b, i, 0, 0, 0)),
            pl.BlockSpec((Q_BLOCK, LANES), lambda b, i: (b * nqb + i, 0)),
            pl.BlockSpec((None, seq, KV_WIDTH), lambda b, i: (b, 0, 0), **resident),
            pl.BlockSpec((None, seq // KEY_TILE, KV_WIDTH, KEY_TILE), lambda b, i: (b, 0, 0, 0), **resident),
            pl.BlockSpec((None, seq, LANES), lambda b, i: (b, 0, 0), **resident),
        ],
        out_specs=pl.BlockSpec((Q_BLOCK, ATT_WIDTH), lambda b, i: (b * nqb + i, 0)),
        out_shape=jax.ShapeDtypeStruct((bsz * seq, ATT_WIDTH), F32),
        scratch_shapes=[
            pltpu.VMEM((seq // KEY_TILE, KEY_TILE, Q_BLOCK), F32),
            pltpu.VMEM((N_KV_HEADS, HEAD_DIM, 2 * Q_BLOCK), F32),
            pltpu.VMEM((N_KV_HEADS, KEY_TILE, 2 * Q_BLOCK), F32),
            pltpu.VMEM((N_KV_HEADS, KEY_TILE, 2 * Q_BLOCK), F32),
            pltpu.VMEM((N_KV_HEADS, KV_WIDTH, 2 * Q_BLOCK), BF16),
            pltpu.VMEM((N_IDX_HEADS // 2, LANES, 2 * Q_BLOCK), BF16),
        ],
        compiler_params=pltpu.CompilerParams(dimension_semantics=("arbitrary", "arbitrary"),
                                             vmem_limit_bytes=VMEM_LIMIT),
        name="prompt_attend",
    )(logit_bound, qt, qit, kiwi, kb, vt, kib)


def _sample_attend_body(k_top, n_pages, t_new,
                        pt_ref, qbd_ref, qih_ref, w_ref, knew_ref, vnew_ref, kinew_ref,
                        cik_hbm, ck_hbm, cv_hbm, o_ref,
                        ibuf, kbuf, vbuf, sem, sc_ref, m_ref, l_ref, acc_ref):
    b = pl.program_id(0)
    nb = pl.num_programs(0)
    slot = b % 2
    cp = SAMPLE_CHUNK_PAGES
    n_chunks = n_pages // cp
    width = cp * PAGE_SIZE

    def page_copies(bb, s, p):
        page = pt_ref[bb, p]
        return (pltpu.make_async_copy(cik_hbm.at[page], ibuf.at[s, p], sem.at[s, 0]),
                pltpu.make_async_copy(ck_hbm.at[page], kbuf.at[s, p], sem.at[s, 1]),
                pltpu.make_async_copy(cv_hbm.at[page], vbuf.at[s, p], sem.at[s, 2]))

    def start_fetch(bb, s):
        def body(p, _):
            for c in page_copies(bb, s, p):
                c.start()
            return 0
        lax.fori_loop(0, n_pages, body, 0)

    def wait_fetch(bb, s):
        def body(p, _):
            for c in page_copies(bb, s, p):
                c.wait()
            return 0
        lax.fori_loop(0, n_pages, body, 0)

    @pl.when(b == 0)
    def _():
        start_fetch(b, slot)

    @pl.when(b + 1 < nb)
    def _():
        start_fetch(b + 1, 1 - slot)

    wait_fetch(b, slot)

    def chunk(buf, c):
        return jnp.concatenate([buf[slot, c * cp + i] for i in range(cp)], axis=1).astype(BF16)

    def new_cols(a):
        padded = jnp.concatenate([a, jnp.zeros((LANES - a.shape[0], a.shape[1]), a.dtype)], axis=0)
        return padded.T.astype(BF16)

    qih = qih_ref[...]
    w = _tile_lanes(w_ref[...], width)

    def head_sum(ki_c):
        d = jnp.dot(qih, ki_c, preferred_element_type=F32)
        d = w[:, :ki_c.shape[1]] * jnp.maximum(d, 0.0)
        acc = d[0:t_new]
        for h in range(1, N_IDX_HEADS):
            acc = acc + d[h * t_new:(h + 1) * t_new]
        return acc

    def score_chunk(c, _):
        sc_ref[c] = head_sum(chunk(ibuf, c))
        return 0

    lax.fori_loop(0, n_chunks, score_chunk, 0)

    tq = lax.broadcasted_iota(I32, (t_new, LANES), 0)
    col = lax.broadcasted_iota(I32, (t_new, LANES), 1)
    sc_new = jnp.where((col < t_new) & (col <= tq), head_sum(new_cols(kinew_ref[...])[:D_IDX]), NEG_INF)
    sc_ref[n_chunks] = jnp.concatenate([sc_new, jnp.full((t_new, width - LANES), NEG_INF, F32)], axis=1)

    qpos = n_pages * PAGE_SIZE + lax.broadcasted_iota(I32, (t_new, 1), 0)
    k_row = jnp.minimum(k_top, qpos + 1)
    thr = _select_threshold(sc_ref, n_chunks + 1, k_row, key_axis=1, unroll=n_chunks + 1)

    _attend_init(m_ref, l_ref, acc_ref)
    qbd = qbd_ref[...]

    def attend(sc, k_c, v_c):
        bias = jnp.where(sc >= thr, 0.0, NEG_INF)
        s = jnp.dot(qbd, k_c, preferred_element_type=F32) + jnp.concatenate([bias] * N_HEADS, axis=0)
        p, m_new, l_new, alpha = _softmax_step(s, m_ref[...], l_ref[...])
        m_ref[...] = m_new
        l_ref[...] = l_new
        acc_ref[...] = (acc_ref[...] * _tile_lanes(alpha, KV_WIDTH)
                        + lax.dot_general(p.astype(BF16), v_c, _NT, preferred_element_type=F32))

    def attend_chunk(c, _):
        attend(sc_ref[c], chunk(kbuf, c), chunk(vbuf, c))
        return 0

    lax.fori_loop(0, n_chunks, attend_chunk, 0)
    attend(sc_ref[n_chunks][:, :LANES], new_cols(knew_ref[...]), new_cols(vnew_ref[...]))

    o_ref[...] = acc_ref[...] / _tile_lanes(l_ref[...], KV_WIDTH)


def _sample_attend(page_table, qbd, qih, w_rows, knew, vnew, kinew, cik_t, ck_t, cv_t, k_top):
    dbsz, n_pages = page_table.shape
    t_new = knew.shape[1]
    rows = N_HEADS * t_new
    past = n_pages * PAGE_SIZE
    assert n_pages % SAMPLE_CHUNK_PAGES == 0
    width = SAMPLE_CHUNK_PAGES * PAGE_SIZE
    n_chunks = n_pages // SAMPLE_CHUNK_PAGES

    def bmap(b, pt):
        return (b, 0, 0)

    grid_spec = pltpu.PrefetchScalarGridSpec(
        num_scalar_prefetch=1,
        grid=(dbsz,),
        in_specs=[
            pl.BlockSpec((None, rows, KV_WIDTH), bmap),
            pl.BlockSpec((None, rows, D_IDX), bmap),
            pl.BlockSpec((None, rows, LANES), bmap),
            pl.BlockSpec((None, t_new, KV_WIDTH), bmap),
            pl.BlockSpec((None, t_new, KV_WIDTH), bmap),
            pl.BlockSpec((None, t_new, LANES), bmap),
            pl.BlockSpec(memory_space=pl.ANY),
            pl.BlockSpec(memory_space=pl.ANY),
            pl.BlockSpec(memory_space=pl.ANY),
        ],
        out_specs=pl.BlockSpec((None, rows, KV_WIDTH), bmap),
        scratch_shapes=[
            pltpu.VMEM((2, n_pages, D_IDX, PAGE_SIZE), F32),
            pltpu.VMEM((2, n_pages, KV_WIDTH, PAGE_SIZE), F32),
            pltpu.VMEM((2, n_pages, KV_WIDTH, PAGE_SIZE), F32),
            pltpu.SemaphoreType.DMA((2, 3)),
            pltpu.VMEM((n_chunks + 1, t_new, width), F32),
            pltpu.VMEM((rows, LANES), F32),
            pltpu.VMEM((rows, LANES), F32),
            pltpu.VMEM((rows, KV_WIDTH), F32),
        ],
    )
    return pl.pallas_call(
        functools.partial(_sample_attend_body, k_top, n_pages, t_new),
        grid_spec=grid_spec,
        out_shape=jax.ShapeDtypeStruct((dbsz, rows, KV_WIDTH), F32),
        compiler_params=pltpu.CompilerParams(dimension_semantics=("arbitrary",),
                                             vmem_limit_bytes=VMEM_LIMIT),
        name="sample_attend",
    )(page_table, qbd, qih, w_rows, knew, vnew, kinew, cik_t, ck_t, cv_t)


def _finish_body(x_ref, o_ref, ga_ref, gm_ref, p_ref, wo_ref, pg_ref, wg_ref, wp_ref, y_ref):
    att = (o_ref[...] * ga_ref[...]).astype(BF16)
    r = (x_ref[...]
         + jnp.dot(att, wo_ref[:ATT_WIDTH, :], preferred_element_type=F32)
         + jnp.dot(gm_ref[...].astype(BF16), wo_ref[ATT_WIDTH:, :], preferred_element_type=F32))
    rn = r * lax.rsqrt(jnp.mean(r * r, axis=-1, keepdims=True) + EPS) * pg_ref[...]
    gate = jax.nn.sigmoid(jnp.dot(rn.astype(BF16), wg_ref[...], preferred_element_type=F32))
    y_ref[...] = r + gate * jnp.dot(p_ref[...].astype(BF16), wp_ref[...], preferred_element_type=F32)


def _finish(x2, o, ga, gm, p2, wo, pg, wg, wp):
    rows, d_model = x2.shape
    tm = min(ROW_TILE, rows)

    def rspec(a):
        return pl.BlockSpec((tm, a.shape[1]), lambda i: (i, 0))

    def cspec(a):
        return pl.BlockSpec(a.shape, lambda i: (0, 0))

    return pl.pallas_call(
        _finish_body,
        grid=(rows // tm,),
        in_specs=[rspec(x2), rspec(o), rspec(ga), rspec(gm), rspec(p2),
                  cspec(wo), cspec(pg), cspec(wg), cspec(wp)],
        out_specs=pl.BlockSpec((tm, d_model), lambda i: (i, 0)),
        out_shape=jax.ShapeDtypeStruct((rows, d_model), F32),
        compiler_params=pltpu.CompilerParams(dimension_semantics=("arbitrary",),
                                             vmem_limit_bytes=VMEM_LIMIT),
        name="finish",
    )(x2, o, ga, gm, p2, wo, pg, wg, wp)


def _rope_tables(pos):
    inv = jnp.power(ROPE_THETA, -jnp.arange(ROPE_HALF, dtype=F32) * 2.0 / ROPE_ROT)
    ang = pos.astype(F32)[:, None] * inv[None, :]
    cos, sin = jnp.cos(ang), jnp.sin(ang)
    n = pos.shape[0]
    one = jnp.ones((n, HEAD_DIM - ROPE_ROT), F32)
    zero8 = jnp.zeros((n, ROPE_HALF), F32)
    zero = jnp.zeros((n, HEAD_DIM - ROPE_ROT), F32)
    cos_h = jnp.concatenate([cos, cos, one], axis=1)
    sa_h = jnp.concatenate([-sin, zero8, zero], axis=1)
    sb_h = jnp.concatenate([zero8, sin, zero], axis=1)
    return tuple(jnp.concatenate([t, t], axis=1) for t in (cos_h, sa_h, sb_h))


def _pair_mix(ws):
    g = ws.shape[0]
    return ws.reshape(g // 2, 2, CHUNK, CHUNK).transpose(0, 2, 1, 3).reshape(g // 2, CHUNK, 2 * CHUNK).astype(BF16)


def _group_diag(q, lead):
    t = q.shape[-2]
    n = len(lead)
    q6 = q.reshape(*lead, t, N_KV_HEADS, 2, HEAD_DIM)
    q6 = jnp.moveaxis(q6, n, n + 2)
    q6 = q6.reshape(*lead, N_KV_HEADS, 2 * t, HEAD_DIM)
    eye = jnp.eye(N_KV_HEADS, dtype=q.dtype)
    qd = q6[..., None, :] * eye[:, None, :, None]
    return qd.reshape(*lead, N_KV_HEADS, 2 * t, KV_WIDTH)


def kernel(x_prompt, x_sample, cache_k, cache_v, cache_idx_k, page_table, p_prompt, p_sample,
           norm_in_g, w_in, q_norm_g, k_norm_g, ln_v_g, ln_v_b, w_s, b_s, w_out,
           ple_norm_g, w_ple_gate, w_ple_proj):
    depth = w_in.shape[0]
    assert depth == 1
    bsz, seq, d_model = x_prompt.shape
    dbsz, t_new, _ = x_sample.shape
    n_pages = page_table.shape[1]
    past = n_pages * PAGE_SIZE
    assert seq % KEY_TILE == 0 and CHUNK % t_new == 0 and (dbsz * t_new) % CHUNK == 0

    perm = np.concatenate([np.arange(0, _ORIG_KI), np.arange(_ORIG_GA, _ORIG_END),
                           np.arange(_ORIG_KI, _ORIG_GA)])
    w = jnp.pad(w_in[0][:, perm], ((0, 0), (0, C_END - _ORIG_END))).astype(BF16)
    seg = jnp.asarray(np.kron(np.eye(N_HEADS), np.full((HEAD_DIM, HEAD_DIM), 1.0 / HEAD_DIM)), BF16)
    weights = (norm_in_g[0][None, :], w, jnp.tile(q_norm_g[0], N_HEADS)[None, :],
               jnp.tile(k_norm_g[0], N_KV_HEADS)[None, :], ln_v_g[0][None, :], ln_v_b[0][None, :], seg)
    ws_tril = jnp.where(jnp.tril(jnp.ones((CHUNK, CHUNK), bool))[None], w_s[0], 0.0)
    bias_p = jnp.repeat(b_s[0].T, GM_WIDTH // GM_GROUPS, axis=1)
    reps = CHUNK // t_new
    ws_s = jnp.einsum("ab,gij->gaibj", jnp.eye(reps, dtype=F32),
                      ws_tril[:, :t_new, :t_new]).reshape(GM_GROUPS, CHUNK, CHUNK)
    bias_s = jnp.tile(bias_p[:t_new], (reps, 1))
    wo = w_out[0].astype(BF16)
    wg = w_ple_gate[0].astype(BF16)
    wp = w_ple_proj[0].astype(BF16)
    pg = ple_norm_g[0][None, :]

    xp = x_prompt.reshape(bsz * seq, d_model)
    (_, k, v, kb, _, kiwi, kib, ga, gm, vt, qt, qit) = _project(
        xp, _rope_tables(jnp.arange(seq)), _pair_mix(ws_tril), bias_p, weights,
        emit_vn=False, emit_vt=True)
    nqb = seq // Q_BLOCK
    k_top_p = min(TOPK_MAX, seq // 4)
    qt = qt.reshape(bsz, nqb, N_KV_HEADS, HEAD_DIM, 2 * Q_BLOCK)
    qit = qit.reshape(bsz, nqb, N_IDX_HEADS // 2, D_IDX, 2 * Q_BLOCK)
    logit_bound = (HEAD_DIM ** 0.5 * LOG2_E * BOUND_MARGIN
                   * jnp.max(jnp.abs(q_norm_g[0])) * jnp.max(jnp.abs(k_norm_g[0]))).reshape(1)
    o = _prompt_attend(logit_bound, qt, qit, kiwi, kb.reshape(bsz, seq, KV_WIDTH),
                       vt.reshape(bsz, seq // KEY_TILE, KV_WIDTH, KEY_TILE),
                       kib.reshape(bsz, seq, LANES), k_top_p)
    y_prompt = _finish(xp, o, ga, gm, p_prompt[0].reshape(bsz * seq, -1), wo, pg, wg, wp)

    xs = x_sample.reshape(dbsz * t_new, d_model)
    pos_s = past + jnp.arange(min(ROW_TILE, dbsz * t_new)) % t_new
    (q, k_s, v_s, _, qi, kiwi_s, _, ga, gm, vn_s) = _project(
        xs, _rope_tables(pos_s), _pair_mix(ws_s), bias_s, weights, emit_vn=True, emit_vt=False)
    k_top_s = min(TOPK_MAX, (past + t_new) // 4)
    qbd = _group_diag(q.reshape(dbsz, t_new, ATT_WIDTH), (dbsz,)).reshape(dbsz, N_HEADS * t_new, KV_WIDTH)
    qih = qi.reshape(dbsz, t_new, N_IDX_HEADS, D_IDX).transpose(0, 2, 1, 3).reshape(
        dbsz, N_IDX_HEADS * t_new, D_IDX)
    kiwi3 = kiwi_s.reshape(dbsz, t_new, LANES)
    w_rows = kiwi3[:, :, D_IDX:D_IDX + N_IDX_HEADS].transpose(0, 2, 1).reshape(dbsz, N_IDX_HEADS * t_new, 1)
    w_rows = jnp.broadcast_to(w_rows, (dbsz, N_IDX_HEADS * t_new, LANES))
    acc = _sample_attend(page_table, qbd, qih, w_rows,
                         k_s.reshape(dbsz, t_new, KV_WIDTH), v_s.reshape(dbsz, t_new, KV_WIDTH), kiwi3,
                         cache_idx_k[0].transpose(0, 2, 1),
                         cache_k[0].transpose(0, 2, 3, 1).reshape(-1, KV_WIDTH, PAGE_SIZE),
                         cache_v[0].transpose(0, 2, 3, 1).reshape(-1, KV_WIDTH, PAGE_SIZE), k_top_s)
    acc = acc.reshape(dbsz, N_KV_HEADS, 2, t_new, N_KV_HEADS, HEAD_DIM)
    o_s = jnp.stack([acc[:, g, :, :, g, :] for g in range(N_KV_HEADS)], axis=1)
    o_s = o_s.transpose(0, 3, 1, 2, 4).reshape(dbsz * t_new, ATT_WIDTH)
    y_sample = _finish(xs, o_s, ga, gm, p_sample[0].reshape(dbsz * t_new, -1), wo, pg, wg, wp)

    return (y_prompt.reshape(bsz, seq, d_model),
            y_sample.reshape(dbsz, t_new, d_model),
            k.reshape(1, bsz, seq, N_KV_HEADS, HEAD_DIM),
            v.reshape(1, bsz, seq, N_KV_HEADS, HEAD_DIM),
            kiwi[:, :D_IDX].reshape(1, bsz, seq, D_IDX),
            k_s.reshape(1, dbsz, t_new, N_KV_HEADS, HEAD_DIM),
            v_s.reshape(1, dbsz, t_new, N_KV_HEADS, HEAD_DIM),
            kiwi_s[:, :D_IDX].reshape(1, dbsz, t_new, D_IDX),
            vn_s.reshape(1, dbsz, t_new, GM_WIDTH))
```

```python
import functools

import numpy as np
import jax
import jax.numpy as jnp
from jax import lax
from jax.experimental import pallas as pl
from jax.experimental.pallas import tpu as pltpu

F32 = jnp.float32
BF16 = jnp.bfloat16
I32 = jnp.int32

HEAD_DIM = 64
N_HEADS = 8
N_KV_HEADS = 4
N_IDX_HEADS = 8
D_IDX = 64
TOPK_MAX = 256
GM_GROUPS = 8
CHUNK = 128
PAGE_SIZE = 128
ROPE_THETA = 500000.0
ROPE_ROT = HEAD_DIM // 4
ROPE_HALF = ROPE_ROT // 2
EPS = 1e-6

ATT_WIDTH = N_HEADS * HEAD_DIM
KV_WIDTH = N_KV_HEADS * HEAD_DIM
IDX_WIDTH = N_IDX_HEADS * D_IDX
GM_WIDTH = GM_GROUPS * 64
LANES = 128
SUBLANES = 8
BF16_SUBLANES = 16
KEY_TILE = 256
Q_BLOCK = 128
ROW_TILE = 256
SAMPLE_CHUNK_PAGES = 8
VMEM_LIMIT = 56 * 1024 * 1024

C_Q, C_K, C_V, C_QI, C_GA, C_U, C_VM, C_GM, C_KIWI, C_END = (
    0, 512, 768, 1024, 1536, 2048, 2560, 3072, 3584, 3712)
_ORIG_KI, _ORIG_GA, _ORIG_END = 1536, 1608, 3656
LOG2_E = 1.4426950408889634
MAX_UNSHIFTED_LOG2 = 60.0
BOUND_MARGIN = 1.05
FIRST_SETTLE_CHECK = 20
SETTLE_CHECK_EVERY = 4
NEG_INF = float("-inf")
INT_MIN = -2 ** 31


def _head_rms(xh, seg, g):
    sq = xh * xh
    hi = sq.astype(BF16)
    lo = (sq - hi.astype(F32)).astype(BF16)
    ms = (jnp.dot(hi, seg, preferred_element_type=F32)
          + jnp.dot(lo, seg, preferred_element_type=F32))
    return xh * lax.rsqrt(ms + EPS) * g


def _rope(xh, cos, sa, sb):
    w = xh.shape[-1]
    return xh * cos + pltpu.roll(xh, w - ROPE_HALF, 1) * sa + pltpu.roll(xh, ROPE_HALF, 1) * sb


def _tile_lanes(t, width):
    reps = width // t.shape[-1]
    return t if reps == 1 else jnp.concatenate([t] * reps, axis=1)


def _project_body(emit_vn, emit_vt, x_ref, ng_ref, w_ref, qg_ref, kg_ref, lng_ref, lnb_ref, mixw_ref,
                  bias_ref, cos_ref, sa_ref, sb_ref, seg_ref, *outs):
    q_o, k_o, v_o, kb_o, qi_o, kiwi_o, kib_o, ga_o, gm_o = outs[:9]
    extra = list(outs[9:])
    vn_o = extra.pop(0) if emit_vn else None
    vt_o, qt_o, qit_o = extra if emit_vt else (None, None, None)

    def store_cols(dst, xq):
        for blk in range(xq.shape[0] // Q_BLOCK):
            xt = xq[blk * Q_BLOCK:(blk + 1) * Q_BLOCK].T.astype(BF16)
            for p in range(xq.shape[1] // (2 * HEAD_DIM)):
                lo = 2 * p * HEAD_DIM
                dst[blk, p] = jnp.concatenate([xt[lo:lo + HEAD_DIM], xt[lo + HEAD_DIM:lo + 2 * HEAD_DIM]],
                                              axis=1)
    x = x_ref[...]
    h = x * lax.rsqrt(jnp.mean(x * x, axis=-1, keepdims=True) + EPS) * ng_ref[...]
    hb = h.astype(BF16)

    def proj(c0, c1):
        return jnp.dot(hb, w_ref[:, c0:c1], preferred_element_type=F32)

    cos, sa, sb = cos_ref[...], sa_ref[...], sb_ref[...]
    seg = seg_ref[...]

    q = _head_rms(proj(C_Q, C_K), seg, qg_ref[...])
    q = _rope(q, _tile_lanes(cos, ATT_WIDTH), _tile_lanes(sa, ATT_WIDTH), _tile_lanes(sb, ATT_WIDTH))
    q = q * (HEAD_DIM ** -0.5 * LOG2_E)
    q_o[...] = q.astype(BF16)
    if emit_vt:
        store_cols(qt_o, q)

    k = _head_rms(proj(C_K, C_V), seg[:KV_WIDTH, :KV_WIDTH], kg_ref[...])
    k = _rope(k, _tile_lanes(cos, KV_WIDTH), _tile_lanes(sa, KV_WIDTH), _tile_lanes(sb, KV_WIDTH))
    k_o[...] = k
    kb_o[...] = k.astype(BF16)

    v = proj(C_V, C_QI)
    v_o[...] = v
    if emit_vt:
        vt_o[...] = v.T.astype(BF16)

    qi = _rope(proj(C_QI, C_GA), _tile_lanes(cos, IDX_WIDTH), _tile_lanes(sa, IDX_WIDTH),
               _tile_lanes(sb, IDX_WIDTH))
    qi_o[...] = qi.astype(BF16)
    if emit_vt:
        store_cols(qit_o, qi)

    lane = lax.broadcasted_iota(I32, cos.shape, 1)
    is_ki = lane < D_IDX
    wi_scale = jnp.where(lane < D_IDX + N_IDX_HEADS, IDX_WIDTH ** -0.5, 1.0)
    kiwi = _rope(proj(C_KIWI, C_END), jnp.where(is_ki, cos, wi_scale),
                 jnp.where(is_ki, sa, 0.0), jnp.where(is_ki, sb, 0.0))
    kiwi_o[...] = kiwi
    kib_o[...] = kiwi.astype(BF16)

    ga_o[...] = jax.nn.silu(proj(C_GA, C_U))

    vmg = jax.nn.gelu(proj(C_VM, C_GM))
    xc = vmg - jnp.mean(vmg, axis=-1, keepdims=True)
    vn = xc * lax.rsqrt(jnp.mean(xc * xc, axis=-1, keepdims=True) + EPS) * lng_ref[...] + lnb_ref[...]
    if emit_vn:
        vn_o[...] = vn

    lane_c = lax.broadcasted_iota(I32, (CHUNK, LANES), 1)
    rows = x.shape[0]
    chunks = []
    for c in range(rows // CHUNK):
        pieces = []
        for p in range(GM_GROUPS // 2):
            t = vn[c * CHUNK:(c + 1) * CHUNK, p * LANES:(p + 1) * LANES]
            rhs = jnp.concatenate([jnp.where(lane_c < 64, t, 0.0), jnp.where(lane_c >= 64, t, 0.0)],
                                  axis=0).astype(BF16)
            pieces.append(jnp.dot(mixw_ref[p], rhs, preferred_element_type=F32))
        chunks.append(jnp.concatenate(pieces, axis=1) + bias_ref[...])
    s = jnp.concatenate(chunks, axis=0)
    gm_o[...] = jax.nn.gelu(proj(C_U, C_VM)) * s * jax.nn.silu(proj(C_GM, C_KIWI))


def _project(x2, pos_tables, mixw, bias, weights, emit_vn, emit_vt):
    rows = x2.shape[0]
    tm = min(ROW_TILE, rows)
    assert rows % tm == 0 and tm % CHUNK == 0
    ng, w, qg, kg, lng, lnb, seg = weights
    cos, sa, sb = pos_tables
    assert cos.shape[0] % tm == 0
    n_pos_blocks = cos.shape[0] // tm

    def row_map(i):
        return (i, 0)

    def pos_map(i):
        return (i % n_pos_blocks, 0)

    def const(i):
        return (0, 0)

    def rspec(width):
        return pl.BlockSpec((tm, width), row_map)

    def cspec(a):
        if a.ndim == 3:
            return pl.BlockSpec(a.shape, lambda i: (0, 0, 0))
        return pl.BlockSpec(a.shape, const)

    out_shapes = [
        jax.ShapeDtypeStruct((rows, ATT_WIDTH), BF16),
        jax.ShapeDtypeStruct((rows, KV_WIDTH), F32),
        jax.ShapeDtypeStruct((rows, KV_WIDTH), F32),
        jax.ShapeDtypeStruct((rows, KV_WIDTH), BF16),
        jax.ShapeDtypeStruct((rows, IDX_WIDTH), BF16),
        jax.ShapeDtypeStruct((rows, LANES), F32),
        jax.ShapeDtypeStruct((rows, LANES), BF16),
        jax.ShapeDtypeStruct((rows, ATT_WIDTH), F32),
        jax.ShapeDtypeStruct((rows, GM_WIDTH), F32),
    ]
    if emit_vn:
        out_shapes.append(jax.ShapeDtypeStruct((rows, GM_WIDTH), F32))
    out_specs = [rspec(s.shape[1]) for s in out_shapes]
    if emit_vt:
        assert tm == KEY_TILE
        out_shapes.append(jax.ShapeDtypeStruct((rows // tm, KV_WIDTH, tm), BF16))
        out_specs.append(pl.BlockSpec((None, KV_WIDTH, tm), lambda i: (i, 0, 0)))
        for width in (ATT_WIDTH, IDX_WIDTH):
            pairs = width // (2 * HEAD_DIM)
            out_shapes.append(jax.ShapeDtypeStruct((rows // Q_BLOCK, pairs, HEAD_DIM, 2 * Q_BLOCK), BF16))
            out_specs.append(pl.BlockSpec((tm // Q_BLOCK, pairs, HEAD_DIM, 2 * Q_BLOCK),
                                          lambda i: (i, 0, 0, 0)))
    in_specs = [rspec(x2.shape[1]), cspec(ng), cspec(w), cspec(qg), cspec(kg), cspec(lng), cspec(lnb),
                cspec(mixw), cspec(bias),
                pl.BlockSpec((tm, LANES), pos_map), pl.BlockSpec((tm, LANES), pos_map),
                pl.BlockSpec((tm, LANES), pos_map), cspec(seg)]
    return pl.pallas_call(
        functools.partial(_project_body, emit_vn, emit_vt),
        grid=(rows // tm,),
        in_specs=in_specs,
        out_specs=out_specs,
        out_shape=out_shapes,
        compiler_params=pltpu.CompilerParams(dimension_semantics=("arbitrary",),
                                             vmem_limit_bytes=VMEM_LIMIT),
        name="project",
    )(x2, ng, w, qg, kg, lng, lnb, mixw, bias, cos, sa, sb, seg)


def _key_to_float(u):
    bits = jnp.where(u < 0, u ^ I32(INT_MIN), ~u)
    return lax.bitcast_convert_type(bits, F32)


def _count(sc_ref, n_tiles, key_axis, pred, unroll=1, tail=()):
    a, b = sc_ref.shape[1], sc_ref.shape[2]

    def fold(ind):
        if key_axis == 1:
            return ind
        parts = [ind[i * SUBLANES:(i + 1) * SUBLANES] for i in range(a // SUBLANES)]
        while len(parts) > 1:
            parts = [parts[i] + parts[i + 1] for i in range(0, len(parts), 2)]
        return parts[0]

    def tiles(j0, n, c):
        for u in range(n):
            c = c + fold(jnp.where(pred(sc_ref[j0 + u], j0 + u), 1.0, 0.0))
        return c

    init = jnp.zeros((a, b) if key_axis == 1 else (SUBLANES, b), F32)
    n_steps = n_tiles // unroll
    c = lax.fori_loop(0, n_steps, lambda i, c: tiles(i * unroll, unroll, c), init)
    done = n_steps * unroll
    for size in tail:
        fits = n_tiles - done >= size
        c = lax.cond(fits, lambda c, j0=done, n=size: tiles(j0, n, c), lambda c: c, c)
        done = done + jnp.where(fits, size, 0)
    return jnp.sum(c, axis=key_axis, keepdims=True)


def _select_threshold(sc_ref, n_tiles, k_row, key_axis, unroll=1, tail=()):
    k_f = k_row.astype(F32)

    def bit_step(i, carry):
        res, cnt_res = carry
        cand = res | lax.shift_left(I32(1), I32(31) - i)
        t = _key_to_float(cand)
        cnt = _count(sc_ref, n_tiles, key_axis, lambda x, j: x >= t, unroll, tail)
        ok = cnt >= k_f
        return jnp.where(ok, cand, res), jnp.where(ok, cnt, cnt_res)

    state = (jnp.zeros(k_row.shape, I32), jnp.zeros(k_row.shape, F32))
    state = lax.fori_loop(0, FIRST_SETTLE_CHECK, bit_step, state)
    for lo in range(FIRST_SETTLE_CHECK, 32, SETTLE_CHECK_EVERY):
        settled = jnp.min(jnp.where(state[1] == k_f, 1.0, 0.0)) > 0.0
        state = lax.cond(settled, lambda s: s,
                         lambda s, lo=lo: lax.fori_loop(lo, lo + SETTLE_CHECK_EVERY, bit_step, s), state)
    res, cnt_res = state
    thr = _key_to_float(res)
    _drop_surplus_ties(sc_ref, n_tiles, thr, k_f, cnt_res, key_axis)
    return thr


def _drop_surplus_ties(sc_ref, n_tiles, thr, k_f, cnt_ge, key_axis):
    t = sc_ref.shape[1 + key_axis]
    surplus = jnp.max(cnt_ge - k_f)

    @pl.when(surplus > 0.0)
    def _():
        n_take = k_f - _count(sc_ref, n_tiles, key_axis, lambda x, j: x > thr)
        r = lax.broadcasted_iota(I32, (t, t), 0)
        c = lax.broadcasted_iota(I32, (t, t), 1)
        tri = jnp.where((r >= c) if key_axis == 0 else (r <= c), 1.0, 0.0).astype(BF16)

        def body(j, before):
            x = sc_ref[j]
            tied = jnp.where(x == thr, 1.0, 0.0)
            if key_axis == 0:
                upto = jnp.dot(tri, tied.astype(BF16), preferred_element_type=F32)
                total = upto[t - 1:t]
            else:
                upto = jnp.dot(tied.astype(BF16), tri, preferred_element_type=F32)
                total = upto[:, t - 1:t]
            rank = tied * (before + upto)
            sc_ref[j] = jnp.where(rank > n_take, NEG_INF, x)
            return before + total

        lax.fori_loop(0, n_tiles, body, jnp.zeros(thr.shape, F32))


def _attend_init(m_ref, l_ref, acc_ref):
    m_ref[...] = jnp.full(m_ref.shape, NEG_INF, F32)
    l_ref[...] = jnp.zeros(l_ref.shape, F32)
    acc_ref[...] = jnp.zeros(acc_ref.shape, F32)


def _softmax_step(s, m_prev, l_prev):
    m_new = jnp.maximum(m_prev, jnp.max(s, axis=1, keepdims=True))
    m_safe = jnp.where(m_new == NEG_INF, 0.0, m_new)
    alpha = jnp.exp2(m_prev - m_safe)
    p = jnp.exp2(s - _tile_lanes(m_safe, s.shape[1]))
    l_new = alpha * l_prev + jnp.sum(p, axis=1, keepdims=True)
    return p, m_new, l_new, alpha


_NT = (((1,), (1,)), ((), ()))


def _two_stage(n_tiles, produce, consume, carry, buf_a, buf_b):
    produce(0, buf_a)

    def pair(j0, c, last):
        produce(j0 + 1, buf_b)
        c = consume(j0, buf_a, c)
        if not last:
            produce(jnp.minimum(j0 + 2, n_tiles - 1), buf_a)
        return consume(j0 + 1, buf_b, c)

    def pairs(j0, n, c):
        for q in range(n):
            c = pair(j0 + 2 * q, c, False)
        return c

    n_steps = n_tiles // 8
    carry = lax.fori_loop(0, n_steps, lambda i, c: pairs(8 * i, 4, c), carry)
    left = n_tiles - 8 * n_steps
    carry = lax.cond(left >= 4, lambda c: pairs(8 * n_steps, 2, c), lambda c: c, carry)
    return lax.cond(left % 4 == 2, lambda c: pair(n_tiles - 2, c, True), lambda c: c, carry)


def _prompt_attend_body(k_top, bound_ref, qt_ref, qit_ref, kiwi_ref, kb_ref, vt_ref,
                        kib_ref, o_ref, sc_ref, acc_ref, buf_a, buf_b, qbdt_ref, qiht_ref):
    qb = pl.program_id(1)

    @pl.when((pl.program_id(0) == 0) & (qb == 0))
    def _():
        qbdt_ref[...] = jnp.zeros(qbdt_ref.shape, BF16)
        qiht_ref[...] = jnp.zeros(qiht_ref.shape, BF16)

    for g in range(N_KV_HEADS):
        qbdt_ref[g, g * HEAD_DIM:(g + 1) * HEAD_DIM, :] = qt_ref[g]
        qiht_ref[g, :D_IDX, :] = qit_ref[g]
    n_tiles = 2 * ((qb * Q_BLOCK + Q_BLOCK + 2 * KEY_TILE - 1) // (2 * KEY_TILE))
    qpos = qb * Q_BLOCK + lax.broadcasted_iota(I32, (1, Q_BLOCK), 1)

    w_t = kiwi_ref[...].T[D_IDX:D_IDX + N_IDX_HEADS, :]
    w_pairs = [jnp.concatenate([w_t[2 * p:2 * p + 1], w_t[2 * p + 1:2 * p + 2]], axis=1)
               for p in range(N_IDX_HEADS // 2)]
    krow = lax.broadcasted_iota(I32, (KEY_TILE, Q_BLOCK), 0)

    def score_dots(j, buf):
        start = pl.multiple_of(j * KEY_TILE, KEY_TILE)
        ki_t = kib_ref[pl.ds(start, KEY_TILE), :]
        for p in range(N_IDX_HEADS // 2):
            buf[p] = jnp.dot(ki_t, qiht_ref[p], preferred_element_type=F32)

    def score_sum(j, buf, c):
        acc = jnp.zeros((KEY_TILE, Q_BLOCK), F32)
        for p in range(N_IDX_HEADS // 2):
            d = jnp.maximum(buf[p], 0.0) * w_pairs[p]
            acc = acc + d[:, :Q_BLOCK] + d[:, Q_BLOCK:]
        sc_ref[j] = jnp.where(j * KEY_TILE + krow <= qpos, acc, NEG_INF)
        return c

    _two_stage(n_tiles, score_dots, score_sum, 0, buf_a, buf_b)

    k_row = jnp.minimum(k_top, qpos + 1)
    thr = _select_threshold(sc_ref, n_tiles, k_row, key_axis=0, unroll=8, tail=(4, 2))
    thr2 = jnp.concatenate([thr, thr], axis=1)

    acc_ref[...] = jnp.zeros(acc_ref.shape, F32)
    m0 = tuple(jnp.full((1, 2 * Q_BLOCK), NEG_INF, F32) for _ in range(N_KV_HEADS))
    l0 = tuple(jnp.zeros((1, 2 * Q_BLOCK), F32) for _ in range(N_KV_HEADS))

    ones_rows = jnp.ones((BF16_SUBLANES, KEY_TILE), BF16)

    def logits(j, buf):
        start = pl.multiple_of(j * KEY_TILE, KEY_TILE)
        k_t = kb_ref[pl.ds(start, KEY_TILE), :]
        for g in range(N_KV_HEADS):
            buf[g] = jnp.dot(k_t, qbdt_ref[g], preferred_element_type=F32)

    def weighted_values(v_t, g, p):
        lhs = jnp.concatenate([v_t[g * HEAD_DIM:(g + 1) * HEAD_DIM, :], ones_rows], axis=0)
        pv = jnp.dot(lhs, p, preferred_element_type=F32)
        return pv[:HEAD_DIM], pv[HEAD_DIM:HEAD_DIM + 1]

    def softmax_pv(j, buf, carry):
        ms, ls = carry
        v_t = vt_ref[j]
        sc = sc_ref[j]
        keep = jnp.concatenate([sc, sc], axis=1) >= thr2
        new_m, new_l = [], []
        for g in range(N_KV_HEADS):
            s = jnp.where(keep, buf[g], NEG_INF)
            m_new = jnp.maximum(ms[g], jnp.max(s, axis=0, keepdims=True))
            m_safe = jnp.where(m_new == NEG_INF, 0.0, m_new)
            alpha = jnp.exp2(ms[g] - m_safe)
            pv, psum = weighted_values(v_t, g, jnp.exp2(s - m_safe).astype(BF16))
            new_m.append(m_new)
            new_l.append(alpha * ls[g] + psum)
            acc_ref[g] = acc_ref[g] * alpha + pv
        return tuple(new_m), tuple(new_l)

    def exp_pv(j, buf, ls):
        v_t = vt_ref[j]
        sc = sc_ref[j]
        keep = jnp.concatenate([sc, sc], axis=1) >= thr2
        new_l = []
        for g in range(N_KV_HEADS):
            pv, psum = weighted_values(v_t, g, jnp.exp2(jnp.where(keep, buf[g], NEG_INF)).astype(BF16))
            new_l.append(ls[g] + psum)
            acc_ref[g] = acc_ref[g] + pv
        return tuple(new_l)

    def write_out(ls):
        blocks = []
        for g in range(N_KV_HEADS):
            og = acc_ref[g] / ls[g]
            blocks += [og[:, :Q_BLOCK], og[:, Q_BLOCK:]]
        o_ref[...] = jnp.concatenate(blocks, axis=0).T

    bounded = bound_ref[0] < MAX_UNSHIFTED_LOG2

    @pl.when(bounded)
    def _():
        write_out(_two_stage(n_tiles, logits, exp_pv, l0, buf_a, buf_b))

    @pl.when(jnp.logical_not(bounded))
    def _():
        write_out(_two_stage(n_tiles, logits, softmax_pv, (m0, l0), buf_a, buf_b)[1])


def _prompt_attend(logit_bound, qt, qit, kiwi, kb, vt, kib, k_top):
    bsz, nqb = qt.shape[:2]
    seq = kb.shape[1]
    assert N_IDX_HEADS // 2 == N_KV_HEADS
    resident = dict(pipeline_mode=pl.Buffered(1))
    return pl.pallas_call(
        functools.partial(_prompt_attend_body, k_top),
        grid=(bsz, nqb),
        in_specs=[
            pl.BlockSpec(memory_space=pltpu.SMEM),
            pl.BlockSpec((None, None, N_KV_HEADS, HEAD_DIM, 2 * Q_BLOCK), lambda b, i: (b, i, 0, 0, 0)),
            pl.BlockSpec((None, None, N_IDX_HEADS // 2, D_IDX, 2 * Q_BLOCK), lambda b, i: (b, i, 0, 0, 0)),
            pl.BlockSpec((Q_BLOCK, LANES), lambda b, i: (b * nqb + i, 0)),
            pl.BlockSpec((None, seq, KV_WIDTH), lambda b, i: (b, 0, 0), **resident),
            pl.BlockSpec((None, seq // KEY_TILE, KV_WIDTH, KEY_TILE), lambda b, i: (b, 0, 0, 0), **resident),
            pl.BlockSpec((None, seq, LANES), lambda b, i: (b, 0, 0), **resident),
        ],
        out_specs=pl.BlockSpec((Q_BLOCK, ATT_WIDTH), lambda b, i: (b * nqb + i, 0)),
        out_shape=jax.ShapeDtypeStruct((bsz * seq, ATT_WIDTH), F32),
        scratch_shapes=[
            pltpu.VMEM((seq // KEY_TILE, KEY_TILE, Q_BLOCK), F32),
            pltpu.VMEM((N_KV_HEADS, HEAD_DIM, 2 * Q_BLOCK), F32),
            pltpu.VMEM((N_KV_HEADS, KEY_TILE, 2 * Q_BLOCK), F32),
            pltpu.VMEM((N_KV_HEADS, KEY_TILE, 2 * Q_BLOCK), F32),
            pltpu.VMEM((N_KV_HEADS, KV_WIDTH, 2 * Q_BLOCK), BF16),
            pltpu.VMEM((N_IDX_HEADS // 2, LANES, 2 * Q_BLOCK), BF16),
        ],
        compiler_params=pltpu.CompilerParams(dimension_semantics=("arbitrary", "arbitrary"),
                                             vmem_limit_bytes=VMEM_LIMIT),
        name="prompt_attend",
    )(logit_bound, qt, qit, kiwi, kb, vt, kib)


def _sample_attend_body(k_top, n_pages, t_new,
                        pt_ref, qbd_ref, qih_ref, w_ref, knew_ref, vnew_ref, kinew_ref,
                        cik_hbm, ck_hbm, cv_hbm, o_ref,
                        ibuf, kbuf, vbuf, sem, sc_ref, m_ref, l_ref, acc_ref):
    b = pl.program_id(0)
    nb = pl.num_programs(0)
    slot = b % 2
    cp = SAMPLE_CHUNK_PAGES
    n_chunks = n_pages // cp
    width = cp * PAGE_SIZE

    def page_copies(bb, s, p):
        page = pt_ref[bb, p]
        return (pltpu.make_async_copy(cik_hbm.at[page], ibuf.at[s, p], sem.at[s, 0]),
                pltpu.make_async_copy(ck_hbm.at[page], kbuf.at[s, p], sem.at[s, 1]),
                pltpu.make_async_copy(cv_hbm.at[page], vbuf.at[s, p], sem.at[s, 2]))

    def start_fetch(bb, s):
        def body(p, _):
            for c in page_copies(bb, s, p):
                c.start()
            return 0
        lax.fori_loop(0, n_pages, body, 0)

    def wait_fetch(bb, s):
        def body(p, _):
            for c in page_copies(bb, s, p):
                c.wait()
            return 0
        lax.fori_loop(0, n_pages, body, 0)

    @pl.when(b == 0)
    def _():
        start_fetch(b, slot)

    @pl.when(b + 1 < nb)
    def _():
        start_fetch(b + 1, 1 - slot)

    wait_fetch(b, slot)

    def chunk(buf, c):
        return jnp.concatenate([buf[slot, c * cp + i] for i in range(cp)], axis=1).astype(BF16)

    def new_cols(a):
        padded = jnp.concatenate([a, jnp.zeros((LANES - a.shape[0], a.shape[1]), a.dtype)], axis=0)
        return padded.T.astype(BF16)

    qih = qih_ref[...]
    w = _tile_lanes(w_ref[...], width)

    def head_sum(ki_c):
        d = jnp.dot(qih, ki_c, preferred_element_type=F32)
        d = w[:, :ki_c.shape[1]] * jnp.maximum(d, 0.0)
        acc = d[0:t_new]
        for h in range(1, N_IDX_HEADS):
            acc = acc + d[h * t_new:(h + 1) * t_new]
        return acc

    def score_chunk(c, _):
        sc_ref[c] = head_sum(chunk(ibuf, c))
        return 0

    lax.fori_loop(0, n_chunks, score_chunk, 0)

    tq = lax.broadcasted_iota(I32, (t_new, LANES), 0)
    col = lax.broadcasted_iota(I32, (t_new, LANES), 1)
    sc_new = jnp.where((col < t_new) & (col <= tq), head_sum(new_cols(kinew_ref[...])[:D_IDX]), NEG_INF)
    sc_ref[n_chunks] = jnp.concatenate([sc_new, jnp.full((t_new, width - LANES), NEG_INF, F32)], axis=1)

    qpos = n_pages * PAGE_SIZE + lax.broadcasted_iota(I32, (t_new, 1), 0)
    k_row = jnp.minimum(k_top, qpos + 1)
    thr = _select_threshold(sc_ref, n_chunks + 1, k_row, key_axis=1, unroll=n_chunks + 1)

    _attend_init(m_ref, l_ref, acc_ref)
    qbd = qbd_ref[...]

    def attend(sc, k_c, v_c):
        bias = jnp.where(sc >= thr, 0.0, NEG_INF)
        s = jnp.dot(qbd, k_c, preferred_element_type=F32) + jnp.concatenate([bias] * N_HEADS, axis=0)
        p, m_new, l_new, alpha = _softmax_step(s, m_ref[...], l_ref[...])
        m_ref[...] = m_new
        l_ref[...] = l_new
        acc_ref[...] = (acc_ref[...] * _tile_lanes(alpha, KV_WIDTH)
                        + lax.dot_general(p.astype(BF16), v_c, _NT, preferred_element_type=F32))

    def attend_chunk(c, _):
        attend(sc_ref[c], chunk(kbuf, c), chunk(vbuf, c))
        return 0

    lax.fori_loop(0, n_chunks, attend_chunk, 0)
    attend(sc_ref[n_chunks][:, :LANES], new_cols(knew_ref[...]), new_cols(vnew_ref[...]))

    o_ref[...] = acc_ref[...] / _tile_lanes(l_ref[...], KV_WIDTH)


def _sample_attend(page_table, qbd, qih, w_rows, knew, vnew, kinew, cik_t, ck_t, cv_t, k_top):
    dbsz, n_pages = page_table.shape
    t_new = knew.shape[1]
    rows = N_HEADS * t_new
    past = n_pages * PAGE_SIZE
    assert n_pages % SAMPLE_CHUNK_PAGES == 0
    width = SAMPLE_CHUNK_PAGES * PAGE_SIZE
    n_chunks = n_pages // SAMPLE_CHUNK_PAGES

    def bmap(b, pt):
        return (b, 0, 0)

    grid_spec = pltpu.PrefetchScalarGridSpec(
        num_scalar_prefetch=1,
        grid=(dbsz,),
        in_specs=[
            pl.BlockSpec((None, rows, KV_WIDTH), bmap),
            pl.BlockSpec((None, rows, D_IDX), bmap),
            pl.BlockSpec((None, rows, LANES), bmap),
            pl.BlockSpec((None, t_new, KV_WIDTH), bmap),
            pl.BlockSpec((None, t_new, KV_WIDTH), bmap),
            pl.BlockSpec((None, t_new, LANES), bmap),
            pl.BlockSpec(memory_space=pl.ANY),
            pl.BlockSpec(memory_space=pl.ANY),
            pl.BlockSpec(memory_space=pl.ANY),
        ],
        out_specs=pl.BlockSpec((None, rows, KV_WIDTH), bmap),
        scratch_shapes=[
            pltpu.VMEM((2, n_pages, D_IDX, PAGE_SIZE), F32),
            pltpu.VMEM((2, n_pages, KV_WIDTH, PAGE_SIZE), F32),
            pltpu.VMEM((2, n_pages, KV_WIDTH, PAGE_SIZE), F32),
            pltpu.SemaphoreType.DMA((2, 3)),
            pltpu.VMEM((n_chunks + 1, t_new, width), F32),
            pltpu.VMEM((rows, LANES), F32),
            pltpu.VMEM((rows, LANES), F32),
            pltpu.VMEM((rows, KV_WIDTH), F32),
        ],
    )
    return pl.pallas_call(
        functools.partial(_sample_attend_body, k_top, n_pages, t_new),
        grid_spec=grid_spec,
        out_shape=jax.ShapeDtypeStruct((dbsz, rows, KV_WIDTH), F32),
        compiler_params=pltpu.CompilerParams(dimension_semantics=("arbitrary",),
                                             vmem_limit_bytes=VMEM_LIMIT),
        name="sample_attend",
    )(page_table, qbd, qih, w_rows, knew, vnew, kinew, cik_t, ck_t, cv_t)


def _finish_body(x_ref, o_ref, ga_ref, gm_ref, p_ref, wo_ref, pg_ref, wg_ref, wp_ref, y_ref):
    att = (o_ref[...] * ga_ref[...]).astype(BF16)
    r = (x_ref[...]
         + jnp.dot(att, wo_ref[:ATT_WIDTH, :], preferred_element_type=F32)
         + jnp.dot(gm_ref[...].astype(BF16), wo_ref[ATT_WIDTH:, :], preferred_element_type=F32))
    rn = r * lax.rsqrt(jnp.mean(r * r, axis=-1, keepdims=True) + EPS) * pg_ref[...]
    gate = jax.nn.sigmoid(jnp.dot(rn.astype(BF16), wg_ref[...], preferred_element_type=F32))
    y_ref[...] = r + gate * jnp.dot(p_ref[...].astype(BF16), wp_ref[...], preferred_element_type=F32)


def _finish(x2, o, ga, gm, p2, wo, pg, wg, wp):
    rows, d_model = x2.shape
    tm = min(ROW_TILE, rows)

    def rspec(a):
        return pl.BlockSpec((tm, a.shape[1]), lambda i: (i, 0))

    def cspec(a):
        return pl.BlockSpec(a.shape, lambda i: (0, 0))

    return pl.pallas_call(
        _finish_body,
        grid=(rows // tm,),
        in_specs=[rspec(x2), rspec(o), rspec(ga), rspec(gm), rspec(p2),
                  cspec(wo), cspec(pg), cspec(wg), cspec(wp)],
        out_specs=pl.BlockSpec((tm, d_model), lambda i: (i, 0)),
        out_shape=jax.ShapeDtypeStruct((rows, d_model), F32),
        compiler_params=pltpu.CompilerParams(dimension_semantics=("arbitrary",),
                                             vmem_limit_bytes=VMEM_LIMIT),
        name="finish",
    )(x2, o, ga, gm, p2, wo, pg, wg, wp)


def _rope_tables(pos):
    inv = jnp.power(ROPE_THETA, -jnp.arange(ROPE_HALF, dtype=F32) * 2.0 / ROPE_ROT)
    ang = pos.astype(F32)[:, None] * inv[None, :]
    cos, sin = jnp.cos(ang), jnp.sin(ang)
    n = pos.shape[0]
    one = jnp.ones((n, HEAD_DIM - ROPE_ROT), F32)
    zero8 = jnp.zeros((n, ROPE_HALF), F32)
    zero = jnp.zeros((n, HEAD_DIM - ROPE_ROT), F32)
    cos_h = jnp.concatenate([cos, cos, one], axis=1)
    sa_h = jnp.concatenate([-sin, zero8, zero], axis=1)
    sb_h = jnp.concatenate([zero8, sin, zero], axis=1)
    return tuple(jnp.concatenate([t, t], axis=1) for t in (cos_h, sa_h, sb_h))


def _pair_mix(ws):
    g = ws.shape[0]
    return ws.reshape(g // 2, 2, CHUNK, CHUNK).transpose(0, 2, 1, 3).reshape(g // 2, CHUNK, 2 * CHUNK).astype(BF16)


def _group_diag(q, lead):
    t = q.shape[-2]
    n = len(lead)
    q6 = q.reshape(*lead, t, N_KV_HEADS, 2, HEAD_DIM)
    q6 = jnp.moveaxis(q6, n, n + 2)
    q6 = q6.reshape(*lead, N_KV_HEADS, 2 * t, HEAD_DIM)
    eye = jnp.eye(N_KV_HEADS, dtype=q.dtype)
    qd = q6[..., None, :] * eye[:, None, :, None]
    return qd.reshape(*lead, N_KV_HEADS, 2 * t, KV_WIDTH)


def kernel(x_prompt, x_sample, cache_k, cache_v, cache_idx_k, page_table, p_prompt, p_sample,
           norm_in_g, w_in, q_norm_g, k_norm_g, ln_v_g, ln_v_b, w_s, b_s, w_out,
           ple_norm_g, w_ple_gate, w_ple_proj):
    depth = w_in.shape[0]
    assert depth == 1
    bsz, seq, d_model = x_prompt.shape
    dbsz, t_new, _ = x_sample.shape
    n_pages = page_table.shape[1]
    past = n_pages * PAGE_SIZE
    assert seq % KEY_TILE == 0 and CHUNK % t_new == 0 and (dbsz * t_new) % CHUNK == 0

    perm = np.concatenate([np.arange(0, _ORIG_KI), np.arange(_ORIG_GA, _ORIG_END),
                           np.arange(_ORIG_KI, _ORIG_GA)])
    w = jnp.pad(w_in[0][:, perm], ((0, 0), (0, C_END - _ORIG_END))).astype(BF16)
    seg = jnp.asarray(np.kron(np.eye(N_HEADS), np.full((HEAD_DIM, HEAD_DIM), 1.0 / HEAD_DIM)), BF16)
    weights = (norm_in_g[0][None, :], w, jnp.tile(q_norm_g[0], N_HEADS)[None, :],
               jnp.tile(k_norm_g[0], N_KV_HEADS)[None, :], ln_v_g[0][None, :], ln_v_b[0][None, :], seg)
    ws_tril = jnp.where(jnp.tril(jnp.ones((CHUNK, CHUNK), bool))[None], w_s[0], 0.0)
    bias_p = jnp.repeat(b_s[0].T, GM_WIDTH // GM_GROUPS, axis=1)
    reps = CHUNK // t_new
    ws_s = jnp.einsum("ab,gij->gaibj", jnp.eye(reps, dtype=F32),
                      ws_tril[:, :t_new, :t_new]).reshape(GM_GROUPS, CHUNK, CHUNK)
    bias_s = jnp.tile(bias_p[:t_new], (reps, 1))
    wo = w_out[0].astype(BF16)
    wg = w_ple_gate[0].astype(BF16)
    wp = w_ple_proj[0].astype(BF16)
    pg = ple_norm_g[0][None, :]

    xp = x_prompt.reshape(bsz * seq, d_model)
    (_, k, v, kb, _, kiwi, kib, ga, gm, vt, qt, qit) = _project(
        xp, _rope_tables(jnp.arange(seq)), _pair_mix(ws_tril), bias_p, weights,
        emit_vn=False, emit_vt=True)
    nqb = seq // Q_BLOCK
    k_top_p = min(TOPK_MAX, seq // 4)
    qt = qt.reshape(bsz, nqb, N_KV_HEADS, HEAD_DIM, 2 * Q_BLOCK)
    qit = qit.reshape(bsz, nqb, N_IDX_HEADS // 2, D_IDX, 2 * Q_BLOCK)
    logit_bound = (HEAD_DIM ** 0.5 * LOG2_E * BOUND_MARGIN
                   * jnp.max(jnp.abs(q_norm_g[0])) * jnp.max(jnp.abs(k_norm_g[0]))).reshape(1)
    o = _prompt_attend(logit_bound, qt, qit, kiwi, kb.reshape(bsz, seq, KV_WIDTH),
                       vt.reshape(bsz, seq // KEY_TILE, KV_WIDTH, KEY_TILE),
                       kib.reshape(bsz, seq, LANES), k_top_p)
    y_prompt = _finish(xp, o, ga, gm, p_prompt[0].reshape(bsz * seq, -1), wo, pg, wg, wp)

    xs = x_sample.reshape(dbsz * t_new, d_model)
    pos_s = past + jnp.arange(min(ROW_TILE, dbsz * t_new)) % t_new
    (q, k_s, v_s, _, qi, kiwi_s, _, ga, gm, vn_s) = _project(
        xs, _rope_tables(pos_s), _pair_mix(ws_s), bias_s, weights, emit_vn=True, emit_vt=False)
    k_top_s = min(TOPK_MAX, (past + t_new) // 4)
    qbd = _group_diag(q.reshape(dbsz, t_new, ATT_WIDTH), (dbsz,)).reshape(dbsz, N_HEADS * t_new, KV_WIDTH)
    qih = qi.reshape(dbsz, t_new, N_IDX_HEADS, D_IDX).transpose(0, 2, 1, 3).reshape(
        dbsz, N_IDX_HEADS * t_new, D_IDX)
    kiwi3 = kiwi_s.reshape(dbsz, t_new, LANES)
    w_rows = kiwi3[:, :, D_IDX:D_IDX + N_IDX_HEADS].transpose(0, 2, 1).reshape(dbsz, N_IDX_HEADS * t_new, 1)
    w_rows = jnp.broadcast_to(w_rows, (dbsz, N_IDX_HEADS * t_new, LANES))
    acc = _sample_attend(page_table, qbd, qih, w_rows,
                         k_s.reshape(dbsz, t_new, KV_WIDTH), v_s.reshape(dbsz, t_new, KV_WIDTH), kiwi3,
                         cache_idx_k[0].transpose(0, 2, 1),
                         cache_k[0].transpose(0, 2, 3, 1).reshape(-1, KV_WIDTH, PAGE_SIZE),
                         cache_v[0].transpose(0, 2, 3, 1).reshape(-1, KV_WIDTH, PAGE_SIZE), k_top_s)
    acc = acc.reshape(dbsz, N_KV_HEADS, 2, t_new, N_KV_HEADS, HEAD_DIM)
    o_s = jnp.stack([acc[:, g, :, :, g, :] for g in range(N_KV_HEADS)], axis=1)
    o_s = o_s.transpose(0, 3, 1, 2, 4).reshape(dbsz * t_new, ATT_WIDTH)
    y_sample = _finish(xs, o_s, ga, gm, p_sample[0].reshape(dbsz * t_new, -1), wo, pg, wg, wp)

    return (y_prompt.reshape(bsz, seq, d_model),
            y_sample.reshape(dbsz, t_new, d_model),
            k.reshape(1, bsz, seq, N_KV_HEADS, HEAD_DIM),
            v.reshape(1, bsz, seq, N_KV_HEADS, HEAD_DIM),
            kiwi[:, :D_IDX].reshape(1, bsz, seq, D_IDX),
            k_s.reshape(1, dbsz, t_new, N_KV_HEADS, HEAD_DIM),
            v_s.reshape(1, dbsz, t_new, N_KV_HEADS, HEAD_DIM),
            kiwi_s[:, :D_IDX].reshape(1, dbsz, t_new, D_IDX),
            vn_s.reshape(1, dbsz, t_new, GM_WIDTH))
```
